```python
import math
import jax
import jax.numpy as jnp
from jax import lax
import numpy as np

D_MODEL = 1024
BATCH = 8
SEQ = 4096
DEPTH = 2

HEAD_DIM = 64
MIX_WIDTH = D_MODEL
N_MIXERS = 4
GROUP_WIDTH = MIX_WIDTH // N_MIXERS
DIL_HEADS = GROUP_WIDTH // HEAD_DIM
DIL_PATTERNS = ((128, 1), (512, 4), (2048, 16))
DIL_MAX_WIN = 2048
Q_BLOCK = 128
NSA_HEADS = GROUP_WIDTH // HEAD_DIM
NSA_CMP_LEN = 32
NSA_CMP_STRIDE = 16
NSA_CMP_HIDDEN = 256
NSA_SEL_LEN = 64
NSA_TOPN = 16
NSA_WIN = 512
NSA_BRANCHES = 3
FORCE_BONUS = 1.0e4
GLA_HEADS = 4
GLA_DV = GROUP_WIDTH // GLA_HEADS
GLA_DK = GLA_DV // 2
GLA_RANK = 16
GLA_TAU = 16.0
GLA_CHUNK = 16
S5_GROUP = 16
S5_GROUPS = GROUP_WIDTH // S5_GROUP
S5_STATE = 64
D_FF = 2752
N_EXPERTS = 8
TOP_K = 2
D_FF_EXPERT = 3584
N_SOFTMAX_HEADS = DIL_HEADS + NSA_HEADS
EPS = 1e-6
NEG_INF = -1e30

PROJ_LAYOUT = (
    ("dil_q", GROUP_WIDTH), ("dil_k", GROUP_WIDTH), ("dil_v", GROUP_WIDTH),
    ("nsa_q", GROUP_WIDTH),
    ("nsa_k_cmp", HEAD_DIM), ("nsa_v_cmp", HEAD_DIM),
    ("nsa_k_slc", HEAD_DIM), ("nsa_v_slc", HEAD_DIM),
    ("nsa_k_win", HEAD_DIM), ("nsa_v_win", HEAD_DIM),
    ("nsa_gate", NSA_HEADS * NSA_BRANCHES),
    ("gla_q", GLA_HEADS * GLA_DK), ("gla_k", GLA_HEADS * GLA_DK), ("gla_v", GROUP_WIDTH),
    ("gla_a", GLA_RANK), ("gla_r", GROUP_WIDTH),
    ("s5_u", GROUP_WIDTH),
)
PROJ_WIDTH = sum(w for _, w in PROJ_LAYOUT)

kernel_name = "hybrid_parallel_mixer_block"


def rms_norm(x, g):
    xf = x.astype(jnp.float32)
    y = xf * lax.rsqrt(jnp.mean(xf * xf, axis=-1, keepdims=True) + EPS)
    return (y * g.astype(jnp.float32)).astype(x.dtype)


def to_heads(t, n):
    b, s, _ = t.shape
    return t.reshape(b, s, n, -1).transpose(0, 2, 1, 3)


def from_heads(t):
    b, h, s, d = t.shape
    return t.transpose(0, 2, 1, 3).reshape(b, s, h * d)


def split_proj(p):
    out = []
    o = 0
    for _, w in PROJ_LAYOUT:
        out.append(p[..., o:o + w])
        o += w
    return out


def alibi_slopes(n):
    return 2.0 ** (-8.0 * jnp.arange(1, n + 1, dtype=jnp.float32) / n)


def dilated_attention(q, k, v, slopes):
    f32 = jnp.float32
    B, H, S, Dh = q.shape
    scale = Dh ** -0.5
    q = q.astype(f32)
    kp = jnp.pad(k.astype(f32), ((0, 0), (0, 0), (DIL_MAX_WIN, 0), (0, 0)))
    vp = jnp.pad(v.astype(f32), ((0, 0), (0, 0), (DIL_MAX_WIN, 0), (0, 0)))
    nb = S // Q_BLOCK
    q_blk = q.reshape(B, H, nb, Q_BLOCK, Dh).transpose(2, 0, 1, 3, 4)
    slope = slopes[None, :, None, None]

    def block(args):
        qb, t0 = args
        tq = t0 + jnp.arange(Q_BLOCK)
        outs, lses = [], []
        for win, dil in DIL_PATTERNS:
            n_keys = win // dil + 1
            j = jnp.arange(n_keys)
            pos = tq[:, None] - j[None, :] * dil
            idx = (pos + DIL_MAX_WIN).reshape(-1)
            kg = jnp.take(kp, idx, axis=2).reshape(B, H, Q_BLOCK, n_keys, Dh)
            vg = jnp.take(vp, idx, axis=2).reshape(B, H, Q_BLOCK, n_keys, Dh)
            s = jnp.einsum('bhqd,bhqkd->bhqk', qb, kg) * scale - slope * (j * dil).astype(f32)
            s = jnp.where((pos >= 0)[None, None], s, -jnp.inf)
            m = jnp.max(s, axis=-1, keepdims=True)
            p = jnp.exp(s - m)
            den = jnp.sum(p, axis=-1, keepdims=True)
            outs.append(jnp.einsum('bhqk,bhqkd->bhqd', p, vg) / den)
            lses.append(m + jnp.log(den))
        w = jax.nn.softmax(jnp.stack(lses, 0), axis=0)
        return jnp.sum(w * jnp.stack(outs, 0), axis=0)

    out = lax.map(block, (q_blk, jnp.arange(nb) * Q_BLOCK))
    return out.transpose(1, 2, 0, 3, 4).reshape(B, H, S, Dh)


def compress_blocks(kv, pos_emb, w1, w2):
    B, S, Dh = kv.shape
    n_cmp = (S - NSA_CMP_LEN) // NSA_CMP_STRIDE + 1
    idx = jnp.arange(n_cmp)[:, None] * NSA_CMP_STRIDE + jnp.arange(NSA_CMP_LEN)[None, :]
    blocks = kv[:, idx] + pos_emb
    h = jax.nn.gelu(blocks.reshape(B, n_cmp, NSA_CMP_LEN * Dh) @ w1)
    return h @ w2


def nsa_attention(q, k_cmp, v_cmp, k_slc, v_slc, k_win, v_win, gate_logits,
                  qk_g, cmp_pos, cmp_w1, cmp_w2, slopes):
    f32 = jnp.float32
    B, S, _ = q.shape
    scale = HEAD_DIM ** -0.5
    slope = slopes[None, :, None, None]
    qh = rms_norm(to_heads(q, NSA_HEADS), qk_g[0]).astype(f32)
    kc = rms_norm(compress_blocks(k_cmp, cmp_pos[0], cmp_w1[0], cmp_w2[0]), qk_g[1]).astype(f32)
    vc = compress_blocks(v_cmp, cmp_pos[1], cmp_w1[1], cmp_w2[1]).astype(f32)
    ks = rms_norm(k_slc, qk_g[2]).astype(f32)
    vs = v_slc.astype(f32)
    kw = rms_norm(k_win, qk_g[3]).astype(f32)
    vw = v_win.astype(f32)
    t = jnp.arange(S)

    n_cmp = kc.shape[1]
    cmp_start = jnp.arange(n_cmp) * NSA_CMP_STRIDE
    dist_c = t[:, None] - (cmp_start + NSA_CMP_LEN - 1)[None, :]
    valid_c = dist_c >= 0
    s_c = jnp.einsum('bhtd,bcd->bhtc', qh, kc) * scale - slope * dist_c.astype(f32)
    p_c = jax.nn.softmax(jnp.where(valid_c, s_c, NEG_INF), axis=-1) * valid_c
    o_cmp = jnp.einsum('bhtc,bcd->bhtd', p_c, vc)

    n_sel_blocks = S // NSA_SEL_LEN
    n_top = min(NSA_TOPN, n_sel_blocks)
    sel_start = jnp.arange(n_sel_blocks) * NSA_SEL_LEN
    cover = ((cmp_start[:, None] < sel_start[None, :] + NSA_SEL_LEN)
             & (cmp_start[:, None] + NSA_CMP_LEN > sel_start[None, :])).astype(f32)
    imp = jnp.einsum('bhtc,cj->btj', p_c, cover)
    cur = t // NSA_SEL_LEN
    j = jnp.arange(n_sel_blocks)
    forced = (j[None, :] == 0) | (j[None, :] == cur[:, None]) | (j[None, :] == cur[:, None] - 1)
    imp = jnp.where(j[None, :] <= cur[:, None], imp + jnp.where(forced, FORCE_BONUS, 0.0), NEG_INF)
    _, sel_idx = lax.top_k(imp, n_top)

    nb = S // Q_BLOCK
    ks_blocks = ks.reshape(B, n_sel_blocks, NSA_SEL_LEN, HEAD_DIM)
    vs_blocks = vs.reshape(B, n_sel_blocks, NSA_SEL_LEN, HEAD_DIM)
    kw_pad = jnp.pad(kw, ((0, 0), (NSA_WIN, 0), (0, 0)))
    vw_pad = jnp.pad(vw, ((0, 0), (NSA_WIN, 0), (0, 0)))
    q_blk = qh.reshape(B, NSA_HEADS, nb, Q_BLOCK, HEAD_DIM).transpose(2, 0, 1, 3, 4)
    idx_blk = sel_idx.reshape(B, nb, Q_BLOCK, n_top).transpose(1, 0, 2, 3)
    offs = jnp.arange(NSA_SEL_LEN)
    gather = jax.vmap(lambda kb, i: kb[i])

    def block(args):
        qb, ib, t0 = args
        tq = t0 + jnp.arange(Q_BLOCK)
        kg = gather(ks_blocks, ib).reshape(B, Q_BLOCK, n_top * NSA_SEL_LEN, HEAD_DIM)
        vg = gather(vs_blocks, ib).reshape(B, Q_BLOCK, n_top * NSA_SEL_LEN, HEAD_DIM)
        pos = (ib[..., None] * NSA_SEL_LEN + offs).reshape(B, Q_BLOCK, n_top * NSA_SEL_LEN)
        dist = tq[None, :, None] - pos
        s = jnp.einsum('bhqd,bqkd->bhqk', qb, kg) * scale - slope * dist[:, None].astype(f32)
        p = jax.nn.softmax(jnp.where((dist >= 0)[:, None], s, NEG_INF), axis=-1)
        o_s = jnp.einsum('bhqk,bqkd->bhqd', p, vg)
        kwb = lax.dynamic_slice_in_dim(kw_pad, t0, Q_BLOCK + NSA_WIN, axis=1)
        vwb = lax.dynamic_slice_in_dim(vw_pad, t0, Q_BLOCK + NSA_WIN, axis=1)
        posw = t0 - NSA_WIN + jnp.arange(Q_BLOCK + NSA_WIN)
        dw = tq[:, None] - posw[None, :]
        mask_w = (dw >= 0) & (dw < NSA_WIN) & (posw[None, :] >= 0)
        sw = jnp.einsum('bhqd,bkd->bhqk', qb, kwb) * scale - slope * dw.astype(f32)
        pw = jax.nn.softmax(jnp.where(mask_w, sw, NEG_INF), axis=-1)
        o_w = jnp.einsum('bhqk,bkd->bhqd', pw, vwb)
        return o_s, o_w

    o_s, o_w = lax.map(block, (q_blk, idx_blk, jnp.arange(nb) * Q_BLOCK))
    o_s = o_s.transpose(1, 2, 0, 3, 4).reshape(B, NSA_HEADS, S, HEAD_DIM)
    o_w = o_w.transpose(1, 2, 0, 3, 4).reshape(B, NSA_HEADS, S, HEAD_DIM)
    g = jax.nn.sigmoid(gate_logits.astype(f32)).reshape(B, S, NSA_HEADS, NSA_BRANCHES).transpose(0, 2, 1, 3)
    o = g[..., 0:1] * o_cmp + g[..., 1:2] * o_s + g[..., 2:3] * o_w
    return from_heads(o)


def gla_attention(q, k, v, a_lr, r, wa2, ba, norm_g):
    f32 = jnp.float32
    B, S, _ = q.shape
    H, dk, dv, C = GLA_HEADS, GLA_DK, GLA_DV, GLA_CHUNK
    n = S // C

    def chunks(t, d):
        return t.astype(f32).reshape(B, n, C, H, d).transpose(0, 3, 1, 2, 4)

    log_a = jax.nn.log_sigmoid((a_lr @ wa2 + ba).astype(f32)) / GLA_TAU
    qc = chunks(q, dk) * dk ** -0.5
    kc = chunks(k, dk)
    vc = chunks(v, dv)
    b = jnp.cumsum(chunks(log_a, dk), axis=3)
    causal = jnp.tril(jnp.ones((C, C), dtype=bool))
    decay = jnp.exp(jnp.where(causal[:, :, None], b[..., :, None, :] - b[..., None, :, :], -jnp.inf))
    attn = jnp.einsum('bhnik,bhnjk,bhnijk->bhnij', qc, kc, decay)
    o_intra = jnp.einsum('bhnij,bhnjv->bhniv', attn, vc)
    b_last = b[..., -1, :]
    kv_chunk = jnp.einsum('bhnjk,bhnjv->bhnkv', kc * jnp.exp(b_last[..., None, :] - b), vc)

    def step(state, inp):
        dl, kvn = inp
        return dl[..., None] * state + kvn, state

    _, prev = lax.scan(step, jnp.zeros((B, H, dk, dv), f32),
                       (jnp.moveaxis(jnp.exp(b_last), 2, 0), jnp.moveaxis(kv_chunk, 2, 0)))
    prev = jnp.moveaxis(prev, 0, 2)
    o_inter = jnp.einsum('bhnik,bhnkv->bhniv', qc * jnp.exp(b), prev)
    o = (o_intra + o_inter).transpose(0, 2, 3, 1, 4).reshape(B, S, H, dv)
    o = rms_norm(o, norm_g)
    return o.reshape(B, S, H * dv) * jax.nn.silu(r.astype(f32))


def _complex_linear_combine(e1, e2):
    a1r, a1i, x1r, x1i = e1
    a2r, a2i, x2r, x2i = e2
    return (a2r * a1r - a2i * a1i, a2r * a1i + a2i * a1r,
            a2r * x1r - a2i * x1i + x2r, a2r * x1i + a2i * x1r + x2i)


def s5_layer(u, a_re, a_im, b_re, b_im, c_re, c_im, d, log_dt, glu_w, glu_b):
    f32 = jnp.float32
    B, S, W = u.shape
    uf = u.astype(f32)
    a_re = a_re.astype(f32)
    a_im = a_im.astype(f32)
    dt = jnp.exp(log_dt.astype(f32))[:, None]
    mag = jnp.exp(dt * a_re)
    abar_re = mag * jnp.cos(dt * a_im)
    abar_im = mag * jnp.sin(dt * a_im)
    den = a_re * a_re + a_im * a_im
    f_re = ((abar_re - 1.0) * a_re + abar_im * a_im) / den
    f_im = (abar_im * a_re - (abar_re - 1.0) * a_im) / den
    b_re = b_re.astype(f32)
    b_im = b_im.astype(f32)
    bb_re = f_re[..., None] * b_re - f_im[..., None] * b_im
    bb_im = f_re[..., None] * b_im + f_im[..., None] * b_re
    ug = uf.reshape(B, S, S5_GROUPS, S5_GROUP)
    bu_re = jnp.einsum('bsgc,gnc->bsgn', ug, bb_re)
    bu_im = jnp.einsum('bsgc,gnc->bsgn', ug, bb_im)
    a_re_t = jnp.broadcast_to(abar_re, bu_re.shape)
    a_im_t = jnp.broadcast_to(abar_im, bu_re.shape)
    _, _, x_re, x_im = lax.associative_scan(_complex_linear_combine, (a_re_t, a_im_t, bu_re, bu_im), axis=1)
    y = (jnp.einsum('bsgn,gcn->bsgc', x_re, c_re.astype(f32))
         - jnp.einsum('bsgn,gcn->bsgc', x_im, c_im.astype(f32)))
    y = y.reshape(B, S, W) + d.astype(f32) * uf
    hg = jax.nn.gelu(y)
    return hg * jax.nn.sigmoid(hg @ glu_w.astype(f32) + glu_b.astype(f32))


def swiglu(x, wg, wu, wd):
    return (jax.nn.silu(x @ wg) * (x @ wu)) @ wd


def moe_swiglu(h, router_w, router_b, wg, wu, wd):
    f32 = jnp.float32
    B, S, D = h.shape
    xt = h.reshape(B * S, D)
    logits = (xt @ router_w).astype(f32) + router_b.astype(f32)
    top_v, top_i = lax.top_k(logits, TOP_K)
    top_w = jax.nn.softmax(top_v, axis=-1)
    combine = jnp.einsum('tk,tke->te', top_w, jax.nn.one_hot(top_i, N_EXPERTS, dtype=f32))
    out = jnp.zeros((B * S, D), f32)
    for e in range(N_EXPERTS):
        out = out + combine[:, e:e + 1] * swiglu(xt, wg[e], wu[e], wd[e]).astype(f32)
    return out.reshape(B, S, D).astype(h.dtype)


def setup_inputs(seed: int = 0) -> dict:
    key = jax.random.key(seed)
    ks = iter(jax.random.split(key, 64))
    f32 = jnp.float32
    L = DEPTH
    nd = (DEPTH + 1) // 2
    nm = DEPTH // 2
    G, N, C = S5_GROUPS, S5_STATE, S5_GROUP

    def nrm(shape, scale):
        return jax.random.normal(next(ks), shape, f32) * scale

    def gain(shape):
        return 1.0 + nrm(shape, 0.02)

    return {
        "x": nrm((BATCH, SEQ, D_MODEL), 1.0),
        "norm1_g": gain((L, D_MODEL)),
        "w_in": nrm((L, D_MODEL, PROJ_WIDTH), D_MODEL ** -0.5),
        "dil_qk_g": gain((L, 2, HEAD_DIM)),
        "nsa_qk_g": gain((L, 4, HEAD_DIM)),
        "nsa_cmp_pos": nrm((L, 2, NSA_CMP_LEN, HEAD_DIM), 0.02),
        "nsa_cmp_w1": nrm((L, 2, NSA_CMP_LEN * HEAD_DIM, NSA_CMP_HIDDEN), (NSA_CMP_LEN * HEAD_DIM) ** -0.5),
        "nsa_cmp_w2": nrm((L, 2, NSA_CMP_HIDDEN, HEAD_DIM), NSA_CMP_HIDDEN ** -0.5),
        "gla_wa2": nrm((L, GLA_RANK, GLA_HEADS * GLA_DK), GLA_RANK ** -0.5),
        "gla_ba": nrm((L, GLA_HEADS * GLA_DK), 0.1),
        "gla_norm_g": gain((L, GLA_DV)),
        "s5_a_re": -0.5 + nrm((L, G, N), 0.01),
        "s5_a_im": math.pi * jnp.arange(N, dtype=f32) + nrm((L, G, N), 0.01),
        "s5_b_re": nrm((L, G, N, C), (2 * C) ** -0.5),
        "s5_b_im": nrm((L, G, N, C), (2 * C) ** -0.5),
        "s5_c_re": nrm((L, G, C, N), (2 * N) ** -0.5),
        "s5_c_im": nrm((L, G, C, N), (2 * N) ** -0.5),
        "s5_d": nrm((L, GROUP_WIDTH), 1.0),
        "s5_log_dt": jax.random.uniform(next(ks), (L, G), f32, math.log(1e-3), math.log(1e-1)),
        "s5_glu_w": nrm((L, GROUP_WIDTH, GROUP_WIDTH), GROUP_WIDTH ** -0.5),
        "s5_glu_b": nrm((L, GROUP_WIDTH), 0.02),
        "out_norm_g": gain((L, MIX_WIDTH)),
        "w_out": nrm((L, MIX_WIDTH, D_MODEL), MIX_WIDTH ** -0.5),
        "norm2_g": gain((L, D_MODEL)),
        "ffn_w_gate": nrm((nd, D_MODEL, D_FF), D_MODEL ** -0.5),
        "ffn_w_up": nrm((nd, D_MODEL, D_FF), D_MODEL ** -0.5),
        "ffn_w_down": nrm((nd, D_FF, D_MODEL), D_FF ** -0.5),
        "moe_router_w": nrm((nm, D_MODEL, N_EXPERTS), D_MODEL ** -0.5),
        "moe_router_b": nrm((nm, N_EXPERTS), 0.01),
        "moe_w_gate": nrm((nm, N_EXPERTS, D_MODEL, D_FF_EXPERT), D_MODEL ** -0.5),
        "moe_w_up": nrm((nm, N_EXPERTS, D_MODEL, D_FF_EXPERT), D_MODEL ** -0.5),
        "moe_w_down": nrm((nm, N_EXPERTS, D_FF_EXPERT, D_MODEL), D_FF_EXPERT ** -0.5),
    }


def reference(x, norm1_g, w_in, dil_qk_g, nsa_qk_g, nsa_cmp_pos, nsa_cmp_w1, nsa_cmp_w2,
              gla_wa2, gla_ba, gla_norm_g, s5_a_re, s5_a_im, s5_b_re, s5_b_im, s5_c_re, s5_c_im,
              s5_d, s5_log_dt, s5_glu_w, s5_glu_b, out_norm_g, w_out, norm2_g,
              ffn_w_gate, ffn_w_up, ffn_w_down, moe_router_w, moe_router_b,
              moe_w_gate, moe_w_up, moe_w_down):
    B, S, _ = x.shape
    slopes = alibi_slopes(N_SOFTMAX_HEADS)
    dil_slopes = slopes[1::2]
    nsa_slopes = slopes[0::2]
    for l in range(DEPTH):
        h = rms_norm(x, norm1_g[l])
        (dq, dk, dv, nq, nkc, nvc, nks, nvs, nkw, nvw, ng,
         gq, gk, gv, ga, gr, su) = split_proj(h @ w_in[l])
        y_dil = from_heads(dilated_attention(
            rms_norm(to_heads(dq, DIL_HEADS), dil_qk_g[l, 0]),
            rms_norm(to_heads(dk, DIL_HEADS), dil_qk_g[l, 1]),
            to_heads(dv, DIL_HEADS), dil_slopes))
        y_nsa = nsa_attention(nq, nkc, nvc, nks, nvs, nkw, nvw, ng, nsa_qk_g[l],
                              nsa_cmp_pos[l], nsa_cmp_w1[l], nsa_cmp_w2[l], nsa_slopes)
        y_gla = gla_attention(gq, gk, gv, ga, gr, gla_wa2[l], gla_ba[l], gla_norm_g[l])
        y_s5 = s5_layer(su, s5_a_re[l], s5_a_im[l], s5_b_re[l], s5_b_im[l], s5_c_re[l], s5_c_im[l],
                        s5_d[l], s5_log_dt[l], s5_glu_w[l], s5_glu_b[l])
        mix = jnp.stack([y_dil.astype(x.dtype), y_nsa.astype(x.dtype),
                         y_gla.astype(x.dtype), y_s5.astype(x.dtype)], axis=-2)
        mix = rms_norm(mix, out_norm_g[l].reshape(N_MIXERS, GROUP_WIDTH)).reshape(B, S, MIX_WIDTH)
        x = x + mix @ w_out[l]
        h = rms_norm(x, norm2_g[l])
        if l % 2 == 0:
            i = l // 2
            x = x + swiglu(h, ffn_w_gate[i], ffn_w_up[i], ffn_w_down[i])
        else:
            i = l // 2
            x = x + moe_swiglu(h, moe_router_w[i], moe_router_b[i], moe_w_gate[i], moe_w_up[i], moe_w_down[i])
    return x
```

```python
import functools
import math

import jax
import jax.numpy as jnp
from jax import lax
from jax.experimental import pallas as pl
from jax.experimental.pallas import tpu as pltpu

F32 = jnp.float32
BF16 = jnp.bfloat16

D_MODEL = 1024
HEAD_DIM = 64
GROUP_WIDTH = 256
N_HEADS = 4
DIL_PATTERNS = ((128, 1), (512, 4), (2048, 16))
Q_BLOCK = 128
NSA_CMP_LEN = 32
NSA_CMP_STRIDE = 16
NSA_CMP_HIDDEN = 256
NSA_SEL_LEN = 64
NSA_TOPN = 16
NSA_WIN = 512
FORCE_BONUS = 1.0e4
GLA_DK = 32
GLA_DV = 64
GLA_RANK = 16
GLA_TAU = 16.0
GLA_CHUNK = 16
S5_GROUP = 16
S5_GROUPS = 16
S5_STATE = 64
S5_CHUNK = 16
D_FF = 2752
N_EXPERTS = 8
D_FF_EXPERT = 3584
EPS = 1e-6
NEG_INF = -1e30

DIL_SLOPES = tuple(2.0 ** (-float(i)) for i in (2, 4, 6, 8))
NSA_SLOPES = tuple(2.0 ** (-float(i)) for i in (1, 3, 5, 7))

COL_DQ, COL_DK, COL_DV, COL_NQ = 0, 256, 512, 768
COL_NCMP, COL_NK, COL_NV, COL_NG = 1024, 1152, 1280, 1408
COL_GQ, COL_GK, COL_GV, COL_GA, COL_GR, COL_SU = 1536, 1664, 1792, 2048, 2304, 2560
PROJ_PAD = 2816
LANE = 128
D_FF_PAD = 2816

VMEM_LIMIT = 56 * 1024 * 1024


def _cparams(sem):
    return pltpu.CompilerParams(dimension_semantics=sem, vmem_limit_bytes=VMEM_LIMIT)


def _rms(x):
    return x * lax.rsqrt(jnp.mean(x * x, axis=-1, keepdims=True) + EPS)


def _dot(a, b):
    return jnp.dot(a, b, preferred_element_type=F32)


def _dot_nt(a, b):
    return lax.dot_general(a, b, (((1,), (1,)), ((), ())), preferred_element_type=F32)


def _split(x):
    hi = x.astype(BF16)
    lo = (x - hi.astype(F32)).astype(BF16)
    return hi, lo


def _dot_lsplit(x, m):
    hi, lo = _split(x)
    return _dot(hi, m) + _dot(lo, m)


def _dot_rsplit(m, x):
    hi, lo = _split(x)
    return _dot(m, hi) + _dot(m, lo)


def _dot3(a, b):
    ah, al = _split(a)
    bh, bl = _split(b)
    return _dot(ah, bh) + _dot(ah, bl) + _dot(al, bh)


def _gelu(x):
    return 0.5 * x * (1.0 + jnp.tanh(math.sqrt(2.0 / math.pi) * (x + 0.044715 * (x * x * x))))


def _sigmoid(x):
    return 1.0 / (1.0 + jnp.exp(-x))


def _iota(shape, dim):
    return lax.broadcasted_iota(jnp.int32, shape, dim)


def _inproj_kernel(x_ref, g_ref, w_ref, o_ref):
    h = _rms(x_ref[...]) * g_ref[...]
    o_ref[...] = _dot(h.astype(BF16), w_ref[...])


def _inproj(x2, g, w_pad):
    T = x2.shape[0]
    tm = 512
    return pl.pallas_call(
        _inproj_kernel,
        grid=(T // tm,),
        in_specs=[
            pl.BlockSpec((tm, D_MODEL), lambda i: (i, 0)),
            pl.BlockSpec((1, D_MODEL), lambda i: (0, 0)),
            pl.BlockSpec((D_MODEL, PROJ_PAD), lambda i: (0, 0)),
        ],
        out_specs=pl.BlockSpec((tm, PROJ_PAD), lambda i: (i, 0)),
        out_shape=jax.ShapeDtypeStruct((T, PROJ_PAD), F32),
        compiler_params=_cparams(("parallel",)),
    )(x2, g.reshape(1, D_MODEL), w_pad)


def _pad_w_in(w):
    def seg(lo, width, pad_to):
        s = w[:, lo:lo + width]
        if pad_to > width:
            s = jnp.pad(s, ((0, 0), (0, pad_to - width)))
        return s
    k_cmp, v_cmp = seg(1024, 64, 64), seg(1088, 64, 64)
    k_slc, v_slc = seg(1152, 64, 64), seg(1216, 64, 64)
    k_win, v_win = seg(1280, 64, 64), seg(1344, 64, 64)
    pieces = [
        seg(0, 256, 256), seg(256, 256, 256), seg(512, 256, 256), seg(768, 256, 256),
        k_cmp, v_cmp, k_slc, k_win, v_slc, v_win,
        seg(1408, 12, 128),
        seg(1420, 128, 128), seg(1548, 128, 128), seg(1676, 256, 256),
        seg(1932, 16, 256),
        seg(1948, 256, 256), seg(2204, 256, 256),
    ]
    out = jnp.concatenate(pieces, axis=1)
    assert out.shape[1] == PROJ_PAD
    return out.astype(BF16)


def _dil_kernel(q_ref, kp_ref, kc_ref, vp_ref, vc_ref, g_ref, o_ref, *, dil):
    u = pl.program_id(2)
    qb = q_ref[0]
    gq = g_ref[0:1, :]
    gk = g_ref[1:2, :]
    ri = _iota((Q_BLOCK, 2 * Q_BLOCK), 0)
    ci = _iota((Q_BLOCK, 2 * Q_BLOCK), 1)
    delta = Q_BLOCK + ri - ci
    valid = (delta >= 0) & (delta <= Q_BLOCK) & ((u > 0) | (ci >= Q_BLOCK))
    delta_f = delta.astype(F32)
    lane = _iota((Q_BLOCK, LANE), 1)
    outs = []
    stats = jnp.zeros((Q_BLOCK, LANE), F32)
    for h in range(N_HEADS):
        sl = slice(h * HEAD_DIM, (h + 1) * HEAD_DIM)
        qn = (_rms(qb[:, sl]) * gq * (HEAD_DIM ** -0.5)).astype(BF16)
        kh = jnp.concatenate([kp_ref[0][:, sl], kc_ref[0][:, sl]], axis=0)
        kn = (_rms(kh) * gk).astype(BF16)
        vh = jnp.concatenate([vp_ref[0][:, sl], vc_ref[0][:, sl]], axis=0).astype(BF16)
        s = _dot_nt(qn, kn) - (DIL_SLOPES[h] * dil) * delta_f
        s = jnp.where(valid, s, NEG_INF)
        m = jnp.max(s, axis=-1, keepdims=True)
        p = jnp.exp(s - m)
        l = jnp.sum(p, axis=-1, keepdims=True)
        outs.append(_dot(p.astype(BF16), vh) / l)
        stats = jnp.where(lane == h, m + jnp.log(l), stats)
    o_ref[0] = jnp.concatenate(outs + [stats], axis=1)


DIL_OUT = GROUP_WIDTH + LANE


def _dil_pattern(proj3, g, dil):
    B, S, _ = proj3.shape
    su = S // dil
    nu = su // Q_BLOCK
    pv = proj3.reshape(B, su, dil * PROJ_PAD)
    ncol = PROJ_PAD // GROUP_WIDTH

    def spec(col, prev):
        if prev:
            return pl.BlockSpec((1, Q_BLOCK, GROUP_WIDTH),
                                lambda b, r, u: (b, jnp.maximum(u - 1, 0), r * ncol + col))
        return pl.BlockSpec((1, Q_BLOCK, GROUP_WIDTH), lambda b, r, u: (b, u, r * ncol + col))

    out = pl.pallas_call(
        functools.partial(_dil_kernel, dil=dil),
        grid=(B, dil, nu),
        in_specs=[spec(0, False), spec(1, True), spec(1, False), spec(2, True), spec(2, False),
                  pl.BlockSpec((2, HEAD_DIM), lambda b, r, u: (0, 0))],
        out_specs=pl.BlockSpec((1, Q_BLOCK, DIL_OUT), lambda b, r, u: (b, u, r)),
        out_shape=jax.ShapeDtypeStruct((B, su, dil * DIL_OUT), F32),
        compiler_params=_cparams(("parallel", "parallel", "parallel")),
    )(pv, pv, pv, pv, pv, g)
    return out.reshape(B, S, DIL_OUT)


def _nsa_prep_kernel(xk_ref, xv_ref, kraw_ref, vraw_ref, pos_ref, w1_ref, w2_ref, g_ref,
                     kc_ref, vc_ref, ks_ref, vs_ref, kw_ref, vw_ref):
    half = NSA_CMP_STRIDE * HEAD_DIM

    def compress(x, idx):
        w1 = w1_ref[idx]
        xb = x.astype(BF16)
        a = _dot(xb, w1[:half, :])
        b = _dot(xb, w1[half:, :])
        bias = _dot(pos_ref[idx].astype(BF16), w1)[0:1, :]
        pre = a + pltpu.roll(b, b.shape[0] - 1, 0) + bias
        return _dot(_gelu(pre).astype(BF16), w2_ref[idx])

    kc_ref[0] = (_rms(compress(xk_ref[0], 0)) * g_ref[1:2, :]).astype(BF16)
    vc_ref[0] = compress(xv_ref[0], 1).astype(BF16)
    kraw = kraw_ref[0]
    vraw = vraw_ref[0]
    ks_ref[0] = (_rms(kraw[:, :HEAD_DIM]) * g_ref[2:3, :]).astype(BF16)
    kw_ref[0] = (_rms(kraw[:, HEAD_DIM:]) * g_ref[3:4, :]).astype(BF16)
    vs_ref[0] = vraw[:, :HEAD_DIM].astype(BF16)
    vw_ref[0] = vraw[:, HEAD_DIM:].astype(BF16)


def _nsa_prep(proj3, pos, w1, w2, g):
    B, S, _ = proj3.shape
    ng = S // NSA_CMP_STRIDE
    gw = NSA_CMP_STRIDE * HEAD_DIM
    xk = proj3[:, :, COL_NCMP:COL_NCMP + HEAD_DIM].reshape(B, ng, gw)
    xv = proj3[:, :, COL_NCMP + HEAD_DIM:COL_NCMP + 2 * HEAD_DIM].reshape(B, ng, gw)
    pos8 = jnp.broadcast_to(pos.reshape(2, 1, NSA_CMP_LEN * HEAD_DIM), (2, 8, NSA_CMP_LEN * HEAD_DIM))
    small = jax.ShapeDtypeStruct((B, ng, HEAD_DIM), BF16)
    big = jax.ShapeDtypeStruct((B, S, HEAD_DIM), BF16)
    small_spec = pl.BlockSpec((1, ng, HEAD_DIM), lambda b: (b, 0, 0))
    big_spec = pl.BlockSpec((1, S, HEAD_DIM), lambda b: (b, 0, 0))
    return pl.pallas_call(
        _nsa_prep_kernel,
        grid=(B,),
        in_specs=[
            pl.BlockSpec((1, ng, gw), lambda b: (b, 0, 0)),
            pl.BlockSpec((1, ng, gw), lambda b: (b, 0, 0)),
            pl.BlockSpec((1, S, LANE), lambda b: (b, 0, COL_NK // LANE)),
            pl.BlockSpec((1, S, LANE), lambda b: (b, 0, COL_NV // LANE)),
            pl.BlockSpec((2, 8, NSA_CMP_LEN * HEAD_DIM), lambda b: (0, 0, 0)),
            pl.BlockSpec((2, NSA_CMP_LEN * HEAD_DIM, NSA_CMP_HIDDEN), lambda b: (0, 0, 0)),
            pl.BlockSpec((2, NSA_CMP_HIDDEN, HEAD_DIM), lambda b: (0, 0, 0)),
            pl.BlockSpec((4, HEAD_DIM), lambda b: (0, 0)),
        ],
        out_specs=[small_spec, small_spec, big_spec, big_spec, big_spec, big_spec],
        out_shape=[small, small, big, big, big, big],
        compiler_params=_cparams(("parallel",)),
    )(xk, xv, proj3, proj3, pos8, w1.astype(BF16), w2.astype(BF16), g)


def _nsa_kernel(q_ref, gate_ref, kc_ref, vc_ref, ks_ref, vs_ref, kw_ref, vw_ref, g_ref, o_ref, *, seq):
    i = pl.program_id(1)
    t0 = i * Q_BLOCK
    rows = N_HEADS * Q_BLOCK
    qb = q_ref[0]
    gq = g_ref[0:1, :]
    q = jnp.concatenate(
        [(_rms(qb[:, h * HEAD_DIM:(h + 1) * HEAD_DIM]) * gq * (HEAD_DIM ** -0.5)).astype(BF16)
         for h in range(N_HEADS)], axis=0)
    rid = _iota((rows, 1), 0)
    head = rid // Q_BLOCK
    slope = jnp.where(head == 0, NSA_SLOPES[0],
                      jnp.where(head == 1, NSA_SLOPES[1],
                                jnp.where(head == 2, NSA_SLOPES[2], NSA_SLOPES[3]))).astype(F32)
    trow = t0 + rid % Q_BLOCK

    ncmp = kc_ref.shape[1]
    cend = _iota((1, ncmp), 1) * NSA_CMP_STRIDE + (NSA_CMP_LEN - 1)
    dist = trow - cend
    valid = dist >= 0
    s = _dot_nt(q, kc_ref[0]) - slope * dist.astype(F32)
    s = jnp.where(valid, s, NEG_INF)
    m = jnp.max(s, axis=-1, keepdims=True)
    p = jnp.exp(s - m)
    p = jnp.where(valid, p / jnp.sum(p, axis=-1, keepdims=True), 0.0)
    o_cmp = _dot(p.astype(BF16), vc_ref[0])

    nsel = seq // NSA_SEL_LEN
    psum = p[0:Q_BLOCK] + p[Q_BLOCK:2 * Q_BLOCK] + p[2 * Q_BLOCK:3 * Q_BLOCK] + p[3 * Q_BLOCK:]
    cc = _iota((ncmp, nsel), 0)
    jj = _iota((ncmp, nsel), 1)
    ratio = NSA_SEL_LEN // NSA_CMP_STRIDE
    cover = ((cc < ratio * jj + ratio) & (cc * NSA_CMP_STRIDE + NSA_CMP_LEN > jj * NSA_SEL_LEN))
    imp = _dot_lsplit(psum, cover.astype(BF16))
    jcol = _iota((Q_BLOCK, nsel), 1)
    cur = (t0 + _iota((Q_BLOCK, nsel), 0)) // NSA_SEL_LEN
    forced = (jcol == 0) | (jcol == cur) | (jcol == cur - 1)
    impv = jnp.where(jcol <= cur, imp + jnp.where(forced, FORCE_BONUS, 0.0), NEG_INF)
    rank = jnp.zeros((Q_BLOCK, nsel), F32)
    for ii in range(nsel):
        col = impv[:, ii:ii + 1]
        beats = (col > impv) | ((col == impv) & (jcol > ii))
        rank = rank + jnp.where(beats, 1.0, 0.0)
    sel = jnp.where((rank < min(NSA_TOPN, nsel)) & (jcol <= cur), 1.0, 0.0).astype(BF16)

    kchunk = 2 * Q_BLOCK
    jrow = _iota((nsel, kchunk), 0)
    kcol = _iota((nsel, kchunk), 1)
    kpos_row = _iota((1, kchunk), 1)

    def sel_body(c, carry):
        m_i, l_i, acc = carry
        k0 = pl.multiple_of(c * kchunk, kchunk)
        kk = ks_ref[0, pl.ds(k0, kchunk), :]
        vv = vs_ref[0, pl.ds(k0, kchunk), :]
        expand = jnp.where((k0 + kcol) // NSA_SEL_LEN == jrow, 1.0, 0.0).astype(BF16)
        selx = _dot(sel, expand)
        selx = jnp.concatenate([selx] * N_HEADS, axis=0)
        dk = trow - (k0 + kpos_row)
        ok = (selx > 0.5) & (dk >= 0)
        sc = jnp.where(ok, _dot_nt(q, kk) - slope * dk.astype(F32), NEG_INF)
        m_new = jnp.maximum(m_i, jnp.max(sc, axis=-1, keepdims=True))
        alpha = jnp.exp(m_i - m_new)
        pc = jnp.exp(sc - m_new)
        l_new = alpha * l_i + jnp.sum(pc, axis=-1, keepdims=True)
        acc_new = alpha * acc + _dot(pc.astype(BF16), vv)
        return m_new, l_new, acc_new

    init = (jnp.full((rows, 1), NEG_INF, F32), jnp.zeros((rows, 1), F32), jnp.zeros((rows, HEAD_DIM), F32))
    _, l_s, acc_s = lax.fori_loop(0, (t0 + Q_BLOCK - 1) // kchunk + 1, sel_body, init)
    o_sel = acc_s / l_s

    span = NSA_WIN + Q_BLOCK
    start = pl.multiple_of(jnp.maximum(t0 - NSA_WIN, 0), Q_BLOCK)
    kw = kw_ref[0, pl.ds(start, span), :]
    vw = vw_ref[0, pl.ds(start, span), :]
    dw = trow - (start + _iota((1, span), 1))
    okw = (dw >= 0) & (dw < NSA_WIN)
    sw = jnp.where(okw, _dot_nt(q, kw) - slope * dw.astype(F32), NEG_INF)
    mw = jnp.max(sw, axis=-1, keepdims=True)
    pw = jnp.exp(sw - mw)
    o_win = _dot(pw.astype(BF16), vw) / jnp.sum(pw, axis=-1, keepdims=True)

    gate = _sigmoid(gate_ref[0])
    outs = []
    for h in range(N_HEADS):
        r = slice(h * Q_BLOCK, (h + 1) * Q_BLOCK)
        outs.append(gate[:, 3 * h:3 * h + 1] * o_cmp[r] + gate[:, 3 * h + 1:3 * h + 2] * o_sel[r]
                    + gate[:, 3 * h + 2:3 * h + 3] * o_win[r])
    o_ref[0] = jnp.concatenate(outs, axis=1)


def _nsa_attention(proj3, kc, vc, ks, vs, kw, vw, g):
    B, S, _ = proj3.shape
    nb = S // Q_BLOCK
    ng = kc.shape[1]
    small_spec = pl.BlockSpec((1, ng, HEAD_DIM), lambda b, i: (b, 0, 0))
    big_spec = pl.BlockSpec((1, S, HEAD_DIM), lambda b, i: (b, 0, 0))
    return pl.pallas_call(
        functools.partial(_nsa_kernel, seq=S),
        grid=(B, nb),
        in_specs=[
            pl.BlockSpec((1, Q_BLOCK, GROUP_WIDTH), lambda b, i: (b, i, COL_NQ // GROUP_WIDTH)),
            pl.BlockSpec((1, Q_BLOCK, LANE), lambda b, i: (b, i, COL_NG // LANE)),
            small_spec, small_spec, big_spec, big_spec, big_spec, big_spec,
            pl.BlockSpec((4, HEAD_DIM), lambda b, i: (0, 0)),
        ],
        out_specs=pl.BlockSpec((1, Q_BLOCK, GROUP_WIDTH), lambda b, i: (b, i, 0)),
        out_shape=jax.ShapeDtypeStruct((B, S, GROUP_WIDTH), F32),
        compiler_params=_cparams(("parallel", "parallel")),
    )(proj3, proj3, kc, vc, ks, vs, kw, vw, g)


GLA_BLOCK = 128
GLA_W = N_HEADS * GLA_DK


def _gla_kernel(q_ref, k_ref, v_ref, a_ref, r_ref, wa_ref, ba_ref, g_ref, o_ref, st_ref):
    nsub = GLA_BLOCK // GLA_CHUNK

    @pl.when(pl.program_id(1) == 0)
    def _():
        st_ref[...] = jnp.zeros_like(st_ref)

    q = q_ref[0] * (GLA_DK ** -0.5)
    k = k_ref[0]
    v = v_ref[0]
    z = _dot3(a_ref[0], wa_ref[...]) + ba_ref[...]
    log_a = (jnp.minimum(z, 0.0) - jnp.log(1.0 + jnp.exp(-jnp.abs(z)))) * (1.0 / GLA_TAU)
    ri = _iota((GLA_BLOCK, GLA_BLOCK), 0)
    ci = _iota((GLA_BLOCK, GLA_BLOCK), 1)
    same = (ri // GLA_CHUNK) == (ci // GLA_CHUNK)
    b_loc = _dot_rsplit(jnp.where(same & (ci <= ri), 1.0, 0.0).astype(BF16), log_a)
    b_tot = _dot_rsplit(jnp.where(same, 1.0, 0.0).astype(BF16), log_a)
    qd = q * jnp.exp(b_loc)
    kd = k * jnp.exp(b_tot - b_loc)

    ind_kv = jnp.where(_iota((GLA_W, GROUP_WIDTH), 0) // GLA_DK == _iota((GLA_W, GROUP_WIDTH), 1) // GLA_DV,
                       1.0, 0.0).astype(BF16)
    rsub = _iota((GLA_BLOCK, 1), 0) % GLA_CHUNK
    o = jnp.zeros((GLA_BLOCK, GROUP_WIDTH), F32)
    for s in range(GLA_CHUNK):
        if s == 0:
            ks_, bs_, vs_ = k, b_loc, v
        else:
            ks_, bs_, vs_ = pltpu.roll(k, s, 0), pltpu.roll(b_loc, s, 0), pltpu.roll(v, s, 0)
        dec = jnp.where(rsub >= s, jnp.exp(jnp.minimum(b_loc - bs_, 0.0)), 0.0)
        w = _dot((q * ks_ * dec).astype(BF16), ind_kv)
        o = o + w * vs_

    v_t = v.T.astype(BF16)
    head_ok = _iota((GROUP_WIDTH, GLA_W), 0) // GLA_DV == _iota((GROUP_WIDTH, GLA_W), 1) // GLA_DK
    rblk = _iota((GLA_BLOCK, 1), 0) // GLA_CHUNK
    st = st_ref[...]
    for c in range(nsub):
        in_c = rblk == c
        o = o + _dot_nt(jnp.where(in_c, qd, 0.0).astype(BF16), st.astype(BF16))
        decay = jnp.exp(b_tot[c * GLA_CHUNK:c * GLA_CHUNK + 1, :])
        upd = _dot(v_t, jnp.where(in_c, kd, 0.0).astype(BF16))
        st = st * decay + jnp.where(head_ok, upd, 0.0)
    st_ref[...] = st

    ind_vv = jnp.where(_iota((GROUP_WIDTH, GROUP_WIDTH), 0) // GLA_DV == _iota((GROUP_WIDTH, GROUP_WIDTH), 1) // GLA_DV,
                       1.0, 0.0).astype(BF16)
    ms = _dot_lsplit(o * o, ind_vv) * (1.0 / GLA_DV)
    r = r_ref[0]
    o_ref[0] = o * lax.rsqrt(ms + EPS) * g_ref[...] * (r * _sigmoid(r))


def _gla(proj3, wa2, ba, norm_g):
    B, S, _ = proj3.shape
    nb = S // GLA_BLOCK
    wa_pad = jnp.zeros((LANE, GLA_W), F32).at[:GLA_RANK].set(wa2)
    g_t = jnp.tile(norm_g.reshape(1, GLA_DV), (1, N_HEADS))

    def col(c, w):
        return pl.BlockSpec((1, GLA_BLOCK, w), lambda b, j: (b, j, c // w))

    return pl.pallas_call(
        _gla_kernel,
        grid=(B, nb),
        in_specs=[col(COL_GQ, LANE), col(COL_GK, LANE), col(COL_GV, GROUP_WIDTH), col(COL_GA, LANE),
                  col(COL_GR, GROUP_WIDTH),
                  pl.BlockSpec((LANE, GLA_W), lambda b, j: (0, 0)),
                  pl.BlockSpec((1, GLA_W), lambda b, j: (0, 0)),
                  pl.BlockSpec((1, GROUP_WIDTH), lambda b, j: (0, 0))],
        out_specs=pl.BlockSpec((1, GLA_BLOCK, GROUP_WIDTH), lambda b, j: (b, j, 0)),
        out_shape=jax.ShapeDtypeStruct((B, S, GROUP_WIDTH), F32),
        scratch_shapes=[pltpu.VMEM((GROUP_WIDTH, GLA_W), F32)],
        compiler_params=_cparams(("parallel", "arbitrary")),
    )(proj3, proj3, proj3, proj3, proj3, wa_pad, ba.reshape(1, GLA_W), g_t)


S5_CW = S5_CHUNK * S5_GROUP
S5_SW = 2 * S5_STATE


def _s5_matrices(a_re, a_im, b_re, b_im, c_re, c_im, log_dt):
    hp = lax.Precision.HIGHEST
    L = S5_CHUNK
    dt = jnp.exp(log_dt)[:, None]
    lam_re, lam_im = dt * a_re, dt * a_im
    tau = jnp.arange(L + 1, dtype=F32)[:, None, None]
    mag = jnp.exp(tau * lam_re)
    p_re, p_im = mag * jnp.cos(tau * lam_im), mag * jnp.sin(tau * lam_im)
    den = a_re * a_re + a_im * a_im
    f_re = ((p_re[1] - 1.0) * a_re + p_im[1] * a_im) / den
    f_im = (p_im[1] * a_re - (p_re[1] - 1.0) * a_im) / den
    bb_re = f_re[..., None] * b_re - f_im[..., None] * b_im
    bb_im = f_re[..., None] * b_im + f_im[..., None] * b_re
    pb_re = p_re[..., None] * bb_re - p_im[..., None] * bb_im
    pb_im = p_re[..., None] * bb_im + p_im[..., None] * bb_re
    kern = (jnp.einsum('gon,tgnc->tgoc', c_re, pb_re, precision=hp)
            - jnp.einsum('gon,tgnc->tgoc', c_im, pb_im, precision=hp))
    s_idx = jnp.arange(L)[:, None]
    t_idx = jnp.arange(L)[None, :]
    lag = jnp.clip(t_idx - s_idx, 0, L)
    toep = jnp.where((t_idx >= s_idx)[..., None, None, None], kern[lag], 0.0)
    toep = toep.transpose(2, 0, 4, 1, 3).reshape(S5_GROUPS, S5_CW, S5_CW)
    rev = (L - 1 - jnp.arange(L))
    bin_re = pb_re[rev].transpose(1, 0, 3, 2).reshape(S5_GROUPS, S5_CW, S5_STATE)
    bin_im = pb_im[rev].transpose(1, 0, 3, 2).reshape(S5_GROUPS, S5_CW, S5_STATE)
    b_in = jnp.concatenate([bin_re, bin_im], axis=-1)
    pt_re, pt_im = p_re[1:], p_im[1:]
    co_re = c_re[None] * pt_re[:, :, None, :] - c_im[None] * pt_im[:, :, None, :]
    co_im = -c_re[None] * pt_im[:, :, None, :] - c_im[None] * pt_re[:, :, None, :]
    c_out = jnp.concatenate([co_re.transpose(1, 3, 0, 2), co_im.transpose(1, 3, 0, 2)], axis=1)
    c_out = c_out.reshape(S5_GROUPS, S5_SW, S5_CW)
    a_chunk = jnp.stack([p_re[L].reshape(1, -1), p_im[L].reshape(1, -1)], axis=0)
    return jnp.concatenate([toep, b_in], axis=-1).astype(BF16), c_out.astype(BF16), a_chunk


def _s5_in_kernel(u_ref, tb_ref, y_ref, v_ref):
    r = _dot(u_ref[0].astype(BF16), tb_ref[0])
    y_ref[0] = r[:, :S5_CW]
    v_ref[0] = r[:, S5_CW:]


def _s5_scan_kernel(v_ref, a_ref, x_ref):
    nb = v_ref.shape[2]
    a_r = jnp.broadcast_to(a_ref[0], (nb, a_ref.shape[2]))
    a_i = jnp.broadcast_to(a_ref[1], (nb, a_ref.shape[2]))

    def body(m, carry):
        x_r, x_i = carry
        x_ref[m, 0] = x_r
        x_ref[m, 1] = x_i
        return a_r * x_r - a_i * x_i + v_ref[m, 0], a_r * x_i + a_i * x_r + v_ref[m, 1]

    zero = jnp.zeros(a_r.shape, F32)
    lax.fori_loop(0, v_ref.shape[0], body, (zero, zero))


def _s5_out_kernel(y_ref, x_ref, c_ref, o_ref):
    o_ref[0] = y_ref[0] + _dot(x_ref[0].astype(BF16), c_ref[0])


def _s5_linear(u3, a_re, a_im, b_re, b_im, c_re, c_im, log_dt):
    B, S, _ = u3.shape
    M = S // S5_CHUNK
    G = S5_GROUPS
    tb, c_out, a_chunk = _s5_matrices(a_re, a_im, b_re, b_im, c_re, c_im, log_dt)
    ug = u3.reshape(B, M, S5_CHUNK, G, S5_GROUP).transpose(3, 0, 1, 2, 4).reshape(G, B * M, S5_CW)
    y_in, v = pl.pallas_call(
        _s5_in_kernel,
        grid=(G,),
        in_specs=[pl.BlockSpec((1, B * M, S5_CW), lambda g: (g, 0, 0)),
                  pl.BlockSpec((1, S5_CW, S5_CW + S5_SW), lambda g: (g, 0, 0))],
        out_specs=[pl.BlockSpec((1, B * M, S5_CW), lambda g: (g, 0, 0)),
                   pl.BlockSpec((1, B * M, S5_SW), lambda g: (g, 0, 0))],
        out_shape=[jax.ShapeDtypeStruct((G, B * M, S5_CW), F32),
                   jax.ShapeDtypeStruct((G, B * M, S5_SW), F32)],
        compiler_params=_cparams(("parallel",)),
    )(ug, tb)
    gn = G * S5_STATE
    v_t = v.reshape(G, B, M, 2, S5_STATE).transpose(2, 3, 1, 0, 4).reshape(M, 2, B, gn)
    lw = 256
    x_t = pl.pallas_call(
        _s5_scan_kernel,
        grid=(gn // lw,),
        in_specs=[pl.BlockSpec((M, 2, B, lw), lambda j: (0, 0, 0, j)),
                  pl.BlockSpec((2, 1, lw), lambda j: (0, 0, j))],
        out_specs=pl.BlockSpec((M, 2, B, lw), lambda j: (0, 0, 0, j)),
        out_shape=jax.ShapeDtypeStruct((M, 2, B, gn), F32),
        compiler_params=_cparams(("parallel",)),
    )(v_t, a_chunk)
    xg = x_t.reshape(M, 2, B, G, S5_STATE).transpose(3, 2, 0, 1, 4).reshape(G, B * M, S5_SW)
    y = pl.pallas_call(
        _s5_out_kernel,
        grid=(G,),
        in_specs=[pl.BlockSpec((1, B * M, S5_CW), lambda g: (g, 0, 0)),
                  pl.BlockSpec((1, B * M, S5_SW), lambda g: (g, 0, 0)),
                  pl.BlockSpec((1, S5_SW, S5_CW), lambda g: (g, 0, 0))],
        out_specs=pl.BlockSpec((1, B * M, S5_CW), lambda g: (g, 0, 0)),
        out_shape=jax.ShapeDtypeStruct((G, B * M, S5_CW), F32),
        compiler_params=_cparams(("parallel",)),
    )(y_in, xg, c_out)
    return y.reshape(G, B, M, S5_CHUNK, S5_GROUP).transpose(1, 2, 3, 0, 4).reshape(B, S, GROUP_WIDTH)


def _mixout_kernel(d1_ref, d2_ref, d3_ref, nsa_ref, gla_ref, s5y_ref, u_ref, x_ref,
                   s5d_ref, gw_ref, gb_ref, ng_ref, wo_ref, o_ref):
    d = [d1_ref[...], d2_ref[...], d3_ref[...]]
    lse = [t[:, GROUP_WIDTH:] for t in d]
    mx = jnp.maximum(jnp.maximum(lse[0], lse[1]), lse[2])
    e = [jnp.exp(t - mx) for t in lse]
    inv = 1.0 / (e[0] + e[1] + e[2])
    spread = jnp.where(_iota((LANE, GROUP_WIDTH), 0) == _iota((LANE, GROUP_WIDTH), 1) // HEAD_DIM,
                       1.0, 0.0).astype(BF16)
    y_dil = jnp.zeros((d[0].shape[0], GROUP_WIDTH), F32)
    for p in range(3):
        y_dil = y_dil + _dot_lsplit(e[p] * inv, spread) * d[p][:, :GROUP_WIDTH]
    u = u_ref[...]
    hg = _gelu(s5y_ref[...] + s5d_ref[...] * u)
    y_s5 = hg * _sigmoid(_dot(hg.astype(BF16), gw_ref[...]) + gb_ref[...])
    parts = [y_dil, nsa_ref[...], gla_ref[...], y_s5]
    mix = jnp.concatenate([(_rms(parts[n]) * ng_ref[n:n + 1, :]).astype(BF16) for n in range(4)], axis=1)
    o_ref[...] = x_ref[...] + _dot(mix, wo_ref[...])


def _mixout(d1, d2, d3, y_nsa, y_gla, y_s5, proj, x2, s5_d, glu_w, glu_b, out_g, w_out):
    T = x2.shape[0]
    tm = 256

    def row(w, c=0):
        return pl.BlockSpec((tm, w), lambda i: (i, c))

    def const(shape):
        return pl.BlockSpec(shape, lambda i: (0,) * len(shape))

    return pl.pallas_call(
        _mixout_kernel,
        grid=(T // tm,),
        in_specs=[row(DIL_OUT), row(DIL_OUT), row(DIL_OUT), row(GROUP_WIDTH), row(GROUP_WIDTH),
                  row(GROUP_WIDTH), row(GROUP_WIDTH, COL_SU // GROUP_WIDTH), row(D_MODEL),
                  const((1, GROUP_WIDTH)), const((GROUP_WIDTH, GROUP_WIDTH)), const((1, GROUP_WIDTH)),
                  const((4, GROUP_WIDTH)), const((D_MODEL, D_MODEL))],
        out_specs=row(D_MODEL),
        out_shape=jax.ShapeDtypeStruct((T, D_MODEL), F32),
        compiler_params=_cparams(("parallel",)),
    )(d1, d2, d3, y_nsa, y_gla, y_s5, proj, x2, s5_d.reshape(1, -1), glu_w.astype(BF16),
      glu_b.reshape(1, -1), out_g.reshape(4, GROUP_WIDTH), w_out.astype(BF16))


def _swiglu_chunks(h, wg_ref, wu_ref, wd_ref, acc, width, chunk, lead=()):
    for c in range(width // chunk):
        cs = slice(c * chunk, (c + 1) * chunk)
        a = _dot(h, wg_ref[lead + (slice(None), cs)])
        u = _dot(h, wu_ref[lead + (slice(None), cs)])
        acc = acc + _dot((a * _sigmoid(a) * u).astype(BF16), wd_ref[lead + (cs, slice(None))])
    return acc


def _ffn_kernel(x_ref, g_ref, wg_ref, wu_ref, wd_ref, o_ref):
    x = x_ref[...]
    h = (_rms(x) * g_ref[...]).astype(BF16)
    o_ref[...] = _swiglu_chunks(h, wg_ref, wu_ref, wd_ref, x, D_FF_PAD, D_FF_PAD // 2)


def _resident(shape, index_map):
    return pl.BlockSpec(shape, index_map, pipeline_mode=pl.Buffered(1))


def _ffn(x2, g, wg, wu, wd):
    T = x2.shape[0]
    tm = 512
    pad = D_FF_PAD - D_FF
    wg_p = jnp.pad(wg, ((0, 0), (0, pad))).astype(BF16)
    wu_p = jnp.pad(wu, ((0, 0), (0, pad))).astype(BF16)
    wd_p = jnp.pad(wd, ((0, pad), (0, 0))).astype(BF16)
    return pl.pallas_call(
        _ffn_kernel,
        grid=(T // tm,),
        in_specs=[pl.BlockSpec((tm, D_MODEL), lambda i: (i, 0)),
                  pl.BlockSpec((1, D_MODEL), lambda i: (0, 0)),
                  _resident((D_MODEL, D_FF_PAD), lambda i: (0, 0)),
                  _resident((D_MODEL, D_FF_PAD), lambda i: (0, 0)),
                  _resident((D_FF_PAD, D_MODEL), lambda i: (0, 0))],
        out_specs=pl.BlockSpec((tm, D_MODEL), lambda i: (i, 0)),
        out_shape=jax.ShapeDtypeStruct((T, D_MODEL), F32),
        compiler_params=_cparams(("parallel",)),
    )(x2, g.reshape(1, D_MODEL), wg_p, wu_p, wd_p)


MOE_TILE = 512
CMB_TILE = 128


def _router_kernel(x_ref, g_ref, w_ref, b_ref, h_ref, r_ref):
    h = _rms(x_ref[...]) * g_ref[...]
    h_ref[...] = h
    lane = _iota((h.shape[0], LANE), 1)
    logits = jnp.where(lane < N_EXPERTS, _dot3(h, w_ref[...]) + b_ref[...], -jnp.inf)
    m1 = jnp.max(logits, axis=-1, keepdims=True)
    i1 = jnp.min(jnp.where(logits == m1, lane, LANE), axis=-1, keepdims=True)
    rest = jnp.where(lane == i1, -jnp.inf, logits)
    m2 = jnp.max(rest, axis=-1, keepdims=True)
    i2 = jnp.min(jnp.where(rest == m2, lane, LANE), axis=-1, keepdims=True)
    e2 = jnp.exp(m2 - m1)
    w1 = 1.0 / (1.0 + e2)
    w2 = e2 / (1.0 + e2)
    r_ref[...] = jnp.where(lane == 0, i1.astype(F32),
                           jnp.where(lane == 1, i2.astype(F32),
                                     jnp.where(lane == 2, w1, jnp.where(lane == 3, w2, 0.0))))


def _router(x2, g, rw, rb):
    T = x2.shape[0]
    tm = 512
    rw_p = jnp.pad(rw, ((0, 0), (0, LANE - N_EXPERTS)))
    rb_p = jnp.pad(rb, (0, LANE - N_EXPERTS)).reshape(1, LANE)
    return pl.pallas_call(
        _router_kernel,
        grid=(T // tm,),
        in_specs=[pl.BlockSpec((tm, D_MODEL), lambda i: (i, 0)),
                  pl.BlockSpec((1, D_MODEL), lambda i: (0, 0)),
                  pl.BlockSpec((D_MODEL, LANE), lambda i: (0, 0)),
                  pl.BlockSpec((1, LANE), lambda i: (0, 0))],
        out_specs=[pl.BlockSpec((tm, D_MODEL), lambda i: (i, 0)),
                   pl.BlockSpec((tm, LANE), lambda i: (i, 0))],
        out_shape=[jax.ShapeDtypeStruct((T, D_MODEL), F32), jax.ShapeDtypeStruct((T, LANE), F32)],
        compiler_params=_cparams(("parallel",)),
    )(x2, g.reshape(1, D_MODEL), rw_p, rb_p)


def _row_copy(src_hbm, row, dst, slot, r, sem):
    return pltpu.make_async_copy(src_hbm.at[pl.ds(row, 1), :], dst.at[slot, pl.ds(r, 1), :], sem.at[slot, r])


def _gather_rows(idx_ref, src_hbm, dst, slot, sem, n):
    def body(r, _):
        _row_copy(src_hbm, idx_ref[0, 0, r], dst, slot, r, sem).start()
        return 0
    lax.fori_loop(0, n, body, 0)


def _wait_rows(idx_ref, src_hbm, dst, slot, sem, n):
    def body(r, _):
        _row_copy(src_hbm, idx_ref[0, 0, r], dst, slot, r, sem).wait()
        return 0
    lax.fori_loop(0, n, body, 0)


def _moe_kernel(texp_ref, cur_ref, nxt_ref, h_hbm, wg_ref, wu_ref, wd_ref, y_ref, xbuf, sem):
    i = pl.program_id(0)
    nt = pl.num_programs(0)
    slot = i % 2

    @pl.when(i == 0)
    def _():
        _gather_rows(cur_ref, h_hbm, xbuf, 0, sem, MOE_TILE)

    @pl.when(i + 1 < nt)
    def _():
        _gather_rows(nxt_ref, h_hbm, xbuf, 1 - slot, sem, MOE_TILE)

    _wait_rows(cur_ref, h_hbm, xbuf, slot, sem, MOE_TILE)
    h = xbuf[slot].astype(BF16)
    y_ref[...] = _swiglu_chunks(h, wg_ref, wu_ref, wd_ref, jnp.zeros((MOE_TILE, D_MODEL), F32),
                                D_FF_EXPERT, D_FF_EXPERT // 4, lead=(0,))


def _moe_experts(h, src_rows, tile_expert, wg, wu, wd):
    nt = tile_expert.shape[0]
    src3 = src_rows.reshape(nt, 1, MOE_TILE)
    grid_spec = pltpu.PrefetchScalarGridSpec(
        num_scalar_prefetch=1,
        grid=(nt,),
        in_specs=[
            pl.BlockSpec((1, 1, MOE_TILE), lambda i, te: (i, 0, 0), memory_space=pltpu.SMEM),
            pl.BlockSpec((1, 1, MOE_TILE), lambda i, te: (jnp.minimum(i + 1, nt - 1), 0, 0),
                         memory_space=pltpu.SMEM),
            pl.BlockSpec(memory_space=pl.ANY),
            _resident((1, D_MODEL, D_FF_EXPERT), lambda i, te: (te[i], 0, 0)),
            _resident((1, D_MODEL, D_FF_EXPERT), lambda i, te: (te[i], 0, 0)),
            _resident((1, D_FF_EXPERT, D_MODEL), lambda i, te: (te[i], 0, 0)),
        ],
        out_specs=pl.BlockSpec((MOE_TILE, D_MODEL), lambda i, te: (i, 0)),
        scratch_shapes=[pltpu.VMEM((2, MOE_TILE, D_MODEL), F32),
                        pltpu.SemaphoreType.DMA((2, MOE_TILE))],
    )
    return pl.pallas_call(
        _moe_kernel,
        grid_spec=grid_spec,
        out_shape=jax.ShapeDtypeStruct((nt * MOE_TILE, D_MODEL), F32),
        compiler_params=_cparams(("arbitrary",)),
    )(tile_expert, src3, src3, h, wg.astype(BF16), wu.astype(BF16), wd.astype(BF16))


def _combine_kernel(p1_ref, p2_ref, p1n_ref, p2n_ref, y_hbm, x_ref, r_ref, o_ref, ybuf, sem):
    i = pl.program_id(0)
    nt = pl.num_programs(0)
    slot = i % 2

    def issue(pa, pb, s):
        _gather_rows(pa, y_hbm, ybuf.at[0], s, sem.at[0], CMB_TILE)
        _gather_rows(pb, y_hbm, ybuf.at[1], s, sem.at[1], CMB_TILE)

    @pl.when(i == 0)
    def _():
        issue(p1_ref, p2_ref, 0)

    @pl.when(i + 1 < nt)
    def _():
        issue(p1n_ref, p2n_ref, 1 - slot)

    _wait_rows(p1_ref, y_hbm, ybuf.at[0], slot, sem.at[0], CMB_TILE)
    _wait_rows(p2_ref, y_hbm, ybuf.at[1], slot, sem.at[1], CMB_TILE)
    r = r_ref[...]
    o_ref[...] = x_ref[...] + r[:, 2:3] * ybuf[0, slot] + r[:, 3:4] * ybuf[1, slot]


def _moe_combine(y_sorted, pos, x2, route):
    T = x2.shape[0]
    nt = T // CMB_TILE
    p1 = pos[:, 0].reshape(nt, 1, CMB_TILE)
    p2 = pos[:, 1].reshape(nt, 1, CMB_TILE)

    def smem(nxt):
        if nxt:
            return pl.BlockSpec((1, 1, CMB_TILE), lambda i: (jnp.minimum(i + 1, nt - 1), 0, 0),
                                memory_space=pltpu.SMEM)
        return pl.BlockSpec((1, 1, CMB_TILE), lambda i: (i, 0, 0), memory_space=pltpu.SMEM)

    return pl.pallas_call(
        _combine_kernel,
        grid=(nt,),
        in_specs=[smem(False), smem(False), smem(True), smem(True),
                  pl.BlockSpec(memory_space=pl.ANY),
                  pl.BlockSpec((CMB_TILE, D_MODEL), lambda i: (i, 0)),
                  pl.BlockSpec((CMB_TILE, LANE), lambda i: (i, 0))],
        out_specs=pl.BlockSpec((CMB_TILE, D_MODEL), lambda i: (i, 0)),
        out_shape=jax.ShapeDtypeStruct((T, D_MODEL), F32),
        scratch_shapes=[pltpu.VMEM((2, 2, CMB_TILE, D_MODEL), F32),
                        pltpu.SemaphoreType.DMA((2, 2, CMB_TILE))],
        compiler_params=_cparams(("arbitrary",)),
    )(p1, p2, p1, p2, y_sorted, x2, route)


def _moe(x2, g, rw, rb, wg, wu, wd):
    T = x2.shape[0]
    h, route = _router(x2, g, rw, rb)
    experts = route[:, :2].astype(jnp.int32)
    onehot = (experts.reshape(-1)[:, None] == jnp.arange(N_EXPERTS)[None, :]).astype(jnp.int32)
    csum = jnp.cumsum(onehot, axis=0)
    counts = csum[-1]
    rank = jnp.take_along_axis(csum, experts.reshape(-1, 1), axis=1)[:, 0] - 1
    padded = ((counts + MOE_TILE - 1) // MOE_TILE) * MOE_TILE
    starts = jnp.cumsum(padded) - padded
    pos = (starts[experts.reshape(-1)] + rank).reshape(T, 2)
    nt = (2 * T) // MOE_TILE + N_EXPERTS
    src_rows = jnp.zeros((nt * MOE_TILE,), jnp.int32).at[pos.reshape(-1)].set(
        jnp.repeat(jnp.arange(T, dtype=jnp.int32), 2))
    tile_start = jnp.arange(nt, dtype=jnp.int32) * MOE_TILE
    ends = jnp.cumsum(padded)
    tile_expert = jnp.minimum(jnp.sum(tile_start[:, None] >= ends[None, :], axis=1), N_EXPERTS - 1)
    y_sorted = _moe_experts(h, src_rows, tile_expert.astype(jnp.int32), wg, wu, wd)
    return _moe_combine(y_sorted, pos, x2, route)


def _layer_mixers(x2, B, S, l, p):
    proj = _inproj(x2, p["norm1_g"][l], _pad_w_in(p["w_in"][l]))
    proj3 = proj.reshape(B, S, PROJ_PAD)
    d = [_dil_pattern(proj3, p["dil_qk_g"][l], dil).reshape(B * S, DIL_OUT) for _, dil in DIL_PATTERNS]
    kc, vc, ks, vs, kw, vw = _nsa_prep(proj3, p["nsa_cmp_pos"][l], p["nsa_cmp_w1"][l], p["nsa_cmp_w2"][l],
                                       p["nsa_qk_g"][l])
    y_nsa = _nsa_attention(proj3, kc, vc, ks, vs, kw, vw, p["nsa_qk_g"][l]).reshape(B * S, GROUP_WIDTH)
    y_gla = _gla(proj3, p["gla_wa2"][l], p["gla_ba"][l], p["gla_norm_g"][l]).reshape(B * S, GROUP_WIDTH)
    y_s5 = _s5_linear(proj3[:, :, COL_SU:COL_SU + GROUP_WIDTH], p["s5_a_re"][l], p["s5_a_im"][l],
                      p["s5_b_re"][l], p["s5_b_im"][l], p["s5_c_re"][l], p["s5_c_im"][l],
                      p["s5_log_dt"][l]).reshape(B * S, GROUP_WIDTH)
    return _mixout(d[0], d[1], d[2], y_nsa, y_gla, y_s5, proj, x2, p["s5_d"][l], p["s5_glu_w"][l],
                   p["s5_glu_b"][l], p["out_norm_g"][l], p["w_out"][l])


def kernel(x, norm1_g, w_in, dil_qk_g, nsa_qk_g, nsa_cmp_pos, nsa_cmp_w1, nsa_cmp_w2, gla_wa2, gla_ba,
           gla_norm_g, s5_a_re, s5_a_im, s5_b_re, s5_b_im, s5_c_re, s5_c_im, s5_d, s5_log_dt, s5_glu_w,
           s5_glu_b, out_norm_g, w_out, norm2_g, ffn_w_gate, ffn_w_up, ffn_w_down, moe_router_w,
           moe_router_b, moe_w_gate, moe_w_up, moe_w_down):
    B, S, D = x.shape
    p = dict(norm1_g=norm1_g, w_in=w_in, dil_qk_g=dil_qk_g, nsa_qk_g=nsa_qk_g, nsa_cmp_pos=nsa_cmp_pos,
             nsa_cmp_w1=nsa_cmp_w1, nsa_cmp_w2=nsa_cmp_w2, gla_wa2=gla_wa2, gla_ba=gla_ba,
             gla_norm_g=gla_norm_g, s5_a_re=s5_a_re, s5_a_im=s5_a_im, s5_b_re=s5_b_re, s5_b_im=s5_b_im,
             s5_c_re=s5_c_re, s5_c_im=s5_c_im, s5_d=s5_d, s5_log_dt=s5_log_dt, s5_glu_w=s5_glu_w,
             s5_glu_b=s5_glu_b, out_norm_g=out_norm_g, w_out=w_out)
    x2 = x.reshape(B * S, D)
    depth = norm1_g.shape[0]
    for l in range(depth):
        x2 = _layer_mixers(x2, B, S, l, p)
        i = l // 2
        if l % 2 == 0:
            x2 = _ffn(x2, norm2_g[l], ffn_w_gate[i], ffn_w_up[i], ffn_w_down[i])
        else:
            x2 = _moe(x2, norm2_g[l], moe_router_w[i], moe_router_b[i], moe_w_gate[i], moe_w_up[i],
                      moe_w_down[i])
    return x2.reshape(B, S, D)
```

```python
import functools
import math

import jax
import jax.numpy as jnp
from jax import lax
from jax.experimental import pallas as pl
from jax.experimental.pallas import tpu as pltpu

F32 = jnp.float32
BF16 = jnp.bfloat16

D_MODEL = 1024
HEAD_DIM = 64
GROUP_WIDTH = 256
N_HEADS = 4
DIL_PATTERNS = ((128, 1), (512, 4), (2048, 16))
Q_BLOCK = 128
NSA_CMP_LEN = 32
NSA_CMP_STRIDE = 16
NSA_CMP_HIDDEN = 256
NSA_SEL_LEN = 64
NSA_TOPN = 16
NSA_WIN = 512
FORCE_BONUS = 1.0e4
GLA_DK = 32
GLA_DV = 64
GLA_RANK = 16
GLA_TAU = 16.0
GLA_CHUNK = 16
S5_GROUP = 16
S5_GROUPS = 16
S5_STATE = 64
S5_CHUNK = 16
D_FF = 2752
N_EXPERTS = 8
D_FF_EXPERT = 3584
EPS = 1e-6
NEG_INF = -1e30

DIL_SLOPES = tuple(2.0 ** (-float(i)) for i in (2, 4, 6, 8))
NSA_SLOPES = tuple(2.0 ** (-float(i)) for i in (1, 3, 5, 7))

COL_DQ, COL_DK, COL_DV, COL_NQ = 0, 256, 512, 768
COL_NCMP, COL_NK, COL_NV, COL_NG = 1024, 1152, 1280, 1408
COL_GQ, COL_GK, COL_GV, COL_GA, COL_GR, COL_SU = 1536, 1664, 1792, 2048, 2304, 2560
PROJ_PAD = 2816
LANE = 128
D_FF_PAD = 2816

VMEM_LIMIT = 56 * 1024 * 1024


def _cparams(sem):
    return pltpu.CompilerParams(dimension_semantics=sem, vmem_limit_bytes=VMEM_LIMIT)


def _rms(x):
    return x * lax.rsqrt(jnp.mean(x * x, axis=-1, keepdims=True) + EPS)


def _dot(a, b):
    return jnp.dot(a, b, preferred_element_type=F32)


def _dot_nt(a, b):
    return lax.dot_general(a, b, (((1,), (1,)), ((), ())), preferred_element_type=F32)


def _split(x):
    hi = x.astype(BF16)
    lo = (x - hi.astype(F32)).astype(BF16)
    return hi, lo


def _dot_lsplit(x, m):
    hi, lo = _split(x)
    return _dot(hi, m) + _dot(lo, m)


def _dot_rsplit(m, x):
    hi, lo = _split(x)
    return _dot(m, hi) + _dot(m, lo)


def _dot3(a, b):
    ah, al = _split(a)
    bh, bl = _split(b)
    return _dot(ah, bh) + _dot(ah, bl) + _dot(al, bh)


def _gelu(x):
    return 0.5 * x * (1.0 + jnp.tanh(math.sqrt(2.0 / math.pi) * (x + 0.044715 * (x * x * x))))


def _sigmoid(x):
    return 1.0 / (1.0 + jnp.exp(-x))


def _iota(shape, dim):
    return lax.broadcasted_iota(jnp.int32, shape, dim)


def _head_indicator(rows, cols, rdiv, cdiv):
    return jnp.where(_iota((rows, cols), 0) // rdiv == _iota((rows, cols), 1) // cdiv, 1.0, 0.0).astype(BF16)


def _inproj_kernel(x_ref, g_ref, w_ref, hg_ref, o_ref):
    h = _rms(x_ref[...]) * g_ref[...]
    o_ref[...] = _dot(h.astype(BF16), w_ref[...])
    ind = _head_indicator(GROUP_WIDTH, GROUP_WIDTH, HEAD_DIM, HEAD_DIM)
    for lo, width in NORM_SEGMENTS:
        v = o_ref[:, lo:lo + width]
        ms = _dot_lsplit(v * v, ind[:width, :width]) * (1.0 / HEAD_DIM)
        o_ref[:, lo:lo + width] = v * lax.rsqrt(ms + EPS) * hg_ref[:, lo:lo + width]


NORM_COLS = COL_NV
NORM_SEGMENTS = ((COL_DQ, GROUP_WIDTH), (COL_DK, GROUP_WIDTH), (COL_NQ, GROUP_WIDTH), (COL_NK, LANE))


def _head_gains(dil_g, nsa_g):
    scale = HEAD_DIM ** -0.5
    z = jnp.zeros((HEAD_DIM,), F32)
    row = jnp.concatenate([jnp.tile(dil_g[0] * scale, N_HEADS), jnp.tile(dil_g[1], N_HEADS),
                           jnp.zeros((GROUP_WIDTH,), F32), jnp.tile(nsa_g[0] * scale, N_HEADS),
                           z, z, nsa_g[2], nsa_g[3]])
    return row.reshape(1, NORM_COLS)


def _inproj(x2, g, w_pad, head_gains):
    T = x2.shape[0]
    tm = 512
    return pl.pallas_call(
        _inproj_kernel,
        grid=(T // tm,),
        in_specs=[
            pl.BlockSpec((tm, D_MODEL), lambda i: (i, 0)),
            pl.BlockSpec((1, D_MODEL), lambda i: (0, 0)),
            pl.BlockSpec((D_MODEL, PROJ_PAD), lambda i: (0, 0)),
            pl.BlockSpec((1, NORM_COLS), lambda i: (0, 0)),
        ],
        out_specs=pl.BlockSpec((tm, PROJ_PAD), lambda i: (i, 0)),
        out_shape=jax.ShapeDtypeStruct((T, PROJ_PAD), F32),
        compiler_params=_cparams(("parallel",)),
    )(x2, g.reshape(1, D_MODEL), w_pad, head_gains)


def _pad_w_in(w):
    def seg(lo, width, pad_to):
        s = w[:, lo:lo + width]
        if pad_to > width:
            s = jnp.pad(s, ((0, 0), (0, pad_to - width)))
        return s
    k_cmp, v_cmp = seg(1024, 64, 64), seg(1088, 64, 64)
    k_slc, v_slc = seg(1152, 64, 64), seg(1216, 64, 64)
    k_win, v_win = seg(1280, 64, 64), seg(1344, 64, 64)
    pieces = [
        seg(0, 256, 256), seg(256, 256, 256), seg(512, 256, 256), seg(768, 256, 256),
        k_cmp, v_cmp, k_slc, k_win, v_slc, v_win,
        seg(1408, 12, 128),
        seg(1420, 128, 128), seg(1548, 128, 128), seg(1676, 256, 256),
        seg(1932, 16, 256),
        seg(1948, 256, 256), seg(2204, 256, 256),
    ]
    out = jnp.concatenate(pieces, axis=1)
    assert out.shape[1] == PROJ_PAD
    return out.astype(BF16)


DIL_SB = Q_BLOCK * max(d for _, d in DIL_PATTERNS)
DIL_SUBSTEPS = DIL_SB // Q_BLOCK


def _dil_kernel(q0_ref, q1_ref, k0_ref, k1_ref, v0_ref, v1_ref, o_ref,
                ks_ref, vs_ref, bias_ref, m_ref, l_ref, acc_ref):
    i = pl.program_id(1)
    rows4 = N_HEADS * Q_BLOCK

    def ld2(ref, rows):
        return jnp.concatenate([ref[0, rows, :], ref[1, rows, :]], axis=1)

    def st2(ref, rows, val):
        ref[0, rows, :] = val[:, :LANE]
        ref[1, rows, :] = val[:, LANE:]

    @pl.when(i == 0)
    def _():
        ks_ref[:, 0:DIL_SB, :] = jnp.zeros((2, DIL_SB, LANE), F32)
        vs_ref[:, 0:DIL_SB, :] = jnp.zeros((2, DIL_SB, LANE), F32)
        ri = _iota((rows4, 2 * Q_BLOCK), 0)
        ci = _iota((rows4, 2 * Q_BLOCK), 1)
        delta = Q_BLOCK + ri % Q_BLOCK - ci
        head = ri // Q_BLOCK
        slope = jnp.where(head == 0, DIL_SLOPES[0],
                          jnp.where(head == 1, DIL_SLOPES[1],
                                    jnp.where(head == 2, DIL_SLOPES[2], DIL_SLOPES[3]))).astype(F32)
        band = (delta >= 0) & (delta <= Q_BLOCK)
        for p, (_, dil) in enumerate(DIL_PATTERNS):
            b = jnp.where(band, -(slope * dil) * delta.astype(F32), NEG_INF)
            bias_ref[p, 0] = b
            bias_ref[p, 1] = jnp.where(ci >= Q_BLOCK, b, NEG_INF)

    @pl.when(i > 0)
    def _():
        ks_ref[:, 0:DIL_SB, :] = ks_ref[:, DIL_SB:, :]
        vs_ref[:, 0:DIL_SB, :] = vs_ref[:, DIL_SB:, :]

    ks_ref[0, DIL_SB:, :] = k0_ref[0]
    ks_ref[1, DIL_SB:, :] = k1_ref[0]
    vs_ref[0, DIL_SB:, :] = v0_ref[0]
    vs_ref[1, DIL_SB:, :] = v1_ref[0]
    lane_head = _iota((Q_BLOCK, GROUP_WIDTH), 1) // HEAD_DIM

    def spread(x):
        out = jnp.zeros((Q_BLOCK, GROUP_WIDTH), F32)
        for h in range(N_HEADS):
            out = jnp.where(lane_head == h, x[h * Q_BLOCK:(h + 1) * Q_BLOCK], out)
        return out

    for p, (_, dil) in enumerate(DIL_PATTERNS):
        def sub(j, carry, p=p, dil=dil):
            start = (j // dil) * (Q_BLOCK * dil) + j % dil
            rows = pl.ds(start, Q_BLOCK, stride=dil)
            cur = pl.ds(DIL_SB + start, Q_BLOCK, stride=dil)
            prev = pl.ds(DIL_SB + start - Q_BLOCK * dil, Q_BLOCK, stride=dil)
            q = jnp.concatenate([q0_ref[0, rows, :], q1_ref[0, rows, :]], axis=1)
            q4 = jnp.concatenate([jnp.where(lane_head == h, q, 0.0) for h in range(N_HEADS)], axis=0)
            kk = jnp.concatenate([ld2(ks_ref, prev), ld2(ks_ref, cur)], axis=0).astype(BF16)
            vv = jnp.concatenate([ld2(vs_ref, prev), ld2(vs_ref, cur)], axis=0).astype(BF16)
            first = jnp.where((i == 0) & (j < dil), 1, 0)
            s = _dot_nt(q4.astype(BF16), kk) + bias_ref[p, first]
            m = jnp.max(s, axis=-1, keepdims=True)
            e = jnp.exp(s - m)
            l = jnp.sum(e, axis=-1, keepdims=True)
            o_t = spread(_dot(e.astype(BF16), vv))
            m_t = spread(m)
            l_t = spread(l)
            if p == 0:
                st2(m_ref, rows, m_t)
                st2(l_ref, rows, l_t)
                st2(acc_ref, rows, o_t)
            else:
                m_old = ld2(m_ref, rows)
                m_new = jnp.maximum(m_old, m_t)
                a = jnp.exp(m_old - m_new)
                b = jnp.exp(m_t - m_new)
                st2(m_ref, rows, m_new)
                st2(l_ref, rows, a * ld2(l_ref, rows) + b * l_t)
                st2(acc_ref, rows, a * ld2(acc_ref, rows) + b * o_t)
            return carry
        lax.fori_loop(0, DIL_SUBSTEPS, sub, 0)
    o_ref[0] = jnp.concatenate([acc_ref[0] / l_ref[0], acc_ref[1] / l_ref[1]], axis=1)


def _dil_attention(proj3):
    B, S, _ = proj3.shape
    assert all(win // d == Q_BLOCK for win, d in DIL_PATTERNS) and S % DIL_SB == 0

    def halves(c):
        return [pl.BlockSpec((1, DIL_SB, LANE), lambda b, i, c=c, h=h: (b, i, c // LANE + h)) for h in range(2)]

    sb = pltpu.VMEM((2, DIL_SB, LANE), F32)
    sb2 = pltpu.VMEM((2, 2 * DIL_SB, LANE), F32)
    return pl.pallas_call(
        _dil_kernel,
        grid=(B, S // DIL_SB),
        in_specs=halves(COL_DQ) + halves(COL_DK) + halves(COL_DV),
        out_specs=pl.BlockSpec((1, DIL_SB, GROUP_WIDTH), lambda b, i: (b, i, 0)),
        out_shape=jax.ShapeDtypeStruct((B, S, GROUP_WIDTH), F32),
        scratch_shapes=[sb2, sb2, pltpu.VMEM((len(DIL_PATTERNS), 2, N_HEADS * Q_BLOCK, 2 * Q_BLOCK), F32),
                        sb, sb, sb],
        compiler_params=_cparams(("parallel", "arbitrary")),
    )(*([proj3] * 6))


def _nsa_prep_kernel(xk_ref, xv_ref, kraw_ref, vraw_ref, pos_ref, w1_ref, w2_ref, g_ref,
                     kc_ref, vc_ref, ks_ref, vs_ref, kw_ref, vw_ref):
    half = NSA_CMP_STRIDE * HEAD_DIM

    def compress(x, idx):
        w1 = w1_ref[idx]
        xb = x.astype(BF16)
        a = _dot(xb, w1[:half, :])
        b = _dot(xb, w1[half:, :])
        bias = _dot(pos_ref[idx].astype(BF16), w1)[0:1, :]
        pre = a + pltpu.roll(b, b.shape[0] - 1, 0) + bias
        return _dot(_gelu(pre).astype(BF16), w2_ref[idx])

    def with_pos(k, pos):
        lane = _iota((k.shape[0], HEAD_DIM), 1)
        extra = jnp.where(lane == 0, (pos // 256).astype(F32), jnp.where(lane == 1, (pos % 256).astype(F32), 0.0))
        return jnp.concatenate([k, extra], axis=1).astype(BF16)

    def with_one(v):
        lane = _iota((v.shape[0], HEAD_DIM), 1)
        return jnp.concatenate([v, jnp.where(lane == 0, 1.0, 0.0)], axis=1).astype(BF16)

    ng = xk_ref.shape[1]
    cend = _iota((ng, 1), 0) * NSA_CMP_STRIDE + (NSA_CMP_LEN - 1)
    kc_ref[0] = with_pos(_rms(compress(xk_ref[0], 0)) * g_ref[1:2, :], cend)
    vc_ref[0] = with_one(compress(xv_ref[0], 1))
    kraw = kraw_ref[0]
    vraw = vraw_ref[0]
    tpos = _iota((kraw.shape[0], 1), 0)
    ks_ref[0] = with_pos(kraw[:, :HEAD_DIM], tpos)
    kw_ref[0] = with_pos(kraw[:, HEAD_DIM:], tpos)
    vs_ref[0] = with_one(vraw[:, :HEAD_DIM])
    vw_ref[0] = with_one(vraw[:, HEAD_DIM:])


def _nsa_prep(proj3, pos, w1, w2, g):
    B, S, _ = proj3.shape
    ng = S // NSA_CMP_STRIDE
    gw = NSA_CMP_STRIDE * HEAD_DIM
    xk = proj3[:, :, COL_NCMP:COL_NCMP + HEAD_DIM].reshape(B, ng, gw)
    xv = proj3[:, :, COL_NCMP + HEAD_DIM:COL_NCMP + 2 * HEAD_DIM].reshape(B, ng, gw)
    pos8 = jnp.broadcast_to(pos.reshape(2, 1, NSA_CMP_LEN * HEAD_DIM), (2, 8, NSA_CMP_LEN * HEAD_DIM))
    small = jax.ShapeDtypeStruct((B, ng, LANE), BF16)
    big = jax.ShapeDtypeStruct((B, S, LANE), BF16)
    small_spec = pl.BlockSpec((1, ng, LANE), lambda b: (b, 0, 0))
    big_spec = pl.BlockSpec((1, S, LANE), lambda b: (b, 0, 0))
    return pl.pallas_call(
        _nsa_prep_kernel,
        grid=(B,),
        in_specs=[
            pl.BlockSpec((1, ng, gw), lambda b: (b, 0, 0)),
            pl.BlockSpec((1, ng, gw), lambda b: (b, 0, 0)),
            pl.BlockSpec((1, S, LANE), lambda b: (b, 0, COL_NK // LANE)),
            pl.BlockSpec((1, S, LANE), lambda b: (b, 0, COL_NV // LANE)),
            pl.BlockSpec((2, 8, NSA_CMP_LEN * HEAD_DIM), lambda b: (0, 0, 0)),
            pl.BlockSpec((2, NSA_CMP_LEN * HEAD_DIM, NSA_CMP_HIDDEN), lambda b: (0, 0, 0)),
            pl.BlockSpec((2, NSA_CMP_HIDDEN, HEAD_DIM), lambda b: (0, 0, 0)),
            pl.BlockSpec((4, HEAD_DIM), lambda b: (0, 0)),
        ],
        out_specs=[small_spec, small_spec, big_spec, big_spec, big_spec, big_spec],
        out_shape=[small, small, big, big, big, big],
        compiler_params=_cparams(("parallel",)),
    )(xk, xv, proj3, proj3, pos8, w1.astype(BF16), w2.astype(BF16), g)


def _nsa_kernel(q_ref, gate_ref, kc_ref, vc_ref, ks_ref, vs_ref, kw_ref, vw_ref, o_ref, *, seq):
    i = pl.program_id(1)
    t0 = i * Q_BLOCK
    rows = N_HEADS * Q_BLOCK
    qb = q_ref[0]
    lane = _iota((Q_BLOCK, HEAD_DIM), 1)
    q = jnp.concatenate(
        [jnp.concatenate([qb[:, h * HEAD_DIM:(h + 1) * HEAD_DIM],
                          jnp.where(lane == 0, NSA_SLOPES[h] * 256.0, jnp.where(lane == 1, NSA_SLOPES[h], 0.0))],
                         axis=1) for h in range(N_HEADS)], axis=0).astype(BF16)
    tq = t0 + _iota((Q_BLOCK, 1), 0)

    ncmp = kc_ref.shape[1]
    cend = _iota((1, ncmp), 1) * NSA_CMP_STRIDE + (NSA_CMP_LEN - 1)
    valid = tq >= cend
    s = jnp.where(valid[None], _dot_nt(q, kc_ref[0]).reshape(N_HEADS, Q_BLOCK, ncmp), NEG_INF)
    e = jnp.exp(s - jnp.max(s, axis=-1, keepdims=True))
    row_ok = jnp.where(tq >= NSA_CMP_LEN - 1, 1.0, 0.0)
    p = e * (row_ok / jnp.sum(e, axis=-1, keepdims=True))
    o_cmp = _dot(p.reshape(rows, ncmp).astype(BF16), vc_ref[0])[:, :HEAD_DIM]

    nsel = seq // NSA_SEL_LEN
    psum = p[0] + p[1] + p[2] + p[3]
    jj = _iota((nsel, ncmp), 0)
    cc = _iota((nsel, ncmp), 1)
    ratio = NSA_SEL_LEN // NSA_CMP_STRIDE
    cover_t = jnp.where((cc < ratio * jj + ratio) & (cc * NSA_CMP_STRIDE + NSA_CMP_LEN > jj * NSA_SEL_LEN),
                        1.0, 0.0).astype(BF16)
    p_hi, p_lo = _split(psum)
    imp = _dot_nt(cover_t, p_hi) + _dot_nt(cover_t, p_lo)
    jrow = _iota((nsel, Q_BLOCK), 0)
    cur = (t0 + _iota((nsel, Q_BLOCK), 1)) // NSA_SEL_LEN
    forced = (jrow == 0) | (jrow == cur) | (jrow == cur - 1)
    impv = jnp.where(jrow <= cur, imp + jnp.where(forced, FORCE_BONUS, 0.0), NEG_INF)
    rank = jnp.zeros((nsel, Q_BLOCK), F32)
    for ii in range(nsel):
        row = impv[ii:ii + 1, :]
        tie = jnp.where(jrow > ii, 1.0, 0.0)
        rank = rank + jnp.where(row > impv, 1.0, jnp.where(row == impv, tie, 0.0))
    sel = jnp.where((rank < min(NSA_TOPN, nsel)) & (jrow <= cur), 1.0, 0.0).T.astype(BF16)

    kchunk = 2 * Q_BLOCK
    jrow_k = _iota((nsel, kchunk), 0)
    kcol = _iota((nsel, kchunk), 1)
    kpos_row = _iota((1, kchunk), 1)

    def sel_body(c, carry):
        m_i, acc = carry
        k0 = pl.multiple_of(c * kchunk, kchunk)
        kk = ks_ref[0, pl.ds(k0, kchunk), :]
        vv = vs_ref[0, pl.ds(k0, kchunk), :]
        expand = jnp.where((k0 + kcol) // NSA_SEL_LEN == jrow_k, 1.0, 0.0).astype(BF16)
        ok = (_dot(sel, expand) > 0.5) & (k0 + kpos_row <= tq)
        sc = jnp.where(ok[None], _dot_nt(q, kk).reshape(N_HEADS, Q_BLOCK, kchunk), NEG_INF)
        m_new = jnp.maximum(m_i, jnp.max(sc, axis=-1, keepdims=True))
        alpha = jnp.exp(m_i - m_new).reshape(rows, 1)
        pc = jnp.exp(sc - m_new).reshape(rows, kchunk).astype(BF16)
        return m_new, alpha * acc + _dot(pc, vv)

    init = (jnp.full((N_HEADS, Q_BLOCK, 1), NEG_INF, F32), jnp.zeros((rows, LANE), F32))
    _, acc_s = lax.fori_loop(0, (t0 + Q_BLOCK - 1) // kchunk + 1, sel_body, init)
    o_sel = acc_s[:, :HEAD_DIM] / acc_s[:, HEAD_DIM:HEAD_DIM + 1]

    span = NSA_WIN + Q_BLOCK
    start = pl.multiple_of(jnp.maximum(t0 - NSA_WIN, 0), Q_BLOCK)
    kw = kw_ref[0, pl.ds(start, span), :]
    vw = vw_ref[0, pl.ds(start, span), :]
    dw = tq - (start + _iota((1, span), 1))
    okw = (dw >= 0) & (dw < NSA_WIN)
    sw = jnp.where(okw[None], _dot_nt(q, kw).reshape(N_HEADS, Q_BLOCK, span), NEG_INF)
    pw = jnp.exp(sw - jnp.max(sw, axis=-1, keepdims=True)).reshape(rows, span).astype(BF16)
    rw = _dot(pw, vw)
    o_win = rw[:, :HEAD_DIM] / rw[:, HEAD_DIM:HEAD_DIM + 1]

    gate = _sigmoid(gate_ref[0])
    outs = []
    for h in range(N_HEADS):
        r = slice(h * Q_BLOCK, (h + 1) * Q_BLOCK)
        outs.append(gate[:, 3 * h:3 * h + 1] * o_cmp[r] + gate[:, 3 * h + 1:3 * h + 2] * o_sel[r]
                    + gate[:, 3 * h + 2:3 * h + 3] * o_win[r])
    o_ref[0] = jnp.concatenate(outs, axis=1)


def _nsa_attention(proj3, kc, vc, ks, vs, kw, vw):
    B, S, _ = proj3.shape
    nb = S // Q_BLOCK
    ng = kc.shape[1]
    small_spec = pl.BlockSpec((1, ng, LANE), lambda b, i: (b, 0, 0))
    big_spec = pl.BlockSpec((1, S, LANE), lambda b, i: (b, 0, 0))
    return pl.pallas_call(
        functools.partial(_nsa_kernel, seq=S),
        grid=(B, nb),
        in_specs=[
            pl.BlockSpec((1, Q_BLOCK, GROUP_WIDTH), lambda b, i: (b, i, COL_NQ // GROUP_WIDTH)),
            pl.BlockSpec((1, Q_BLOCK, LANE), lambda b, i: (b, i, COL_NG // LANE)),
            small_spec, small_spec, big_spec, big_spec, big_spec, big_spec,
        ],
        out_specs=pl.BlockSpec((1, Q_BLOCK, GROUP_WIDTH), lambda b, i: (b, i, 0)),
        out_shape=jax.ShapeDtypeStruct((B, S, GROUP_WIDTH), F32),
        compiler_params=_cparams(("parallel", "parallel")),
    )(proj3, proj3, kc, vc, ks, vs, kw, vw)


GLA_BLOCK = 128
GLA_W = N_HEADS * GLA_DK


def _gla_kernel(q_ref, k_ref, v_ref, a_ref, r_ref, wa_ref, ba_ref, g_ref, o_ref, st_ref):
    nsub = GLA_BLOCK // GLA_CHUNK

    @pl.when(pl.program_id(1) == 0)
    def _():
        st_ref[...] = jnp.zeros_like(st_ref)

    q = q_ref[0] * (GLA_DK ** -0.5)
    k = k_ref[0]
    v = v_ref[0]
    z = _dot3(a_ref[0], wa_ref[...]) + ba_ref[...]
    log_a = (jnp.minimum(z, 0.0) - jnp.log(1.0 + jnp.exp(-jnp.abs(z)))) * (1.0 / GLA_TAU)
    ri = _iota((GLA_BLOCK, GLA_BLOCK), 0)
    ci = _iota((GLA_BLOCK, GLA_BLOCK), 1)
    same = (ri // GLA_CHUNK) == (ci // GLA_CHUNK)
    b_loc = _dot_rsplit(jnp.where(same & (ci <= ri), 1.0, 0.0).astype(BF16), log_a)
    b_tot = _dot_rsplit(jnp.where(same, 1.0, 0.0).astype(BF16), log_a)
    qd = q * jnp.exp(b_loc)
    kd = k * jnp.exp(b_tot - b_loc)

    ind_kv = jnp.where(_iota((GLA_W, GROUP_WIDTH), 0) // GLA_DK == _iota((GLA_W, GROUP_WIDTH), 1) // GLA_DV,
                       1.0, 0.0).astype(BF16)
    rsub = _iota((GLA_BLOCK, 1), 0) % GLA_CHUNK
    o = jnp.zeros((GLA_BLOCK, GROUP_WIDTH), F32)
    for s in range(GLA_CHUNK):
        if s == 0:
            ks_, bs_, vs_ = k, b_loc, v
        else:
            ks_, bs_, vs_ = pltpu.roll(k, s, 0), pltpu.roll(b_loc, s, 0), pltpu.roll(v, s, 0)
        dec = jnp.where(rsub >= s, jnp.exp(jnp.minimum(b_loc - bs_, 0.0)), 0.0)
        w = _dot((q * ks_ * dec).astype(BF16), ind_kv)
        o = o + w * vs_

    v_t = v.T.astype(BF16)
    head_ok = _iota((GROUP_WIDTH, GLA_W), 0) // GLA_DV == _iota((GROUP_WIDTH, GLA_W), 1) // GLA_DK
    rblk = _iota((GLA_BLOCK, 1), 0) // GLA_CHUNK
    st = st_ref[...]
    for c in range(nsub):
        in_c = rblk == c
        o = o + _dot_nt(jnp.where(in_c, qd, 0.0).astype(BF16), st.astype(BF16))
        decay = jnp.exp(b_tot[c * GLA_CHUNK:c * GLA_CHUNK + 1, :])
        upd = _dot(v_t, jnp.where(in_c, kd, 0.0).astype(BF16))
        st = st * decay + jnp.where(head_ok, upd, 0.0)
    st_ref[...] = st

    ind_vv = jnp.where(_iota((GROUP_WIDTH, GROUP_WIDTH), 0) // GLA_DV == _iota((GROUP_WIDTH, GROUP_WIDTH), 1) // GLA_DV,
                       1.0, 0.0).astype(BF16)
    ms = _dot_lsplit(o * o, ind_vv) * (1.0 / GLA_DV)
    r = r_ref[0]
    o_ref[0] = o * lax.rsqrt(ms + EPS) * g_ref[...] * (r * _sigmoid(r))


def _gla(proj3, wa2, ba, norm_g):
    B, S, _ = proj3.shape
    nb = S // GLA_BLOCK
    wa_pad = jnp.zeros((LANE, GLA_W), F32).at[:GLA_RANK].set(wa2)
    g_t = jnp.tile(norm_g.reshape(1, GLA_DV), (1, N_HEADS))

    def col(c, w):
        return pl.BlockSpec((1, GLA_BLOCK, w), lambda b, j: (b, j, c // w))

    return pl.pallas_call(
        _gla_kernel,
        grid=(B, nb),
        in_specs=[col(COL_GQ, LANE), col(COL_GK, LANE), col(COL_GV, GROUP_WIDTH), col(COL_GA, LANE),
                  col(COL_GR, GROUP_WIDTH),
                  pl.BlockSpec((LANE, GLA_W), lambda b, j: (0, 0)),
                  pl.BlockSpec((1, GLA_W), lambda b, j: (0, 0)),
                  pl.BlockSpec((1, GROUP_WIDTH), lambda b, j: (0, 0))],
        out_specs=pl.BlockSpec((1, GLA_BLOCK, GROUP_WIDTH), lambda b, j: (b, j, 0)),
        out_shape=jax.ShapeDtypeStruct((B, S, GROUP_WIDTH), F32),
        scratch_shapes=[pltpu.VMEM((GROUP_WIDTH, GLA_W), F32)],
        compiler_params=_cparams(("parallel", "arbitrary")),
    )(proj3, proj3, proj3, proj3, proj3, wa_pad, ba.reshape(1, GLA_W), g_t)


S5_CW = S5_CHUNK * S5_GROUP
S5_SW = 2 * S5_STATE


def _s5_matrices(a_re, a_im, b_re, b_im, c_re, c_im, log_dt):
    hp = lax.Precision.HIGHEST
    L = S5_CHUNK
    dt = jnp.exp(log_dt)[:, None]
    lam_re, lam_im = dt * a_re, dt * a_im
    tau = jnp.arange(L + 1, dtype=F32)[:, None, None]
    mag = jnp.exp(tau * lam_re)
    p_re, p_im = mag * jnp.cos(tau * lam_im), mag * jnp.sin(tau * lam_im)
    den = a_re * a_re + a_im * a_im
    f_re = ((p_re[1] - 1.0) * a_re + p_im[1] * a_im) / den
    f_im = (p_im[1] * a_re - (p_re[1] - 1.0) * a_im) / den
    bb_re = f_re[..., None] * b_re - f_im[..., None] * b_im
    bb_im = f_re[..., None] * b_im + f_im[..., None] * b_re
    pb_re = p_re[..., None] * bb_re - p_im[..., None] * bb_im
    pb_im = p_re[..., None] * bb_im + p_im[..., None] * bb_re
    kern = (jnp.einsum('gon,tgnc->tgoc', c_re, pb_re, precision=hp)
            - jnp.einsum('gon,tgnc->tgoc', c_im, pb_im, precision=hp))
    s_idx = jnp.arange(L)[:, None]
    t_idx = jnp.arange(L)[None, :]
    lag = jnp.clip(t_idx - s_idx, 0, L)
    toep = jnp.where((t_idx >= s_idx)[..., None, None, None], kern[lag], 0.0)
    toep = toep.transpose(2, 0, 4, 1, 3).reshape(S5_GROUPS, S5_CW, S5_CW)
    rev = (L - 1 - jnp.arange(L))
    bin_re = pb_re[rev].transpose(1, 0, 3, 2).reshape(S5_GROUPS, S5_CW, S5_STATE)
    bin_im = pb_im[rev].transpose(1, 0, 3, 2).reshape(S5_GROUPS, S5_CW, S5_STATE)
    b_in = jnp.concatenate([bin_re, bin_im], axis=-1)
    pt_re, pt_im = p_re[1:], p_im[1:]
    co_re = c_re[None] * pt_re[:, :, None, :] - c_im[None] * pt_im[:, :, None, :]
    co_im = -c_re[None] * pt_im[:, :, None, :] - c_im[None] * pt_re[:, :, None, :]
    c_out = jnp.concatenate([co_re.transpose(1, 3, 0, 2), co_im.transpose(1, 3, 0, 2)], axis=1)
    c_out = c_out.reshape(S5_GROUPS, S5_SW, S5_CW)
    a_chunk = jnp.stack([p_re[L].reshape(1, -1), p_im[L].reshape(1, -1)], axis=0)
    return jnp.concatenate([toep, b_in], axis=-1).astype(BF16), c_out.astype(BF16), a_chunk


def _s5_in_kernel(u_ref, tb_ref, y_ref, v_ref):
    r = _dot(u_ref[0].astype(BF16), tb_ref[0])
    y_ref[0] = r[:, :S5_CW]
    v_ref[0] = r[:, S5_CW:]


def _s5_scan_kernel(v_ref, a_ref, x_ref):
    nb = v_ref.shape[2]
    a_r = jnp.broadcast_to(a_ref[0], (nb, a_ref.shape[2]))
    a_i = jnp.broadcast_to(a_ref[1], (nb, a_ref.shape[2]))

    def body(m, carry):
        x_r, x_i = carry
        x_ref[m, 0] = x_r
        x_ref[m, 1] = x_i
        return a_r * x_r - a_i * x_i + v_ref[m, 0], a_r * x_i + a_i * x_r + v_ref[m, 1]

    zero = jnp.zeros(a_r.shape, F32)
    lax.fori_loop(0, v_ref.shape[0], body, (zero, zero))


def _s5_out_kernel(y_ref, x_ref, c_ref, o_ref):
    o_ref[0] = y_ref[0] + _dot(x_ref[0].astype(BF16), c_ref[0])


def _s5_linear(u3, a_re, a_im, b_re, b_im, c_re, c_im, log_dt):
    B, S, _ = u3.shape
    M = S // S5_CHUNK
    G = S5_GROUPS
    tb, c_out, a_chunk = _s5_matrices(a_re, a_im, b_re, b_im, c_re, c_im, log_dt)
    ug = u3.reshape(B, M, S5_CHUNK, G, S5_GROUP).transpose(3, 0, 1, 2, 4).reshape(G, B * M, S5_CW)
    y_in, v = pl.pallas_call(
        _s5_in_kernel,
        grid=(G,),
        in_specs=[pl.BlockSpec((1, B * M, S5_CW), lambda g: (g, 0, 0)),
                  pl.BlockSpec((1, S5_CW, S5_CW + S5_SW), lambda g: (g, 0, 0))],
        out_specs=[pl.BlockSpec((1, B * M, S5_CW), lambda g: (g, 0, 0)),
                   pl.BlockSpec((1, B * M, S5_SW), lambda g: (g, 0, 0))],
        out_shape=[jax.ShapeDtypeStruct((G, B * M, S5_CW), F32),
                   jax.ShapeDtypeStruct((G, B * M, S5_SW), F32)],
        compiler_params=_cparams(("parallel",)),
    )(ug, tb)
    gn = G * S5_STATE
    v_t = v.reshape(G, B, M, 2, S5_STATE).transpose(2, 3, 1, 0, 4).reshape(M, 2, B, gn)
    lw = 256
    x_t = pl.pallas_call(
        _s5_scan_kernel,
        grid=(gn // lw,),
        in_specs=[pl.BlockSpec((M, 2, B, lw), lambda j: (0, 0, 0, j)),
                  pl.BlockSpec((2, 1, lw), lambda j: (0, 0, j))],
        out_specs=pl.BlockSpec((M, 2, B, lw), lambda j: (0, 0, 0, j)),
        out_shape=jax.ShapeDtypeStruct((M, 2, B, gn), F32),
        compiler_params=_cparams(("parallel",)),
    )(v_t, a_chunk)
    xg = x_t.reshape(M, 2, B, G, S5_STATE).transpose(3, 2, 0, 1, 4).reshape(G, B * M, S5_SW)
    y = pl.pallas_call(
        _s5_out_kernel,
        grid=(G,),
        in_specs=[pl.BlockSpec((1, B * M, S5_CW), lambda g: (g, 0, 0)),
                  pl.BlockSpec((1, B * M, S5_SW), lambda g: (g, 0, 0)),
                  pl.BlockSpec((1, S5_SW, S5_CW), lambda g: (g, 0, 0))],
        out_specs=pl.BlockSpec((1, B * M, S5_CW), lambda g: (g, 0, 0)),
        out_shape=jax.ShapeDtypeStruct((G, B * M, S5_CW), F32),
        compiler_params=_cparams(("parallel",)),
    )(y_in, xg, c_out)
    return y.reshape(G, B, M, S5_CHUNK, S5_GROUP).transpose(1, 2, 3, 0, 4).reshape(B, S, GROUP_WIDTH)


def _mixout_kernel(dil_ref, nsa_ref, gla_ref, s5y_ref, u_ref, x_ref,
                   s5d_ref, gw_ref, gb_ref, ng_ref, wo_ref, o_ref):
    u = u_ref[...]
    hg = _gelu(s5y_ref[...] + s5d_ref[...] * u)
    y_s5 = hg * _sigmoid(_dot(hg.astype(BF16), gw_ref[...]) + gb_ref[...])
    parts = [dil_ref[...], nsa_ref[...], gla_ref[...], y_s5]
    mix = jnp.concatenate([(_rms(parts[n]) * ng_ref[n:n + 1, :]).astype(BF16) for n in range(4)], axis=1)
    o_ref[...] = x_ref[...] + _dot(mix, wo_ref[...])


def _mixout(y_dil, y_nsa, y_gla, y_s5, proj, x2, s5_d, glu_w, glu_b, out_g, w_out):
    T = x2.shape[0]
    tm = 256

    def row(w, c=0):
        return pl.BlockSpec((tm, w), lambda i: (i, c))

    def const(shape):
        return pl.BlockSpec(shape, lambda i: (0,) * len(shape))

    return pl.pallas_call(
        _mixout_kernel,
        grid=(T // tm,),
        in_specs=[row(GROUP_WIDTH), row(GROUP_WIDTH), row(GROUP_WIDTH),
                  row(GROUP_WIDTH), row(GROUP_WIDTH, COL_SU // GROUP_WIDTH), row(D_MODEL),
                  const((1, GROUP_WIDTH)), const((GROUP_WIDTH, GROUP_WIDTH)), const((1, GROUP_WIDTH)),
                  const((4, GROUP_WIDTH)), const((D_MODEL, D_MODEL))],
        out_specs=row(D_MODEL),
        out_shape=jax.ShapeDtypeStruct((T, D_MODEL), F32),
        compiler_params=_cparams(("parallel",)),
    )(y_dil, y_nsa, y_gla, y_s5, proj, x2, s5_d.reshape(1, -1), glu_w.astype(BF16),
      glu_b.reshape(1, -1), out_g.reshape(4, GROUP_WIDTH), w_out.astype(BF16))


def _swiglu_chunks(h, wg_ref, wu_ref, wd_ref, acc, width, chunk, lead=()):
    for c in range(width // chunk):
        cs = slice(c * chunk, (c + 1) * chunk)
        a = _dot(h, wg_ref[lead + (slice(None), cs)])
        u = _dot(h, wu_ref[lead + (slice(None), cs)])
        acc = acc + _dot((a * _sigmoid(a) * u).astype(BF16), wd_ref[lead + (cs, slice(None))])
    return acc


def _ffn_kernel(x_ref, g_ref, wg_ref, wu_ref, wd_ref, o_ref):
    x = x_ref[...]
    h = (_rms(x) * g_ref[...]).astype(BF16)
    o_ref[...] = _swiglu_chunks(h, wg_ref, wu_ref, wd_ref, x, D_FF_PAD, D_FF_PAD // 2)


def _resident(shape, index_map):
    return pl.BlockSpec(shape, index_map, pipeline_mode=pl.Buffered(1))


def _ffn(x2, g, wg, wu, wd):
    T = x2.shape[0]
    tm = 512
    pad = D_FF_PAD - D_FF
    wg_p = jnp.pad(wg, ((0, 0), (0, pad))).astype(BF16)
    wu_p = jnp.pad(wu, ((0, 0), (0, pad))).astype(BF16)
    wd_p = jnp.pad(wd, ((0, pad), (0, 0))).astype(BF16)
    return pl.pallas_call(
        _ffn_kernel,
        grid=(T // tm,),
        in_specs=[pl.BlockSpec((tm, D_MODEL), lambda i: (i, 0)),
                  pl.BlockSpec((1, D_MODEL), lambda i: (0, 0)),
                  _resident((D_MODEL, D_FF_PAD), lambda i: (0, 0)),
                  _resident((D_MODEL, D_FF_PAD), lambda i: (0, 0)),
                  _resident((D_FF_PAD, D_MODEL), lambda i: (0, 0))],
        out_specs=pl.BlockSpec((tm, D_MODEL), lambda i: (i, 0)),
        out_shape=jax.ShapeDtypeStruct((T, D_MODEL), F32),
        compiler_params=_cparams(("parallel",)),
    )(x2, g.reshape(1, D_MODEL), wg_p, wu_p, wd_p)


MOE_TILE = 512
CMB_TILE = 128


def _router_kernel(x_ref, g_ref, w_ref, b_ref, h_ref, r_ref):
    h = _rms(x_ref[...]) * g_ref[...]
    h_ref[...] = h
    lane = _iota((h.shape[0], LANE), 1)
    logits = jnp.where(lane < N_EXPERTS, _dot3(h, w_ref[...]) + b_ref[...], -jnp.inf)
    m1 = jnp.max(logits, axis=-1, keepdims=True)
    i1 = jnp.min(jnp.where(logits == m1, lane, LANE), axis=-1, keepdims=True)
    rest = jnp.where(lane == i1, -jnp.inf, logits)
    m2 = jnp.max(rest, axis=-1, keepdims=True)
    i2 = jnp.min(jnp.where(rest == m2, lane, LANE), axis=-1, keepdims=True)
    e2 = jnp.exp(m2 - m1)
    w1 = 1.0 / (1.0 + e2)
    w2 = e2 / (1.0 + e2)
    r_ref[...] = jnp.where(lane == 0, i1.astype(F32),
                           jnp.where(lane == 1, i2.astype(F32),
                                     jnp.where(lane == 2, w1, jnp.where(lane == 3, w2, 0.0))))


def _router(x2, g, rw, rb):
    T = x2.shape[0]
    tm = 512
    rw_p = jnp.pad(rw, ((0, 0), (0, LANE - N_EXPERTS)))
    rb_p = jnp.pad(rb, (0, LANE - N_EXPERTS)).reshape(1, LANE)
    return pl.pallas_call(
        _router_kernel,
        grid=(T // tm,),
        in_specs=[pl.BlockSpec((tm, D_MODEL), lambda i: (i, 0)),
                  pl.BlockSpec((1, D_MODEL), lambda i: (0, 0)),
                  pl.BlockSpec((D_MODEL, LANE), lambda i: (0, 0)),
                  pl.BlockSpec((1, LANE), lambda i: (0, 0))],
        out_specs=[pl.BlockSpec((tm, D_MODEL), lambda i: (i, 0)),
                   pl.BlockSpec((tm, LANE), lambda i: (i, 0))],
        out_shape=[jax.ShapeDtypeStruct((T, D_MODEL), F32), jax.ShapeDtypeStruct((T, LANE), F32)],
        compiler_params=_cparams(("parallel",)),
    )(x2, g.reshape(1, D_MODEL), rw_p, rb_p)


def _row_copy(src_hbm, row, dst, slot, r, sem):
    return pltpu.make_async_copy(src_hbm.at[pl.ds(row, 1), :], dst.at[slot, pl.ds(r, 1), :], sem.at[slot, r])


def _gather_rows(idx_ref, src_hbm, dst, slot, sem, n):
    def body(r, _):
        _row_copy(src_hbm, idx_ref[0, 0, r], dst, slot, r, sem).start()
        return 0
    lax.fori_loop(0, n, body, 0)


def _wait_rows(idx_ref, src_hbm, dst, slot, sem, n):
    def body(r, _):
        _row_copy(src_hbm, idx_ref[0, 0, r], dst, slot, r, sem).wait()
        return 0
    lax.fori_loop(0, n, body, 0)


def _moe_kernel(texp_ref, cur_ref, nxt_ref, h_hbm, wg_ref, wu_ref, wd_ref, y_ref, xbuf, sem):
    i = pl.program_id(0)
    nt = pl.num_programs(0)
    slot = i % 2

    @pl.when(i == 0)
    def _():
        _gather_rows(cur_ref, h_hbm, xbuf, 0, sem, MOE_TILE)

    @pl.when(i + 1 < nt)
    def _():
        _gather_rows(nxt_ref, h_hbm, xbuf, 1 - slot, sem, MOE_TILE)

    _wait_rows(cur_ref, h_hbm, xbuf, slot, sem, MOE_TILE)
    h = xbuf[slot].astype(BF16)
    y_ref[...] = _swiglu_chunks(h, wg_ref, wu_ref, wd_ref, jnp.zeros((MOE_TILE, D_MODEL), F32),
                                D_FF_EXPERT, D_FF_EXPERT // 4, lead=(0,))


def _moe_experts(h, src_rows, tile_expert, wg, wu, wd):
    nt = tile_expert.shape[0]
    src3 = src_rows.reshape(nt, 1, MOE_TILE)
    grid_spec = pltpu.PrefetchScalarGridSpec(
        num_scalar_prefetch=1,
        grid=(nt,),
        in_specs=[
            pl.BlockSpec((1, 1, MOE_TILE), lambda i, te: (i, 0, 0), memory_space=pltpu.SMEM),
            pl.BlockSpec((1, 1, MOE_TILE), lambda i, te: (jnp.minimum(i + 1, nt - 1), 0, 0),
                         memory_space=pltpu.SMEM),
            pl.BlockSpec(memory_space=pl.ANY),
            _resident((1, D_MODEL, D_FF_EXPERT), lambda i, te: (te[i], 0, 0)),
            _resident((1, D_MODEL, D_FF_EXPERT), lambda i, te: (te[i], 0, 0)),
            _resident((1, D_FF_EXPERT, D_MODEL), lambda i, te: (te[i], 0, 0)),
        ],
        out_specs=pl.BlockSpec((MOE_TILE, D_MODEL), lambda i, te: (i, 0)),
        scratch_shapes=[pltpu.VMEM((2, MOE_TILE, D_MODEL), F32),
                        pltpu.SemaphoreType.DMA((2, MOE_TILE))],
    )
    return pl.pallas_call(
        _moe_kernel,
        grid_spec=grid_spec,
        out_shape=jax.ShapeDtypeStruct((nt * MOE_TILE, D_MODEL), F32),
        compiler_params=_cparams(("arbitrary",)),
    )(tile_expert, src3, src3, h, wg.astype(BF16), wu.astype(BF16), wd.astype(BF16))


def _combine_kernel(p1_ref, p2_ref, p1n_ref, p2n_ref, y_hbm, x_ref, r_ref, o_ref, ybuf, sem):
    i = pl.program_id(0)
    nt = pl.num_programs(0)
    slot = i % 2

    def issue(pa, pb, s):
        _gather_rows(pa, y_hbm, ybuf.at[0], s, sem.at[0], CMB_TILE)
        _gather_rows(pb, y_hbm, ybuf.at[1], s, sem.at[1], CMB_TILE)

    @pl.when(i == 0)
    def _():
        issue(p1_ref, p2_ref, 0)

    @pl.when(i + 1 < nt)
    def _():
        issue(p1n_ref, p2n_ref, 1 - slot)

    _wait_rows(p1_ref, y_hbm, ybuf.at[0], slot, sem.at[0], CMB_TILE)
    _wait_rows(p2_ref, y_hbm, ybuf.at[1], slot, sem.at[1], CMB_TILE)
    r = r_ref[...]
    o_ref[...] = x_ref[...] + r[:, 2:3] * ybuf[0, slot] + r[:, 3:4] * ybuf[1, slot]


def _moe_combine(y_sorted, pos, x2, route):
    T = x2.shape[0]
    nt = T // CMB_TILE
    p1 = pos[:, 0].reshape(nt, 1, CMB_TILE)
    p2 = pos[:, 1].reshape(nt, 1, CMB_TILE)

    def smem(nxt):
        if nxt:
            return pl.BlockSpec((1, 1, CMB_TILE), lambda i: (jnp.minimum(i + 1, nt - 1), 0, 0),
                                memory_space=pltpu.SMEM)
        return pl.BlockSpec((1, 1, CMB_TILE), lambda i: (i, 0, 0), memory_space=pltpu.SMEM)

    return pl.pallas_call(
        _combine_kernel,
        grid=(nt,),
        in_specs=[smem(False), smem(False), smem(True), smem(True),
                  pl.BlockSpec(memory_space=pl.ANY),
                  pl.BlockSpec((CMB_TILE, D_MODEL), lambda i: (i, 0)),
                  pl.BlockSpec((CMB_TILE, LANE), lambda i: (i, 0))],
        out_specs=pl.BlockSpec((CMB_TILE, D_MODEL), lambda i: (i, 0)),
        out_shape=jax.ShapeDtypeStruct((T, D_MODEL), F32),
        scratch_shapes=[pltpu.VMEM((2, 2, CMB_TILE, D_MODEL), F32),
                        pltpu.SemaphoreType.DMA((2, 2, CMB_TILE))],
        compiler_params=_cparams(("arbitrary",)),
    )(p1, p2, p1, p2, y_sorted, x2, route)


def _moe(x2, g, rw, rb, wg, wu, wd):
    T = x2.shape[0]
    h, route = _router(x2, g, rw, rb)
    experts = route[:, :2].astype(jnp.int32)
    onehot = (experts.reshape(-1)[:, None] == jnp.arange(N_EXPERTS)[None, :]).astype(jnp.int32)
    csum = jnp.cumsum(onehot, axis=0)
    counts = csum[-1]
    rank = jnp.take_along_axis(csum, experts.reshape(-1, 1), axis=1)[:, 0] - 1
    padded = ((counts + MOE_TILE - 1) // MOE_TILE) * MOE_TILE
    starts = jnp.cumsum(padded) - padded
    pos = (starts[experts.reshape(-1)] + rank).reshape(T, 2)
    nt = (2 * T) // MOE_TILE + N_EXPERTS
    src_rows = jnp.zeros((nt * MOE_TILE,), jnp.int32).at[pos.reshape(-1)].set(
        jnp.repeat(jnp.arange(T, dtype=jnp.int32), 2))
    tile_start = jnp.arange(nt, dtype=jnp.int32) * MOE_TILE
    ends = jnp.cumsum(padded)
    tile_expert = jnp.minimum(jnp.sum(tile_start[:, None] >= ends[None, :], axis=1), N_EXPERTS - 1)
    y_sorted = _moe_experts(h, src_rows, tile_expert.astype(jnp.int32), wg, wu, wd)
    return _moe_combine(y_sorted, pos, x2, route)


def _layer_mixers(x2, B, S, l, p):
    proj = _inproj(x2, p["norm1_g"][l], _pad_w_in(p["w_in"][l]), _head_gains(p["dil_qk_g"][l], p["nsa_qk_g"][l]))
    proj3 = proj.reshape(B, S, PROJ_PAD)
    y_dil = _dil_attention(proj3).reshape(B * S, GROUP_WIDTH)
    kc, vc, ks, vs, kw, vw = _nsa_prep(proj3, p["nsa_cmp_pos"][l], p["nsa_cmp_w1"][l], p["nsa_cmp_w2"][l],
                                       p["nsa_qk_g"][l])
    y_nsa = _nsa_attention(proj3, kc, vc, ks, vs, kw, vw).reshape(B * S, GROUP_WIDTH)
    y_gla = _gla(proj3, p["gla_wa2"][l], p["gla_ba"][l], p["gla_norm_g"][l]).reshape(B * S, GROUP_WIDTH)
    y_s5 = _s5_linear(proj3[:, :, COL_SU:COL_SU + GROUP_WIDTH], p["s5_a_re"][l], p["s5_a_im"][l],
                      p["s5_b_re"][l], p["s5_b_im"][l], p["s5_c_re"][l], p["s5_c_im"][l],
                      p["s5_log_dt"][l]).reshape(B * S, GROUP_WIDTH)
    return _mixout(y_dil, y_nsa, y_gla, y_s5, proj, x2, p["s5_d"][l], p["s5_glu_w"][l],
                   p["s5_glu_b"][l], p["out_norm_g"][l], p["w_out"][l])


def kernel(x, norm1_g, w_in, dil_qk_g, nsa_qk_g, nsa_cmp_pos, nsa_cmp_w1, nsa_cmp_w2, gla_wa2, gla_ba,
           gla_norm_g, s5_a_re, s5_a_im, s5_b_re, s5_b_im, s5_c_re, s5_c_im, s5_d, s5_log_dt, s5_glu_w,
           s5_glu_b, out_norm_g, w_out, norm2_g, ffn_w_gate, ffn_w_up, ffn_w_down, moe_router_w,
           moe_router_b, moe_w_gate, moe_w_up, moe_w_down):
    B, S, D = x.shape
    p = dict(norm1_g=norm1_g, w_in=w_in, dil_qk_g=dil_qk_g, nsa_qk_g=nsa_qk_g, nsa_cmp_pos=nsa_cmp_pos,
             nsa_cmp_w1=nsa_cmp_w1, nsa_cmp_w2=nsa_cmp_w2, gla_wa2=gla_wa2, gla_ba=gla_ba,
             gla_norm_g=gla_norm_g, s5_a_re=s5_a_re, s5_a_im=s5_a_im, s5_b_re=s5_b_re, s5_b_im=s5_b_im,
             s5_c_re=s5_c_re, s5_c_im=s5_c_im, s5_d=s5_d, s5_log_dt=s5_log_dt, s5_glu_w=s5_glu_w,
             s5_glu_b=s5_glu_b, out_norm_g=out_norm_g, w_out=w_out)
    x2 = x.reshape(B * S, D)
    depth = norm1_g.shape[0]
    for l in range(depth):
        x2 = _layer_mixers(x2, B, S, l, p)
        i = l // 2
        if l % 2 == 0:
            x2 = _ffn(x2, norm2_g[l], ffn_w_gate[i], ffn_w_up[i], ffn_w_down[i])
        else:
            x2 = _moe(x2, norm2_g[l], moe_router_w[i], moe_router_b[i], moe_w_gate[i], moe_w_up[i],
                      moe_w_down[i])
    return x2.reshape(B, S, D)
```

```python
import functools
import math

import jax
import jax.numpy as jnp
from jax import lax
from jax.experimental import pallas as pl
from jax.experimental.pallas import tpu as pltpu

F32 = jnp.float32
BF16 = jnp.bfloat16

D_MODEL = 1024
HEAD_DIM = 64
GROUP_WIDTH = 256
N_HEADS = 4
DIL_PATTERNS = ((128, 1), (512, 4), (2048, 16))
Q_BLOCK = 128
NSA_CMP_LEN = 32
NSA_CMP_STRIDE = 16
NSA_CMP_HIDDEN = 256
NSA_SEL_LEN = 64
NSA_TOPN = 16
NSA_WIN = 512
FORCE_BONUS = 1.0e4
GLA_DK = 32
GLA_DV = 64
GLA_RANK = 16
GLA_TAU = 16.0
GLA_CHUNK = 16
S5_GROUP = 16
S5_GROUPS = 16
S5_STATE = 64
S5_CHUNK = 16
D_FF = 2752
N_EXPERTS = 8
D_FF_EXPERT = 3584
EPS = 1e-6
NEG_INF = -1e30

DIL_SLOPES = tuple(2.0 ** (-float(i)) for i in (2, 4, 6, 8))
NSA_SLOPES = tuple(2.0 ** (-float(i)) for i in (1, 3, 5, 7))

COL_DQ, COL_DK, COL_DV, COL_NQ = 0, 256, 512, 768
COL_NCMP, COL_NK, COL_NV, COL_NG = 1024, 1152, 1280, 1408
COL_GQ, COL_GK, COL_GV, COL_GA, COL_GR, COL_SU = 1536, 1664, 1792, 2048, 2304, 2560
PROJ_PAD = 2816
LANE = 128
D_FF_PAD = 2816

VMEM_LIMIT = 56 * 1024 * 1024


def _cparams(sem):
    return pltpu.CompilerParams(dimension_semantics=sem, vmem_limit_bytes=VMEM_LIMIT)


def _rms(x):
    return x * lax.rsqrt(jnp.mean(x * x, axis=-1, keepdims=True) + EPS)


def _dot(a, b):
    return jnp.dot(a, b, preferred_element_type=F32)


def _dot_nt(a, b):
    return lax.dot_general(a, b, (((1,), (1,)), ((), ())), preferred_element_type=F32)


def _split(x):
    hi = x.astype(BF16)
    lo = (x - hi.astype(F32)).astype(BF16)
    return hi, lo


def _dot_lsplit(x, m):
    hi, lo = _split(x)
    return _dot(hi, m) + _dot(lo, m)


def _dot_rsplit(m, x):
    hi, lo = _split(x)
    return _dot(m, hi) + _dot(m, lo)


def _dot3(a, b):
    ah, al = _split(a)
    bh, bl = _split(b)
    return _dot(ah, bh) + _dot(ah, bl) + _dot(al, bh)


def _gelu(x):
    return 0.5 * x * (1.0 + jnp.tanh(math.sqrt(2.0 / math.pi) * (x + 0.044715 * (x * x * x))))


def _sigmoid(x):
    return 1.0 / (1.0 + jnp.exp(-x))


def _iota(shape, dim):
    return lax.broadcasted_iota(jnp.int32, shape, dim)


def _head_indicator(rows, cols, rdiv, cdiv):
    return jnp.where(_iota((rows, cols), 0) // rdiv == _iota((rows, cols), 1) // cdiv, 1.0, 0.0).astype(BF16)


def _inproj_kernel(x_ref, g_ref, w_ref, hg_ref, o_ref):
    h = _rms(x_ref[...]) * g_ref[...]
    o_ref[...] = _dot(h.astype(BF16), w_ref[...])
    ind = _head_indicator(GROUP_WIDTH, GROUP_WIDTH, HEAD_DIM, HEAD_DIM)
    for lo, width in NORM_SEGMENTS:
        v = o_ref[:, lo:lo + width]
        ms = _dot_lsplit(v * v, ind[:width, :width]) * (1.0 / HEAD_DIM)
        o_ref[:, lo:lo + width] = v * lax.rsqrt(ms + EPS) * hg_ref[:, lo:lo + width]


NORM_COLS = COL_NV
NORM_SEGMENTS = ((COL_DQ, GROUP_WIDTH), (COL_DK, GROUP_WIDTH), (COL_NQ, GROUP_WIDTH), (COL_NK, LANE))


def _head_gains(dil_g, nsa_g):
    scale = HEAD_DIM ** -0.5
    z = jnp.zeros((HEAD_DIM,), F32)
    row = jnp.concatenate([jnp.tile(dil_g[0] * scale, N_HEADS), jnp.tile(dil_g[1], N_HEADS),
                           jnp.zeros((GROUP_WIDTH,), F32), jnp.tile(nsa_g[0] * scale, N_HEADS),
                           z, z, nsa_g[2], nsa_g[3]])
    return row.reshape(1, NORM_COLS)


def _inproj(x2, g, w_pad, head_gains):
    T = x2.shape[0]
    tm = 512
    return pl.pallas_call(
        _inproj_kernel,
        grid=(T // tm,),
        in_specs=[
            pl.BlockSpec((tm, D_MODEL), lambda i: (i, 0)),
            pl.BlockSpec((1, D_MODEL), lambda i: (0, 0)),
            pl.BlockSpec((D_MODEL, PROJ_PAD), lambda i: (0, 0)),
            pl.BlockSpec((1, NORM_COLS), lambda i: (0, 0)),
        ],
        out_specs=pl.BlockSpec((tm, PROJ_PAD), lambda i: (i, 0)),
        out_shape=jax.ShapeDtypeStruct((T, PROJ_PAD), F32),
        compiler_params=_cparams(("parallel",)),
    )(x2, g.reshape(1, D_MODEL), w_pad, head_gains)


def _pad_w_in(w):
    def seg(lo, width, pad_to):
        s = w[:, lo:lo + width]
        if pad_to > width:
            s = jnp.pad(s, ((0, 0), (0, pad_to - width)))
        return s
    k_cmp, v_cmp = seg(1024, 64, 64), seg(1088, 64, 64)
    k_slc, v_slc = seg(1152, 64, 64), seg(1216, 64, 64)
    k_win, v_win = seg(1280, 64, 64), seg(1344, 64, 64)
    pieces = [
        seg(0, 256, 256), seg(256, 256, 256), seg(512, 256, 256), seg(768, 256, 256),
        k_cmp, v_cmp, k_slc, k_win, v_slc, v_win,
        seg(1408, 12, 128),
        seg(1420, 128, 128), seg(1548, 128, 128), seg(1676, 256, 256),
        seg(1932, 16, 256),
        seg(1948, 256, 256), seg(2204, 256, 256),
    ]
    out = jnp.concatenate(pieces, axis=1)
    assert out.shape[1] == PROJ_PAD
    return out.astype(BF16)


DIL_SB = Q_BLOCK * max(d for _, d in DIL_PATTERNS)
DIL_SUBSTEPS = DIL_SB // Q_BLOCK


def _dil_kernel(q0_ref, q1_ref, k0_ref, k1_ref, v0_ref, v1_ref, o_ref,
                ks_ref, vs_ref, bias_ref, m_ref, l_ref, acc_ref):
    i = pl.program_id(1)
    rows4 = N_HEADS * Q_BLOCK

    def ld2(ref, rows):
        return jnp.concatenate([ref[0, rows, :], ref[1, rows, :]], axis=1)

    def st2(ref, rows, val):
        ref[0, rows, :] = val[:, :LANE]
        ref[1, rows, :] = val[:, LANE:]

    @pl.when(i == 0)
    def _():
        ks_ref[:, 0:DIL_SB, :] = jnp.zeros((2, DIL_SB, LANE), F32)
        vs_ref[:, 0:DIL_SB, :] = jnp.zeros((2, DIL_SB, LANE), F32)
        ri = _iota((rows4, 2 * Q_BLOCK), 0)
        ci = _iota((rows4, 2 * Q_BLOCK), 1)
        delta = Q_BLOCK + ri % Q_BLOCK - ci
        head = ri // Q_BLOCK
        slope = jnp.where(head == 0, DIL_SLOPES[0],
                          jnp.where(head == 1, DIL_SLOPES[1],
                                    jnp.where(head == 2, DIL_SLOPES[2], DIL_SLOPES[3]))).astype(F32)
        band = (delta >= 0) & (delta <= Q_BLOCK)
        for p, (_, dil) in enumerate(DIL_PATTERNS):
            b = jnp.where(band, -(slope * dil) * delta.astype(F32), NEG_INF)
            bias_ref[p, 0] = b
            bias_ref[p, 1] = jnp.where(ci >= Q_BLOCK, b, NEG_INF)

    @pl.when(i > 0)
    def _():
        ks_ref[:, 0:DIL_SB, :] = ks_ref[:, DIL_SB:, :]
        vs_ref[:, 0:DIL_SB, :] = vs_ref[:, DIL_SB:, :]

    ks_ref[0, DIL_SB:, :] = k0_ref[0]
    ks_ref[1, DIL_SB:, :] = k1_ref[0]
    vs_ref[0, DIL_SB:, :] = v0_ref[0]
    vs_ref[1, DIL_SB:, :] = v1_ref[0]
    lane_head = _iota((Q_BLOCK, GROUP_WIDTH), 1) // HEAD_DIM

    def spread(x):
        out = jnp.zeros((Q_BLOCK, GROUP_WIDTH), F32)
        for h in range(N_HEADS):
            out = jnp.where(lane_head == h, x[h * Q_BLOCK:(h + 1) * Q_BLOCK], out)
        return out

    for p, (_, dil) in enumerate(DIL_PATTERNS):
        def sub(j, carry, p=p, dil=dil):
            start = (j // dil) * (Q_BLOCK * dil) + j % dil
            rows = pl.ds(start, Q_BLOCK, stride=dil)
            cur = pl.ds(DIL_SB + start, Q_BLOCK, stride=dil)
            prev = pl.ds(DIL_SB + start - Q_BLOCK * dil, Q_BLOCK, stride=dil)
            q = jnp.concatenate([q0_ref[0, rows, :], q1_ref[0, rows, :]], axis=1)
            q4 = jnp.concatenate([jnp.where(lane_head == h, q, 0.0) for h in range(N_HEADS)], axis=0)
            kk = jnp.concatenate([ld2(ks_ref, prev), ld2(ks_ref, cur)], axis=0).astype(BF16)
            vv = jnp.concatenate([ld2(vs_ref, prev), ld2(vs_ref, cur)], axis=0).astype(BF16)
            first = jnp.where((i == 0) & (j < dil), 1, 0)
            s = _dot_nt(q4.astype(BF16), kk) + bias_ref[p, first]
            m = jnp.max(s, axis=-1, keepdims=True)
            e = jnp.exp(s - m)
            l = jnp.sum(e, axis=-1, keepdims=True)
            o_t = spread(_dot(e.astype(BF16), vv))
            m_t = spread(m)
            l_t = spread(l)
            if p == 0:
                st2(m_ref, rows, m_t)
                st2(l_ref, rows, l_t)
                st2(acc_ref, rows, o_t)
            else:
                m_old = ld2(m_ref, rows)
                m_new = jnp.maximum(m_old, m_t)
                a = jnp.exp(m_old - m_new)
                b = jnp.exp(m_t - m_new)
                st2(m_ref, rows, m_new)
                st2(l_ref, rows, a * ld2(l_ref, rows) + b * l_t)
                st2(acc_ref, rows, a * ld2(acc_ref, rows) + b * o_t)
            return carry
        lax.fori_loop(0, DIL_SUBSTEPS, sub, 0)
    o_ref[0] = jnp.concatenate([acc_ref[0] / l_ref[0], acc_ref[1] / l_ref[1]], axis=1)


def _dil_attention(proj3):
    B, S, _ = proj3.shape
    assert all(win // d == Q_BLOCK for win, d in DIL_PATTERNS) and S % DIL_SB == 0

    def halves(c):
        return [pl.BlockSpec((1, DIL_SB, LANE), lambda b, i, c=c, h=h: (b, i, c // LANE + h)) for h in range(2)]

    sb = pltpu.VMEM((2, DIL_SB, LANE), F32)
    sb2 = pltpu.VMEM((2, 2 * DIL_SB, LANE), F32)
    return pl.pallas_call(
        _dil_kernel,
        grid=(B, S // DIL_SB),
        in_specs=halves(COL_DQ) + halves(COL_DK) + halves(COL_DV),
        out_specs=pl.BlockSpec((1, DIL_SB, GROUP_WIDTH), lambda b, i: (b, i, 0)),
        out_shape=jax.ShapeDtypeStruct((B, S, GROUP_WIDTH), F32),
        scratch_shapes=[sb2, sb2, pltpu.VMEM((len(DIL_PATTERNS), 2, N_HEADS * Q_BLOCK, 2 * Q_BLOCK), F32),
                        sb, sb, sb],
        compiler_params=_cparams(("parallel", "arbitrary")),
    )(*([proj3] * 6))


def _nsa_prep_kernel(xk_ref, xv_ref, kraw_ref, vraw_ref, pos_ref, w1_ref, w2_ref, g_ref,
                     kc_ref, vc_ref, ks_ref, vs_ref, kw_ref, vw_ref):
    half = NSA_CMP_STRIDE * HEAD_DIM

    def compress(x, idx):
        w1 = w1_ref[idx]
        xb = x.astype(BF16)
        a = _dot(xb, w1[:half, :])
        b = _dot(xb, w1[half:, :])
        bias = _dot(pos_ref[idx].astype(BF16), w1)[0:1, :]
        pre = a + pltpu.roll(b, b.shape[0] - 1, 0) + bias
        return _dot(_gelu(pre).astype(BF16), w2_ref[idx])

    def with_pos(k, pos):
        lane = _iota((k.shape[0], HEAD_DIM), 1)
        extra = jnp.where(lane == 0, (pos // 256).astype(F32), jnp.where(lane == 1, (pos % 256).astype(F32), 0.0))
        return jnp.concatenate([k, extra], axis=1).astype(BF16)

    def with_one_t(v):
        lane = _iota((v.shape[0], HEAD_DIM), 1)
        return jnp.concatenate([v, jnp.where(lane == 0, 1.0, 0.0)], axis=1).T.astype(BF16)

    ng = xk_ref.shape[1]
    cend = _iota((ng, 1), 0) * NSA_CMP_STRIDE + (NSA_CMP_LEN - 1)
    kc_ref[0] = with_pos(_rms(compress(xk_ref[0], 0)) * g_ref[1:2, :], cend)
    vc_ref[0] = with_one_t(compress(xv_ref[0], 1))
    kraw = kraw_ref[0]
    tpos = _iota((kraw.shape[0], 1), 0)
    ks_ref[0] = with_pos(kraw[:, :HEAD_DIM], tpos)
    kw_ref[0] = with_pos(kraw[:, HEAD_DIM:], tpos)
    for c in range(vs_ref.shape[1]):
        vs_ref[0, c] = with_one_t(vraw_ref[0, c * NSA_KCHUNK:(c + 1) * NSA_KCHUNK, :HEAD_DIM])
    for c in range(vw_ref.shape[1]):
        vw_ref[0, c] = with_one_t(vraw_ref[0, c * Q_BLOCK:(c + 1) * Q_BLOCK, HEAD_DIM:])


NSA_KCHUNK = 2 * Q_BLOCK


def _nsa_prep(proj3, pos, w1, w2, g):
    B, S, _ = proj3.shape
    ng = S // NSA_CMP_STRIDE
    gw = NSA_CMP_STRIDE * HEAD_DIM
    xk = proj3[:, :, COL_NCMP:COL_NCMP + HEAD_DIM].reshape(B, ng, gw)
    xv = proj3[:, :, COL_NCMP + HEAD_DIM:COL_NCMP + 2 * HEAD_DIM].reshape(B, ng, gw)
    pos8 = jnp.broadcast_to(pos.reshape(2, 1, NSA_CMP_LEN * HEAD_DIM), (2, 8, NSA_CMP_LEN * HEAD_DIM))
    small = jax.ShapeDtypeStruct((B, ng, LANE), BF16)
    small_t = jax.ShapeDtypeStruct((B, LANE, ng), BF16)
    big = jax.ShapeDtypeStruct((B, S, LANE), BF16)
    vs_t = jax.ShapeDtypeStruct((B, S // NSA_KCHUNK, LANE, NSA_KCHUNK), BF16)
    vw_t = jax.ShapeDtypeStruct((B, S // Q_BLOCK, LANE, Q_BLOCK), BF16)
    small_spec = pl.BlockSpec((1, ng, LANE), lambda b: (b, 0, 0))
    small_t_spec = pl.BlockSpec((1, LANE, ng), lambda b: (b, 0, 0))
    big_spec = pl.BlockSpec((1, S, LANE), lambda b: (b, 0, 0))
    vs_spec = pl.BlockSpec((1, S // NSA_KCHUNK, LANE, NSA_KCHUNK), lambda b: (b, 0, 0, 0))
    vw_spec = pl.BlockSpec((1, S // Q_BLOCK, LANE, Q_BLOCK), lambda b: (b, 0, 0, 0))
    return pl.pallas_call(
        _nsa_prep_kernel,
        grid=(B,),
        in_specs=[
            pl.BlockSpec((1, ng, gw), lambda b: (b, 0, 0)),
            pl.BlockSpec((1, ng, gw), lambda b: (b, 0, 0)),
            pl.BlockSpec((1, S, LANE), lambda b: (b, 0, COL_NK // LANE)),
            pl.BlockSpec((1, S, LANE), lambda b: (b, 0, COL_NV // LANE)),
            pl.BlockSpec((2, 8, NSA_CMP_LEN * HEAD_DIM), lambda b: (0, 0, 0)),
            pl.BlockSpec((2, NSA_CMP_LEN * HEAD_DIM, NSA_CMP_HIDDEN), lambda b: (0, 0, 0)),
            pl.BlockSpec((2, NSA_CMP_HIDDEN, HEAD_DIM), lambda b: (0, 0, 0)),
            pl.BlockSpec((4, HEAD_DIM), lambda b: (0, 0)),
        ],
        out_specs=[small_spec, small_t_spec, big_spec, vs_spec, big_spec, vw_spec],
        out_shape=[small, small_t, big, vs_t, big, vw_t],
        compiler_params=_cparams(("parallel",)),
    )(xk, xv, proj3, proj3, pos8, w1.astype(BF16), w2.astype(BF16), g)


def _nsa_kernel(q_ref, gate_ref, kc_ref, vc_ref, ks_ref, vs_ref, kw_ref, vw_ref, o_ref, sel_ref, *, seq):
    i = pl.program_id(1)
    t0 = i * Q_BLOCK
    q_t = q_ref[0].T
    sub = _iota((HEAD_DIM, Q_BLOCK), 0)
    q = jnp.concatenate(
        [jnp.concatenate([q_t[h * HEAD_DIM:(h + 1) * HEAD_DIM],
                          jnp.where(sub == 0, NSA_SLOPES[h] * 256.0, jnp.where(sub == 1, NSA_SLOPES[h], 0.0))],
                         axis=0) for h in range(N_HEADS)], axis=1).astype(BF16)
    tq = t0 + _iota((1, Q_BLOCK), 1)

    def masked(s, ok):
        return jnp.concatenate([jnp.where(ok, s[:, h * Q_BLOCK:(h + 1) * Q_BLOCK], NEG_INF)
                                for h in range(N_HEADS)], axis=1)

    def heads(x):
        return jnp.concatenate([x] * N_HEADS, axis=1)

    ncmp = kc_ref.shape[1]
    cend = _iota((ncmp, 1), 0) * NSA_CMP_STRIDE + (NSA_CMP_LEN - 1)
    s = masked(_dot(kc_ref[0], q), cend <= tq)
    e = jnp.exp(s - jnp.max(s, axis=0, keepdims=True))
    col_ok = heads(jnp.where(tq >= NSA_CMP_LEN - 1, 1.0, 0.0))
    p = e * (col_ok / jnp.sum(e, axis=0, keepdims=True))
    o_cmp = _dot(vc_ref[0], p.astype(BF16))[:HEAD_DIM]

    nsel = seq // NSA_SEL_LEN
    psum = p[:, :Q_BLOCK] + p[:, Q_BLOCK:2 * Q_BLOCK] + p[:, 2 * Q_BLOCK:3 * Q_BLOCK] + p[:, 3 * Q_BLOCK:]
    jj = _iota((nsel, ncmp), 0)
    cc = _iota((nsel, ncmp), 1)
    ratio = NSA_SEL_LEN // NSA_CMP_STRIDE
    cover_t = jnp.where((cc < ratio * jj + ratio) & (cc * NSA_CMP_STRIDE + NSA_CMP_LEN > jj * NSA_SEL_LEN),
                        1.0, 0.0).astype(BF16)
    imp = _dot_rsplit(cover_t, psum)
    jrow = _iota((nsel, Q_BLOCK), 0)
    cur = (t0 + _iota((nsel, Q_BLOCK), 1)) // NSA_SEL_LEN
    forced = (jrow == 0) | (jrow == cur) | (jrow == cur - 1)
    impv = jnp.where(jrow <= cur, imp + jnp.where(forced, FORCE_BONUS, 0.0), NEG_INF)
    rank = jnp.zeros((nsel, Q_BLOCK), F32)
    for ii in range(nsel):
        row = impv[ii:ii + 1, :]
        tie = jnp.where(jrow > ii, 1.0, 0.0)
        rank = rank + jnp.where(row > impv, 1.0, jnp.where(row == impv, tie, 0.0))
    sel_ref[...] = jnp.where((rank < min(NSA_TOPN, nsel)) & (jrow <= cur), 1.0, 0.0)

    kpos_col = _iota((NSA_KCHUNK, 1), 0)
    per_chunk = NSA_KCHUNK // NSA_SEL_LEN

    def chunk_scores(c):
        k0 = pl.multiple_of(c * NSA_KCHUNK, NSA_KCHUNK)
        picked = jnp.concatenate(
            [jnp.broadcast_to(sel_ref[pl.ds(c * per_chunk + b, 1), :], (NSA_SEL_LEN, Q_BLOCK))
             for b in range(per_chunk)], axis=0)
        ok = (picked > 0.5) & (k0 + kpos_col <= tq)
        return masked(_dot(ks_ref[0, pl.ds(k0, NSA_KCHUNK), :], q), ok)

    def sel_body(it, carry):
        m_i, acc = carry
        sa = chunk_scores(2 * it)
        sb = chunk_scores(2 * it + 1)
        m_new = jnp.maximum(m_i, jnp.maximum(jnp.max(sa, axis=0, keepdims=True), jnp.max(sb, axis=0, keepdims=True)))
        pa = jnp.exp(sa - m_new).astype(BF16)
        pb = jnp.exp(sb - m_new).astype(BF16)
        return m_new, jnp.exp(m_i - m_new) * acc + _dot(vs_ref[0, 2 * it], pa) + _dot(vs_ref[0, 2 * it + 1], pb)

    cols = N_HEADS * Q_BLOCK
    init = (jnp.full((1, cols), NEG_INF, F32), jnp.zeros((LANE, cols), F32))
    _, acc_s = lax.fori_loop(0, (t0 + Q_BLOCK - 1) // (2 * NSA_KCHUNK) + 1, sel_body, init)
    o_sel = acc_s[:HEAD_DIM] / acc_s[HEAD_DIM:HEAD_DIM + 1]

    nwb = NSA_WIN // Q_BLOCK + 1
    span = nwb * Q_BLOCK
    b0 = jnp.maximum(i - NSA_WIN // Q_BLOCK, 0)
    start = pl.multiple_of(b0 * Q_BLOCK, Q_BLOCK)
    dw = tq - (start + _iota((span, 1), 0))
    sw = masked(_dot(kw_ref[0, pl.ds(start, span), :], q), (dw >= 0) & (dw < NSA_WIN))
    pw = jnp.exp(sw - jnp.max(sw, axis=0, keepdims=True)).astype(BF16)
    rw = _dot(vw_ref[0, b0], pw[:Q_BLOCK])
    for j in range(1, nwb):
        rw = rw + _dot(vw_ref[0, b0 + j], pw[j * Q_BLOCK:(j + 1) * Q_BLOCK])
    o_win = rw[:HEAD_DIM] / rw[HEAD_DIM:HEAD_DIM + 1]

    gate = _sigmoid(gate_ref[0].T)
    outs = []
    for h in range(N_HEADS):
        c = slice(h * Q_BLOCK, (h + 1) * Q_BLOCK)
        outs.append(gate[3 * h:3 * h + 1] * o_cmp[:, c] + gate[3 * h + 1:3 * h + 2] * o_sel[:, c]
                    + gate[3 * h + 2:3 * h + 3] * o_win[:, c])
    o_ref[0] = jnp.concatenate(outs, axis=0).T


def _nsa_attention(proj3, kc, vc, ks, vs, kw, vw):
    B, S, _ = proj3.shape
    nb = S // Q_BLOCK
    ng = kc.shape[1]
    assert S >= NSA_WIN + Q_BLOCK and S % (2 * NSA_KCHUNK) == 0
    small_spec = pl.BlockSpec((1, ng, LANE), lambda b, i: (b, 0, 0))
    small_t_spec = pl.BlockSpec((1, LANE, ng), lambda b, i: (b, 0, 0))
    big_spec = pl.BlockSpec((1, S, LANE), lambda b, i: (b, 0, 0))
    vs_spec = pl.BlockSpec((1, S // NSA_KCHUNK, LANE, NSA_KCHUNK), lambda b, i: (b, 0, 0, 0))
    vw_spec = pl.BlockSpec((1, S // Q_BLOCK, LANE, Q_BLOCK), lambda b, i: (b, 0, 0, 0))
    return pl.pallas_call(
        functools.partial(_nsa_kernel, seq=S),
        grid=(B, nb),
        in_specs=[
            pl.BlockSpec((1, Q_BLOCK, GROUP_WIDTH), lambda b, i: (b, i, COL_NQ // GROUP_WIDTH)),
            pl.BlockSpec((1, Q_BLOCK, LANE), lambda b, i: (b, i, COL_NG // LANE)),
            small_spec, small_t_spec, big_spec, vs_spec, big_spec, vw_spec,
        ],
        out_specs=pl.BlockSpec((1, Q_BLOCK, GROUP_WIDTH), lambda b, i: (b, i, 0)),
        out_shape=jax.ShapeDtypeStruct((B, S, GROUP_WIDTH), F32),
        scratch_shapes=[pltpu.VMEM((S // NSA_SEL_LEN, Q_BLOCK), F32)],
        compiler_params=_cparams(("parallel", "parallel")),
    )(proj3, proj3, kc, vc, ks, vs, kw, vw)


GLA_BLOCK = 128
GLA_W = N_HEADS * GLA_DK


def _gla_kernel(q_ref, k_ref, v_ref, a_ref, r_ref, wa_ref, ba_ref, g_ref, o_ref, st_ref):
    nsub = GLA_BLOCK // GLA_CHUNK

    @pl.when(pl.program_id(1) == 0)
    def _():
        st_ref[...] = jnp.zeros_like(st_ref)

    q = q_ref[0] * (GLA_DK ** -0.5)
    k = k_ref[0]
    v = v_ref[0]
    z = _dot3(a_ref[0], wa_ref[...]) + ba_ref[...]
    log_a = (jnp.minimum(z, 0.0) - jnp.log(1.0 + jnp.exp(-jnp.abs(z)))) * (1.0 / GLA_TAU)
    ri = _iota((GLA_BLOCK, GLA_BLOCK), 0)
    ci = _iota((GLA_BLOCK, GLA_BLOCK), 1)
    same = (ri // GLA_CHUNK) == (ci // GLA_CHUNK)
    b_loc = _dot_rsplit(jnp.where(same & (ci <= ri), 1.0, 0.0).astype(BF16), log_a)
    b_tot = _dot_rsplit(jnp.where(same, 1.0, 0.0).astype(BF16), log_a)
    qd = q * jnp.exp(b_loc)
    kd = k * jnp.exp(b_tot - b_loc)

    ind_kv = jnp.where(_iota((GLA_W, GROUP_WIDTH), 0) // GLA_DK == _iota((GLA_W, GROUP_WIDTH), 1) // GLA_DV,
                       1.0, 0.0).astype(BF16)
    rsub = _iota((GLA_BLOCK, 1), 0) % GLA_CHUNK
    o = jnp.zeros((GLA_BLOCK, GROUP_WIDTH), F32)
    for s in range(GLA_CHUNK):
        if s == 0:
            ks_, bs_, vs_ = k, b_loc, v
        else:
            ks_, bs_, vs_ = pltpu.roll(k, s, 0), pltpu.roll(b_loc, s, 0), pltpu.roll(v, s, 0)
        dec = jnp.where(rsub >= s, jnp.exp(jnp.minimum(b_loc - bs_, 0.0)), 0.0)
        w = _dot((q * ks_ * dec).astype(BF16), ind_kv)
        o = o + w * vs_

    v_t = v.T.astype(BF16)
    head_ok = _iota((GROUP_WIDTH, GLA_W), 0) // GLA_DV == _iota((GROUP_WIDTH, GLA_W), 1) // GLA_DK
    rblk = _iota((GLA_BLOCK, 1), 0) // GLA_CHUNK
    st = st_ref[...]
    for c in range(nsub):
        in_c = rblk == c
        o = o + _dot_nt(jnp.where(in_c, qd, 0.0).astype(BF16), st.astype(BF16))
        decay = jnp.exp(b_tot[c * GLA_CHUNK:c * GLA_CHUNK + 1, :])
        upd = _dot(v_t, jnp.where(in_c, kd, 0.0).astype(BF16))
        st = st * decay + jnp.where(head_ok, upd, 0.0)
    st_ref[...] = st

    ind_vv = jnp.where(_iota((GROUP_WIDTH, GROUP_WIDTH), 0) // GLA_DV == _iota((GROUP_WIDTH, GROUP_WIDTH), 1) // GLA_DV,
                       1.0, 0.0).astype(BF16)
    ms = _dot_lsplit(o * o, ind_vv) * (1.0 / GLA_DV)
    r = r_ref[0]
    o_ref[0] = o * lax.rsqrt(ms + EPS) * g_ref[...] * (r * _sigmoid(r))


def _gla(proj3, wa2, ba, norm_g):
    B, S, _ = proj3.shape
    nb = S // GLA_BLOCK
    wa_pad = jnp.zeros((LANE, GLA_W), F32).at[:GLA_RANK].set(wa2)
    g_t = jnp.tile(norm_g.reshape(1, GLA_DV), (1, N_HEADS))

    def col(c, w):
        return pl.BlockSpec((1, GLA_BLOCK, w), lambda b, j: (b, j, c // w))

    return pl.pallas_call(
        _gla_kernel,
        grid=(B, nb),
        in_specs=[col(COL_GQ, LANE), col(COL_GK, LANE), col(COL_GV, GROUP_WIDTH), col(COL_GA, LANE),
                  col(COL_GR, GROUP_WIDTH),
                  pl.BlockSpec((LANE, GLA_W), lambda b, j: (0, 0)),
                  pl.BlockSpec((1, GLA_W), lambda b, j: (0, 0)),
                  pl.BlockSpec((1, GROUP_WIDTH), lambda b, j: (0, 0))],
        out_specs=pl.BlockSpec((1, GLA_BLOCK, GROUP_WIDTH), lambda b, j: (b, j, 0)),
        out_shape=jax.ShapeDtypeStruct((B, S, GROUP_WIDTH), F32),
        scratch_shapes=[pltpu.VMEM((GROUP_WIDTH, GLA_W), F32)],
        compiler_params=_cparams(("parallel", "arbitrary")),
    )(proj3, proj3, proj3, proj3, proj3, wa_pad, ba.reshape(1, GLA_W), g_t)


S5_CW = S5_CHUNK * S5_GROUP
S5_SW = 2 * S5_STATE


def _s5_matrices(a_re, a_im, b_re, b_im, c_re, c_im, log_dt):
    hp = lax.Precision.HIGHEST
    L = S5_CHUNK
    dt = jnp.exp(log_dt)[:, None]
    lam_re, lam_im = dt * a_re, dt * a_im
    tau = jnp.arange(L + 1, dtype=F32)[:, None, None]
    mag = jnp.exp(tau * lam_re)
    p_re, p_im = mag * jnp.cos(tau * lam_im), mag * jnp.sin(tau * lam_im)
    den = a_re * a_re + a_im * a_im
    f_re = ((p_re[1] - 1.0) * a_re + p_im[1] * a_im) / den
    f_im = (p_im[1] * a_re - (p_re[1] - 1.0) * a_im) / den
    bb_re = f_re[..., None] * b_re - f_im[..., None] * b_im
    bb_im = f_re[..., None] * b_im + f_im[..., None] * b_re
    pb_re = p_re[..., None] * bb_re - p_im[..., None] * bb_im
    pb_im = p_re[..., None] * bb_im + p_im[..., None] * bb_re
    kern = (jnp.einsum('gon,tgnc->tgoc', c_re, pb_re, precision=hp)
            - jnp.einsum('gon,tgnc->tgoc', c_im, pb_im, precision=hp))
    s_idx = jnp.arange(L)[:, None]
    t_idx = jnp.arange(L)[None, :]
    lag = jnp.clip(t_idx - s_idx, 0, L)
    toep = jnp.where((t_idx >= s_idx)[..., None, None, None], kern[lag], 0.0)
    toep = toep.transpose(2, 0, 4, 1, 3).reshape(S5_GROUPS, S5_CW, S5_CW)
    rev = (L - 1 - jnp.arange(L))
    bin_re = pb_re[rev].transpose(1, 0, 3, 2).reshape(S5_GROUPS, S5_CW, S5_STATE)
    bin_im = pb_im[rev].transpose(1, 0, 3, 2).reshape(S5_GROUPS, S5_CW, S5_STATE)
    b_in = jnp.concatenate([bin_re, bin_im], axis=-1)
    pt_re, pt_im = p_re[1:], p_im[1:]
    co_re = c_re[None] * pt_re[:, :, None, :] - c_im[None] * pt_im[:, :, None, :]
    co_im = -c_re[None] * pt_im[:, :, None, :] - c_im[None] * pt_re[:, :, None, :]
    c_out = jnp.concatenate([co_re.transpose(1, 3, 0, 2), co_im.transpose(1, 3, 0, 2)], axis=1)
    c_out = c_out.reshape(S5_GROUPS, S5_SW, S5_CW)
    a_chunk = jnp.stack([p_re[L].reshape(1, -1), p_im[L].reshape(1, -1)], axis=0)
    return jnp.concatenate([toep, b_in], axis=-1).astype(BF16), c_out.astype(BF16), a_chunk


def _s5_in_kernel(u_ref, tb_ref, y_ref, v_ref):
    r = _dot(u_ref[0].astype(BF16), tb_ref[0])
    y_ref[0] = r[:, :S5_CW]
    v_ref[0] = r[:, S5_CW:]


def _s5_scan_kernel(v_ref, a_ref, x_ref):
    nb = v_ref.shape[2]
    a_r = jnp.broadcast_to(a_ref[0], (nb, a_ref.shape[2]))
    a_i = jnp.broadcast_to(a_ref[1], (nb, a_ref.shape[2]))

    def body(m, carry):
        x_r, x_i = carry
        x_ref[m, 0] = x_r
        x_ref[m, 1] = x_i
        return a_r * x_r - a_i * x_i + v_ref[m, 0], a_r * x_i + a_i * x_r + v_ref[m, 1]

    zero = jnp.zeros(a_r.shape, F32)
    lax.fori_loop(0, v_ref.shape[0], body, (zero, zero))


def _s5_out_kernel(y_ref, x_ref, c_ref, o_ref):
    o_ref[0] = y_ref[0] + _dot(x_ref[0].astype(BF16), c_ref[0])


def _s5_linear(u3, a_re, a_im, b_re, b_im, c_re, c_im, log_dt):
    B, S, _ = u3.shape
    M = S // S5_CHUNK
    G = S5_GROUPS
    tb, c_out, a_chunk = _s5_matrices(a_re, a_im, b_re, b_im, c_re, c_im, log_dt)
    ug = u3.reshape(B, M, S5_CHUNK, G, S5_GROUP).transpose(3, 0, 1, 2, 4).reshape(G, B * M, S5_CW)
    y_in, v = pl.pallas_call(
        _s5_in_kernel,
        grid=(G,),
        in_specs=[pl.BlockSpec((1, B * M, S5_CW), lambda g: (g, 0, 0)),
                  pl.BlockSpec((1, S5_CW, S5_CW + S5_SW), lambda g: (g, 0, 0))],
        out_specs=[pl.BlockSpec((1, B * M, S5_CW), lambda g: (g, 0, 0)),
                   pl.BlockSpec((1, B * M, S5_SW), lambda g: (g, 0, 0))],
        out_shape=[jax.ShapeDtypeStruct((G, B * M, S5_CW), F32),
                   jax.ShapeDtypeStruct((G, B * M, S5_SW), F32)],
        compiler_params=_cparams(("parallel",)),
    )(ug, tb)
    gn = G * S5_STATE
    v_t = v.reshape(G, B, M, 2, S5_STATE).transpose(2, 3, 1, 0, 4).reshape(M, 2, B, gn)
    lw = 256
    x_t = pl.pallas_call(
        _s5_scan_kernel,
        grid=(gn // lw,),
        in_specs=[pl.BlockSpec((M, 2, B, lw), lambda j: (0, 0, 0, j)),
                  pl.BlockSpec((2, 1, lw), lambda j: (0, 0, j))],
        out_specs=pl.BlockSpec((M, 2, B, lw), lambda j: (0, 0, 0, j)),
        out_shape=jax.ShapeDtypeStruct((M, 2, B, gn), F32),
        compiler_params=_cparams(("parallel",)),
    )(v_t, a_chunk)
    xg = x_t.reshape(M, 2, B, G, S5_STATE).transpose(3, 2, 0, 1, 4).reshape(G, B * M, S5_SW)
    y = pl.pallas_call(
        _s5_out_kernel,
        grid=(G,),
        in_specs=[pl.BlockSpec((1, B * M, S5_CW), lambda g: (g, 0, 0)),
                  pl.BlockSpec((1, B * M, S5_SW), lambda g: (g, 0, 0)),
                  pl.BlockSpec((1, S5_SW, S5_CW), lambda g: (g, 0, 0))],
        out_specs=pl.BlockSpec((1, B * M, S5_CW), lambda g: (g, 0, 0)),
        out_shape=jax.ShapeDtypeStruct((G, B * M, S5_CW), F32),
        compiler_params=_cparams(("parallel",)),
    )(y_in, xg, c_out)
    return y.reshape(G, B, M, S5_CHUNK, S5_GROUP).transpose(1, 2, 3, 0, 4).reshape(B, S, GROUP_WIDTH)


def _mixout_kernel(dil_ref, nsa_ref, gla_ref, s5y_ref, u_ref, x_ref,
                   s5d_ref, gw_ref, gb_ref, ng_ref, wo_ref, o_ref):
    u = u_ref[...]
    hg = _gelu(s5y_ref[...] + s5d_ref[...] * u)
    y_s5 = hg * _sigmoid(_dot(hg.astype(BF16), gw_ref[...]) + gb_ref[...])
    parts = [dil_ref[...], nsa_ref[...], gla_ref[...], y_s5]
    mix = jnp.concatenate([(_rms(parts[n]) * ng_ref[n:n + 1, :]).astype(BF16) for n in range(4)], axis=1)
    o_ref[...] = x_ref[...] + _dot(mix, wo_ref[...])


def _mixout(y_dil, y_nsa, y_gla, y_s5, proj, x2, s5_d, glu_w, glu_b, out_g, w_out):
    T = x2.shape[0]
    tm = 256

    def row(w, c=0):
        return pl.BlockSpec((tm, w), lambda i: (i, c))

    def const(shape):
        return pl.BlockSpec(shape, lambda i: (0,) * len(shape))

    return pl.pallas_call(
        _mixout_kernel,
        grid=(T // tm,),
        in_specs=[row(GROUP_WIDTH), row(GROUP_WIDTH), row(GROUP_WIDTH),
                  row(GROUP_WIDTH), row(GROUP_WIDTH, COL_SU // GROUP_WIDTH), row(D_MODEL),
                  const((1, GROUP_WIDTH)), const((GROUP_WIDTH, GROUP_WIDTH)), const((1, GROUP_WIDTH)),
                  const((4, GROUP_WIDTH)), const((D_MODEL, D_MODEL))],
        out_specs=row(D_MODEL),
        out_shape=jax.ShapeDtypeStruct((T, D_MODEL), F32),
        compiler_params=_cparams(("parallel",)),
    )(y_dil, y_nsa, y_gla, y_s5, proj, x2, s5_d.reshape(1, -1), glu_w.astype(BF16),
      glu_b.reshape(1, -1), out_g.reshape(4, GROUP_WIDTH), w_out.astype(BF16))


def _swiglu_chunks(h, wg_ref, wu_ref, wd_ref, acc, width, chunk, lead=()):
    for c in range(width // chunk):
        cs = slice(c * chunk, (c + 1) * chunk)
        a = _dot(h, wg_ref[lead + (slice(None), cs)])
        u = _dot(h, wu_ref[lead + (slice(None), cs)])
        acc = acc + _dot((a * _sigmoid(a) * u).astype(BF16), wd_ref[lead + (cs, slice(None))])
    return acc


def _ffn_kernel(x_ref, g_ref, wg_ref, wu_ref, wd_ref, o_ref):
    x = x_ref[...]
    h = (_rms(x) * g_ref[...]).astype(BF16)
    o_ref[...] = _swiglu_chunks(h, wg_ref, wu_ref, wd_ref, x, D_FF_PAD, D_FF_PAD // 2)


def _resident(shape, index_map):
    return pl.BlockSpec(shape, index_map, pipeline_mode=pl.Buffered(1))


def _ffn(x2, g, wg, wu, wd):
    T = x2.shape[0]
    tm = 512
    pad = D_FF_PAD - D_FF
    wg_p = jnp.pad(wg, ((0, 0), (0, pad))).astype(BF16)
    wu_p = jnp.pad(wu, ((0, 0), (0, pad))).astype(BF16)
    wd_p = jnp.pad(wd, ((0, pad), (0, 0))).astype(BF16)
    return pl.pallas_call(
        _ffn_kernel,
        grid=(T // tm,),
        in_specs=[pl.BlockSpec((tm, D_MODEL), lambda i: (i, 0)),
                  pl.BlockSpec((1, D_MODEL), lambda i: (0, 0)),
                  _resident((D_MODEL, D_FF_PAD), lambda i: (0, 0)),
                  _resident((D_MODEL, D_FF_PAD), lambda i: (0, 0)),
                  _resident((D_FF_PAD, D_MODEL), lambda i: (0, 0))],
        out_specs=pl.BlockSpec((tm, D_MODEL), lambda i: (i, 0)),
        out_shape=jax.ShapeDtypeStruct((T, D_MODEL), F32),
        compiler_params=_cparams(("parallel",)),
    )(x2, g.reshape(1, D_MODEL), wg_p, wu_p, wd_p)


MOE_TILE = 512
DSP_TILE = 128
CMB_TILE = 128


def _router_kernel(x_ref, g_ref, w_ref, b_ref, r_ref):
    h = _rms(x_ref[...]) * g_ref[...]
    lane = _iota((h.shape[0], LANE), 1)
    logits = jnp.where(lane < N_EXPERTS, _dot3(h, w_ref[...]) + b_ref[...], -jnp.inf)
    m1 = jnp.max(logits, axis=-1, keepdims=True)
    i1 = jnp.min(jnp.where(logits == m1, lane, LANE), axis=-1, keepdims=True)
    rest = jnp.where(lane == i1, -jnp.inf, logits)
    m2 = jnp.max(rest, axis=-1, keepdims=True)
    i2 = jnp.min(jnp.where(rest == m2, lane, LANE), axis=-1, keepdims=True)
    e2 = jnp.exp(m2 - m1)
    w1 = 1.0 / (1.0 + e2)
    w2 = e2 / (1.0 + e2)
    r_ref[...] = jnp.where(lane == 0, i1.astype(F32),
                           jnp.where(lane == 1, i2.astype(F32),
                                     jnp.where(lane == 2, w1, jnp.where(lane == 3, w2, 0.0))))


def _router(x2, g, rw, rb):
    T = x2.shape[0]
    tm = 512
    rw_p = jnp.pad(rw, ((0, 0), (0, LANE - N_EXPERTS)))
    rb_p = jnp.pad(rb, (0, LANE - N_EXPERTS)).reshape(1, LANE)
    return pl.pallas_call(
        _router_kernel,
        grid=(T // tm,),
        in_specs=[pl.BlockSpec((tm, D_MODEL), lambda i: (i, 0)),
                  pl.BlockSpec((1, D_MODEL), lambda i: (0, 0)),
                  pl.BlockSpec((D_MODEL, LANE), lambda i: (0, 0)),
                  pl.BlockSpec((1, LANE), lambda i: (0, 0))],
        out_specs=pl.BlockSpec((tm, LANE), lambda i: (i, 0)),
        out_shape=jax.ShapeDtypeStruct((T, LANE), F32),
        compiler_params=_cparams(("parallel",)),
    )(x2, g.reshape(1, D_MODEL), rw_p, rb_p)


def _smem_rows(width, nt, shift):
    return pl.BlockSpec((1, 1, width), lambda i, *_: (jnp.clip(i + shift, 0, nt - 1), 0, 0),
                        memory_space=pltpu.SMEM)


def _scatter_copy(buf, slot, r, dst_hbm, row, sem, k):
    return pltpu.make_async_copy(buf.at[slot, pl.ds(r, 1), :], dst_hbm.at[pl.ds(row, 1), :], sem.at[slot, k, r])


def _dispatch_kernel(p1_ref, p2_ref, q1_ref, q2_ref, x_ref, g_ref, xs_in, xs_hbm, hbuf, sem):
    del xs_in
    i = pl.program_id(0)
    nt = pl.num_programs(0)
    slot = i % 2
    hbuf[slot] = _rms(x_ref[...]) * g_ref[...]
    for r in range(DSP_TILE):
        _scatter_copy(hbuf, slot, r, xs_hbm, p1_ref[0, 0, r], sem, 0).start()
        _scatter_copy(hbuf, slot, r, xs_hbm, p2_ref[0, 0, r], sem, 1).start()

    def wait_all(pa, pb, s):
        for r in range(DSP_TILE):
            _scatter_copy(hbuf, s, r, xs_hbm, pa[0, 0, r], sem, 0).wait()
            _scatter_copy(hbuf, s, r, xs_hbm, pb[0, 0, r], sem, 1).wait()

    @pl.when(i > 0)
    def _():
        wait_all(q1_ref, q2_ref, 1 - slot)

    @pl.when(i == nt - 1)
    def _():
        wait_all(p1_ref, p2_ref, slot)


def _moe_dispatch(x2, g, pos, n_rows):
    T = x2.shape[0]
    nt = T // DSP_TILE
    p1 = pos[:, 0].reshape(nt, 1, DSP_TILE)
    p2 = pos[:, 1].reshape(nt, 1, DSP_TILE)
    return pl.pallas_call(
        _dispatch_kernel,
        grid=(nt,),
        in_specs=[_smem_rows(DSP_TILE, nt, 0), _smem_rows(DSP_TILE, nt, 0),
                  _smem_rows(DSP_TILE, nt, -1), _smem_rows(DSP_TILE, nt, -1),
                  pl.BlockSpec((DSP_TILE, D_MODEL), lambda i: (i, 0)),
                  pl.BlockSpec((1, D_MODEL), lambda i: (0, 0)),
                  pl.BlockSpec(memory_space=pl.ANY)],
        out_specs=pl.BlockSpec(memory_space=pl.ANY),
        out_shape=jax.ShapeDtypeStruct((n_rows, D_MODEL), F32),
        scratch_shapes=[pltpu.VMEM((2, DSP_TILE, D_MODEL), F32), pltpu.SemaphoreType.DMA((2, 2, DSP_TILE))],
        input_output_aliases={6: 0},
        compiler_params=_cparams(("arbitrary",)),
    )(p1, p2, p1, p2, x2, g.reshape(1, D_MODEL), jnp.zeros((n_rows, D_MODEL), F32))


def _moe_kernel(texp_ref, x_ref, wg_ref, wu_ref, wd_ref, y_ref):
    y_ref[...] = _swiglu_chunks(x_ref[...].astype(BF16), wg_ref, wu_ref, wd_ref,
                                jnp.zeros((MOE_TILE, D_MODEL), F32), D_FF_EXPERT, D_FF_EXPERT // 4, lead=(0,))


def _moe_experts(xs, tile_expert, wg, wu, wd):
    nt = tile_expert.shape[0]
    grid_spec = pltpu.PrefetchScalarGridSpec(
        num_scalar_prefetch=1,
        grid=(nt,),
        in_specs=[
            pl.BlockSpec((MOE_TILE, D_MODEL), lambda i, te: (i, 0)),
            _resident((1, D_MODEL, D_FF_EXPERT), lambda i, te: (te[i], 0, 0)),
            _resident((1, D_MODEL, D_FF_EXPERT), lambda i, te: (te[i], 0, 0)),
            _resident((1, D_FF_EXPERT, D_MODEL), lambda i, te: (te[i], 0, 0)),
        ],
        out_specs=pl.BlockSpec((MOE_TILE, D_MODEL), lambda i, te: (i, 0)),
    )
    return pl.pallas_call(
        _moe_kernel,
        grid_spec=grid_spec,
        out_shape=jax.ShapeDtypeStruct((nt * MOE_TILE, D_MODEL), F32),
        compiler_params=_cparams(("arbitrary",)),
    )(tile_expert, xs, wg.astype(BF16), wu.astype(BF16), wd.astype(BF16))


def _gather_copy(src_hbm, row, buf, slot, k, r, sem):
    return pltpu.make_async_copy(src_hbm.at[pl.ds(row, 1), :], buf.at[slot, k, pl.ds(r, 1), :], sem.at[slot, k, r])


def _combine_kernel(p1_ref, p2_ref, n1_ref, n2_ref, y_hbm, x_ref, r_ref, o_ref, ybuf, sem):
    i = pl.program_id(0)
    nt = pl.num_programs(0)
    slot = i % 2

    def issue(pa, pb, s):
        for r in range(CMB_TILE):
            _gather_copy(y_hbm, pa[0, 0, r], ybuf, s, 0, r, sem).start()
            _gather_copy(y_hbm, pb[0, 0, r], ybuf, s, 1, r, sem).start()

    @pl.when(i == 0)
    def _():
        issue(p1_ref, p2_ref, 0)

    @pl.when(i + 1 < nt)
    def _():
        issue(n1_ref, n2_ref, 1 - slot)

    for r in range(CMB_TILE):
        _gather_copy(y_hbm, p1_ref[0, 0, r], ybuf, slot, 0, r, sem).wait()
        _gather_copy(y_hbm, p2_ref[0, 0, r], ybuf, slot, 1, r, sem).wait()
    r = r_ref[...]
    o_ref[...] = x_ref[...] + r[:, 2:3] * ybuf[slot, 0] + r[:, 3:4] * ybuf[slot, 1]


def _moe_combine(y_sorted, pos, x2, route):
    T = x2.shape[0]
    nt = T // CMB_TILE
    p1 = pos[:, 0].reshape(nt, 1, CMB_TILE)
    p2 = pos[:, 1].reshape(nt, 1, CMB_TILE)
    return pl.pallas_call(
        _combine_kernel,
        grid=(nt,),
        in_specs=[_smem_rows(CMB_TILE, nt, 0), _smem_rows(CMB_TILE, nt, 0),
                  _smem_rows(CMB_TILE, nt, 1), _smem_rows(CMB_TILE, nt, 1),
                  pl.BlockSpec(memory_space=pl.ANY),
                  pl.BlockSpec((CMB_TILE, D_MODEL), lambda i: (i, 0)),
                  pl.BlockSpec((CMB_TILE, LANE), lambda i: (i, 0))],
        out_specs=pl.BlockSpec((CMB_TILE, D_MODEL), lambda i: (i, 0)),
        out_shape=jax.ShapeDtypeStruct((T, D_MODEL), F32),
        scratch_shapes=[pltpu.VMEM((2, 2, CMB_TILE, D_MODEL), F32),
                        pltpu.SemaphoreType.DMA((2, 2, CMB_TILE))],
        compiler_params=_cparams(("arbitrary",)),
    )(p1, p2, p1, p2, y_sorted, x2, route)


def _moe(x2, g, rw, rb, wg, wu, wd):
    T = x2.shape[0]
    route = _router(x2, g, rw, rb)
    experts = route[:, :2].astype(jnp.int32)
    onehot = (experts.reshape(-1)[:, None] == jnp.arange(N_EXPERTS)[None, :]).astype(jnp.int32)
    csum = jnp.cumsum(onehot, axis=0)
    counts = csum[-1]
    rank = jnp.sum(csum * onehot, axis=1) - 1
    padded = ((counts + MOE_TILE - 1) // MOE_TILE) * MOE_TILE
    ends = jnp.cumsum(padded)
    starts = ends - padded
    pos = (jnp.sum(starts[None, :] * onehot, axis=1) + rank).reshape(T, 2)
    nt = (2 * T) // MOE_TILE + N_EXPERTS
    tile_start = jnp.arange(nt, dtype=jnp.int32) * MOE_TILE
    tile_expert = jnp.minimum(jnp.sum(tile_start[:, None] >= ends[None, :], axis=1), N_EXPERTS - 1)
    xs = _moe_dispatch(x2, g, pos, nt * MOE_TILE)
    y_sorted = _moe_experts(xs, tile_expert.astype(jnp.int32), wg, wu, wd)
    return _moe_combine(y_sorted, pos, x2, route)


def _layer_mixers(x2, B, S, l, p):
    proj = _inproj(x2, p["norm1_g"][l], _pad_w_in(p["w_in"][l]), _head_gains(p["dil_qk_g"][l], p["nsa_qk_g"][l]))
    proj3 = proj.reshape(B, S, PROJ_PAD)
    y_dil = _dil_attention(proj3).reshape(B * S, GROUP_WIDTH)
    kc, vc, ks, vs, kw, vw = _nsa_prep(proj3, p["nsa_cmp_pos"][l], p["nsa_cmp_w1"][l], p["nsa_cmp_w2"][l],
                                       p["nsa_qk_g"][l])
    y_nsa = _nsa_attention(proj3, kc, vc, ks, vs, kw, vw).reshape(B * S, GROUP_WIDTH)
    y_gla = _gla(proj3, p["gla_wa2"][l], p["gla_ba"][l], p["gla_norm_g"][l]).reshape(B * S, GROUP_WIDTH)
    y_s5 = _s5_linear(proj3[:, :, COL_SU:COL_SU + GROUP_WIDTH], p["s5_a_re"][l], p["s5_a_im"][l],
                      p["s5_b_re"][l], p["s5_b_im"][l], p["s5_c_re"][l], p["s5_c_im"][l],
                      p["s5_log_dt"][l]).reshape(B * S, GROUP_WIDTH)
    return _mixout(y_dil, y_nsa, y_gla, y_s5, proj, x2, p["s5_d"][l], p["s5_glu_w"][l],
                   p["s5_glu_b"][l], p["out_norm_g"][l], p["w_out"][l])


def kernel(x, norm1_g, w_in, dil_qk_g, nsa_qk_g, nsa_cmp_pos, nsa_cmp_w1, nsa_cmp_w2, gla_wa2, gla_ba,
           gla_norm_g, s5_a_re, s5_a_im, s5_b_re, s5_b_im, s5_c_re, s5_c_im, s5_d, s5_log_dt, s5_glu_w,
           s5_glu_b, out_norm_g, w_out, norm2_g, ffn_w_gate, ffn_w_up, ffn_w_down, moe_router_w,
           moe_router_b, moe_w_gate, moe_w_up, moe_w_down):
    B, S, D = x.shape
    p = dict(norm1_g=norm1_g, w_in=w_in, dil_qk_g=dil_qk_g, nsa_qk_g=nsa_qk_g, nsa_cmp_pos=nsa_cmp_pos,
             nsa_cmp_w1=nsa_cmp_w1, nsa_cmp_w2=nsa_cmp_w2, gla_wa2=gla_wa2, gla_ba=gla_ba,
             gla_norm_g=gla_norm_g, s5_a_re=s5_a_re, s5_a_im=s5_a_im, s5_b_re=s5_b_re, s5_b_im=s5_b_im,
             s5_c_re=s5_c_re, s5_c_im=s5_c_im, s5_d=s5_d, s5_log_dt=s5_log_dt, s5_glu_w=s5_glu_w,
             s5_glu_b=s5_glu_b, out_norm_g=out_norm_g, w_out=w_out)
    x2 = x.reshape(B * S, D)
    depth = norm1_g.shape[0]
    for l in range(depth):
        x2 = _layer_mixers(x2, B, S, l, p)
        i = l // 2
        if l % 2 == 0:
            x2 = _ffn(x2, norm2_g[l], ffn_w_gate[i], ffn_w_up[i], ffn_w_down[i])
        else:
            x2 = _moe(x2, norm2_g[l], moe_router_w[i], moe_router_b[i], moe_w_gate[i], moe_w_up[i],
                      moe_w_down[i])
    return x2.reshape(B, S, D)
```

```python
import functools
import math

import jax
import jax.numpy as jnp
from jax import lax
from jax.experimental import pallas as pl
from jax.experimental.pallas import tpu as pltpu

F32 = jnp.float32
BF16 = jnp.bfloat16

D_MODEL = 1024
HEAD_DIM = 64
GROUP_WIDTH = 256
N_HEADS = 4
DIL_PATTERNS = ((128, 1), (512, 4), (2048, 16))
Q_BLOCK = 128
NSA_CMP_LEN = 32
NSA_CMP_STRIDE = 16
NSA_CMP_HIDDEN = 256
NSA_SEL_LEN = 64
NSA_TOPN = 16
NSA_WIN = 512
FORCE_BONUS = 1.0e4
GLA_DK = 32
GLA_DV = 64
GLA_RANK = 16
GLA_TAU = 16.0
GLA_CHUNK = 16
S5_GROUP = 16
S5_GROUPS = 16
S5_STATE = 64
S5_CHUNK = 16
D_FF = 2752
N_EXPERTS = 8
D_FF_EXPERT = 3584
EPS = 1e-6
NEG_INF = -1e30

DIL_SLOPES = tuple(2.0 ** (-float(i)) for i in (2, 4, 6, 8))
NSA_SLOPES = tuple(2.0 ** (-float(i)) for i in (1, 3, 5, 7))

COL_DQ, COL_DK, COL_DV, COL_NQ = 0, 256, 512, 768
COL_NCMP, COL_NK, COL_NV, COL_NG = 1024, 1152, 1280, 1408
COL_GQ, COL_GK, COL_GV, COL_GA, COL_GR, COL_SU = 1536, 1664, 1792, 2048, 2304, 2560
PROJ_PAD = 2816
LANE = 128
D_FF_PAD = 2816

VMEM_LIMIT = 56 * 1024 * 1024


def _cparams(sem):
    return pltpu.CompilerParams(dimension_semantics=sem, vmem_limit_bytes=VMEM_LIMIT)


def _rms(x):
    return x * lax.rsqrt(jnp.mean(x * x, axis=-1, keepdims=True) + EPS)


def _dot(a, b):
    return jnp.dot(a, b, preferred_element_type=F32)


def _dot_nt(a, b):
    return lax.dot_general(a, b, (((1,), (1,)), ((), ())), preferred_element_type=F32)


def _split(x):
    hi = x.astype(BF16)
    lo = (x - hi.astype(F32)).astype(BF16)
    return hi, lo


def _dot_lsplit(x, m):
    hi, lo = _split(x)
    return _dot(hi, m) + _dot(lo, m)


def _dot_rsplit(m, x):
    hi, lo = _split(x)
    return _dot(m, hi) + _dot(m, lo)


def _dot3(a, b):
    ah, al = _split(a)
    bh, bl = _split(b)
    return _dot(ah, bh) + _dot(ah, bl) + _dot(al, bh)


def _gelu(x):
    return 0.5 * x * (1.0 + jnp.tanh(math.sqrt(2.0 / math.pi) * (x + 0.044715 * (x * x * x))))


def _sigmoid(x):
    return 1.0 / (1.0 + jnp.exp(-x))


def _iota(shape, dim):
    return lax.broadcasted_iota(jnp.int32, shape, dim)


def _head_indicator(rows, cols, rdiv, cdiv):
    return jnp.where(_iota((rows, cols), 0) // rdiv == _iota((rows, cols), 1) // cdiv, 1.0, 0.0).astype(BF16)


def _inproj_kernel(x_ref, g_ref, w_ref, hg_ref, o_ref):
    h = _rms(x_ref[...]) * g_ref[...]
    o_ref[...] = _dot(h.astype(BF16), w_ref[...])
    ind = _head_indicator(GROUP_WIDTH, GROUP_WIDTH, HEAD_DIM, HEAD_DIM)
    for lo, width in NORM_SEGMENTS:
        v = o_ref[:, lo:lo + width]
        ms = _dot_lsplit(v * v, ind[:width, :width]) * (1.0 / HEAD_DIM)
        o_ref[:, lo:lo + width] = v * lax.rsqrt(ms + EPS) * hg_ref[:, lo:lo + width]


NORM_COLS = COL_NV
NORM_SEGMENTS = ((COL_DQ, GROUP_WIDTH), (COL_DK, GROUP_WIDTH), (COL_NQ, GROUP_WIDTH), (COL_NK, LANE))


def _head_gains(dil_g, nsa_g):
    scale = HEAD_DIM ** -0.5
    z = jnp.zeros((HEAD_DIM,), F32)
    row = jnp.concatenate([jnp.tile(dil_g[0] * scale, N_HEADS), jnp.tile(dil_g[1], N_HEADS),
                           jnp.zeros((GROUP_WIDTH,), F32), jnp.tile(nsa_g[0] * scale, N_HEADS),
                           z, z, nsa_g[2], nsa_g[3]])
    return row.reshape(1, NORM_COLS)


def _inproj(x2, g, w_pad, head_gains):
    T = x2.shape[0]
    tm = 512
    return pl.pallas_call(
        _inproj_kernel,
        grid=(T // tm,),
        in_specs=[
            pl.BlockSpec((tm, D_MODEL), lambda i: (i, 0)),
            pl.BlockSpec((1, D_MODEL), lambda i: (0, 0)),
            pl.BlockSpec((D_MODEL, PROJ_PAD), lambda i: (0, 0)),
            pl.BlockSpec((1, NORM_COLS), lambda i: (0, 0)),
        ],
        out_specs=pl.BlockSpec((tm, PROJ_PAD), lambda i: (i, 0)),
        out_shape=jax.ShapeDtypeStruct((T, PROJ_PAD), F32),
        compiler_params=_cparams(("parallel",)),
    )(x2, g.reshape(1, D_MODEL), w_pad, head_gains)


def _pad_w_in(w):
    def seg(lo, width, pad_to):
        s = w[:, lo:lo + width]
        if pad_to > width:
            s = jnp.pad(s, ((0, 0), (0, pad_to - width)))
        return s
    k_cmp, v_cmp = seg(1024, 64, 64), seg(1088, 64, 64)
    k_slc, v_slc = seg(1152, 64, 64), seg(1216, 64, 64)
    k_win, v_win = seg(1280, 64, 64), seg(1344, 64, 64)
    pieces = [
        seg(0, 256, 256), seg(256, 256, 256), seg(512, 256, 256), seg(768, 256, 256),
        k_cmp, v_cmp, k_slc, k_win, v_slc, v_win,
        seg(1408, 12, 128),
        seg(1420, 128, 128), seg(1548, 128, 128), seg(1676, 256, 256),
        seg(1932, 16, 256),
        seg(1948, 256, 256), seg(2204, 256, 256),
    ]
    out = jnp.concatenate(pieces, axis=1)
    assert out.shape[1] == PROJ_PAD
    return out.astype(BF16)


DIL_SB = Q_BLOCK * max(d for _, d in DIL_PATTERNS)
DIL_SUBSTEPS = DIL_SB // Q_BLOCK


def _dil_kernel(q0_ref, q1_ref, k0_ref, k1_ref, v0_ref, v1_ref, o_ref,
                ks_ref, vs_ref, bias_ref, m_ref, l_ref, acc_ref):
    i = pl.program_id(1)
    rows4 = N_HEADS * Q_BLOCK

    def ld2(ref, rows):
        return jnp.concatenate([ref[0, rows, :], ref[1, rows, :]], axis=1)

    def st2(ref, rows, val):
        ref[0, rows, :] = val[:, :LANE]
        ref[1, rows, :] = val[:, LANE:]

    @pl.when(i == 0)
    def _():
        ks_ref[:, 0:DIL_SB, :] = jnp.zeros((2, DIL_SB, LANE), F32)
        vs_ref[:, 0:DIL_SB, :] = jnp.zeros((2, DIL_SB, LANE), F32)
        ri = _iota((rows4, 2 * Q_BLOCK), 0)
        ci = _iota((rows4, 2 * Q_BLOCK), 1)
        delta = Q_BLOCK + ri % Q_BLOCK - ci
        head = ri // Q_BLOCK
        slope = jnp.where(head == 0, DIL_SLOPES[0],
                          jnp.where(head == 1, DIL_SLOPES[1],
                                    jnp.where(head == 2, DIL_SLOPES[2], DIL_SLOPES[3]))).astype(F32)
        band = (delta >= 0) & (delta <= Q_BLOCK)
        for p, (_, dil) in enumerate(DIL_PATTERNS):
            b = jnp.where(band, -(slope * dil) * delta.astype(F32), NEG_INF)
            bias_ref[p, 0] = b
            bias_ref[p, 1] = jnp.where(ci >= Q_BLOCK, b, NEG_INF)

    @pl.when(i > 0)
    def _():
        ks_ref[:, 0:DIL_SB, :] = ks_ref[:, DIL_SB:, :]
        vs_ref[:, 0:DIL_SB, :] = vs_ref[:, DIL_SB:, :]

    ks_ref[0, DIL_SB:, :] = k0_ref[0]
    ks_ref[1, DIL_SB:, :] = k1_ref[0]
    vs_ref[0, DIL_SB:, :] = v0_ref[0]
    vs_ref[1, DIL_SB:, :] = v1_ref[0]
    lane_head = _iota((Q_BLOCK, GROUP_WIDTH), 1) // HEAD_DIM

    def spread(x):
        out = jnp.zeros((Q_BLOCK, GROUP_WIDTH), F32)
        for h in range(N_HEADS):
            out = jnp.where(lane_head == h, x[h * Q_BLOCK:(h + 1) * Q_BLOCK], out)
        return out

    order = sorted(range(len(DIL_PATTERNS)), key=lambda n: -DIL_PATTERNS[n][1])
    assert DIL_PATTERNS[order[-1]][1] == 1
    for p in order:
        dil = DIL_PATTERNS[p][1]

        def sub(j, carry, p=p, dil=dil):
            start = (j // dil) * (Q_BLOCK * dil) + j % dil
            rows = pl.ds(start, Q_BLOCK, stride=dil)
            cur = pl.ds(DIL_SB + start, Q_BLOCK, stride=dil)
            prev = pl.ds(DIL_SB + start - Q_BLOCK * dil, Q_BLOCK, stride=dil)
            q = jnp.concatenate([q0_ref[0, rows, :], q1_ref[0, rows, :]], axis=1)
            q4 = jnp.concatenate([jnp.where(lane_head == h, q, 0.0) for h in range(N_HEADS)], axis=0)
            kk = jnp.concatenate([ld2(ks_ref, prev), ld2(ks_ref, cur)], axis=0).astype(BF16)
            vv = jnp.concatenate([ld2(vs_ref, prev), ld2(vs_ref, cur)], axis=0).astype(BF16)
            first = jnp.where((i == 0) & (j < dil), 1, 0)
            s = _dot_nt(q4.astype(BF16), kk) + bias_ref[p, first]
            m = jnp.max(s, axis=-1, keepdims=True)
            e = jnp.exp(s - m)
            l = jnp.sum(e, axis=-1, keepdims=True)
            o_t = spread(_dot(e.astype(BF16), vv))
            m_t = spread(m)
            l_t = spread(l)
            if p == order[0]:
                st2(m_ref, rows, m_t)
                st2(l_ref, rows, l_t)
                st2(acc_ref, rows, o_t)
                return carry
            m_old = ld2(m_ref, rows)
            m_new = jnp.maximum(m_old, m_t)
            a = jnp.exp(m_old - m_new)
            b = jnp.exp(m_t - m_new)
            l_new = a * ld2(l_ref, rows) + b * l_t
            acc_new = a * ld2(acc_ref, rows) + b * o_t
            if p == order[-1]:
                o_ref[0, pl.ds(pl.multiple_of(start, Q_BLOCK), Q_BLOCK), :] = acc_new / l_new
            else:
                st2(m_ref, rows, m_new)
                st2(l_ref, rows, l_new)
                st2(acc_ref, rows, acc_new)
            return carry
        lax.fori_loop(0, DIL_SUBSTEPS, sub, 0)


def _dil_attention(proj3):
    B, S, _ = proj3.shape
    assert all(win // d == Q_BLOCK for win, d in DIL_PATTERNS) and S % DIL_SB == 0

    def halves(c):
        return [pl.BlockSpec((1, DIL_SB, LANE), lambda b, i, c=c, h=h: (b, i, c // LANE + h)) for h in range(2)]

    sb = pltpu.VMEM((2, DIL_SB, LANE), F32)
    sb2 = pltpu.VMEM((2, 2 * DIL_SB, LANE), F32)
    return pl.pallas_call(
        _dil_kernel,
        grid=(B, S // DIL_SB),
        in_specs=halves(COL_DQ) + halves(COL_DK) + halves(COL_DV),
        out_specs=pl.BlockSpec((1, DIL_SB, GROUP_WIDTH), lambda b, i: (b, i, 0)),
        out_shape=jax.ShapeDtypeStruct((B, S, GROUP_WIDTH), F32),
        scratch_shapes=[sb2, sb2, pltpu.VMEM((len(DIL_PATTERNS), 2, N_HEADS * Q_BLOCK, 2 * Q_BLOCK), F32),
                        sb, sb, sb],
        compiler_params=_cparams(("parallel", "arbitrary")),
    )(*([proj3] * 6))


def _nsa_prep_kernel(xk_ref, xv_ref, kraw_ref, vraw_ref, pos_ref, w1_ref, w2_ref, g_ref,
                     kc_ref, vc_ref, ks_ref, vs_ref, kw_ref, vw_ref):
    half = NSA_CMP_STRIDE * HEAD_DIM

    def compress(x, idx):
        w1 = w1_ref[idx]
        xb = x.astype(BF16)
        a = _dot(xb, w1[:half, :])
        b = _dot(xb, w1[half:, :])
        bias = _dot(pos_ref[idx].astype(BF16), w1)[0:1, :]
        pre = a + pltpu.roll(b, b.shape[0] - 1, 0) + bias
        return _dot(_gelu(pre).astype(BF16), w2_ref[idx])

    def with_pos(k, pos):
        lane = _iota((k.shape[0], HEAD_DIM), 1)
        extra = jnp.where(lane == 0, (pos // 256).astype(F32), jnp.where(lane == 1, (pos % 256).astype(F32), 0.0))
        return jnp.concatenate([k, extra], axis=1).astype(BF16)

    def with_one_t(v):
        lane = _iota((v.shape[0], HEAD_DIM), 1)
        return jnp.concatenate([v, jnp.where(lane == 0, 1.0, 0.0)], axis=1).T.astype(BF16)

    ng = xk_ref.shape[1]
    cend = _iota((ng, 1), 0) * NSA_CMP_STRIDE + (NSA_CMP_LEN - 1)
    kc_ref[0] = with_pos(_rms(compress(xk_ref[0], 0)) * g_ref[1:2, :], cend)
    vc_ref[0] = with_one_t(compress(xv_ref[0], 1))
    kraw = kraw_ref[0]
    tpos = _iota((kraw.shape[0], 1), 0)
    ks_ref[0] = with_pos(kraw[:, :HEAD_DIM], tpos)
    kw_ref[0] = with_pos(kraw[:, HEAD_DIM:], tpos)
    for c in range(vs_ref.shape[1]):
        vs_ref[0, c] = with_one_t(vraw_ref[0, c * NSA_KCHUNK:(c + 1) * NSA_KCHUNK, :HEAD_DIM])
    for c in range(vw_ref.shape[1]):
        vw_ref[0, c] = with_one_t(vraw_ref[0, c * Q_BLOCK:(c + 1) * Q_BLOCK, HEAD_DIM:])


NSA_KCHUNK = 2 * Q_BLOCK


def _nsa_prep(proj3, pos, w1, w2, g):
    B, S, _ = proj3.shape
    ng = S // NSA_CMP_STRIDE
    gw = NSA_CMP_STRIDE * HEAD_DIM
    xk = proj3[:, :, COL_NCMP:COL_NCMP + HEAD_DIM].reshape(B, ng, gw)
    xv = proj3[:, :, COL_NCMP + HEAD_DIM:COL_NCMP + 2 * HEAD_DIM].reshape(B, ng, gw)
    pos8 = jnp.broadcast_to(pos.reshape(2, 1, NSA_CMP_LEN * HEAD_DIM), (2, 8, NSA_CMP_LEN * HEAD_DIM))
    small = jax.ShapeDtypeStruct((B, ng, LANE), BF16)
    small_t = jax.ShapeDtypeStruct((B, LANE, ng), BF16)
    big = jax.ShapeDtypeStruct((B, S, LANE), BF16)
    vs_t = jax.ShapeDtypeStruct((B, S // NSA_KCHUNK, LANE, NSA_KCHUNK), BF16)
    vw_t = jax.ShapeDtypeStruct((B, S // Q_BLOCK, LANE, Q_BLOCK), BF16)
    small_spec = pl.BlockSpec((1, ng, LANE), lambda b: (b, 0, 0))
    small_t_spec = pl.BlockSpec((1, LANE, ng), lambda b: (b, 0, 0))
    big_spec = pl.BlockSpec((1, S, LANE), lambda b: (b, 0, 0))
    vs_spec = pl.BlockSpec((1, S // NSA_KCHUNK, LANE, NSA_KCHUNK), lambda b: (b, 0, 0, 0))
    vw_spec = pl.BlockSpec((1, S // Q_BLOCK, LANE, Q_BLOCK), lambda b: (b, 0, 0, 0))
    return pl.pallas_call(
        _nsa_prep_kernel,
        grid=(B,),
        in_specs=[
            pl.BlockSpec((1, ng, gw), lambda b: (b, 0, 0)),
            pl.BlockSpec((1, ng, gw), lambda b: (b, 0, 0)),
            pl.BlockSpec((1, S, LANE), lambda b: (b, 0, COL_NK // LANE)),
            pl.BlockSpec((1, S, LANE), lambda b: (b, 0, COL_NV // LANE)),
            pl.BlockSpec((2, 8, NSA_CMP_LEN * HEAD_DIM), lambda b: (0, 0, 0)),
            pl.BlockSpec((2, NSA_CMP_LEN * HEAD_DIM, NSA_CMP_HIDDEN), lambda b: (0, 0, 0)),
            pl.BlockSpec((2, NSA_CMP_HIDDEN, HEAD_DIM), lambda b: (0, 0, 0)),
            pl.BlockSpec((4, HEAD_DIM), lambda b: (0, 0)),
        ],
        out_specs=[small_spec, small_t_spec, big_spec, vs_spec, big_spec, vw_spec],
        out_shape=[small, small_t, big, vs_t, big, vw_t],
        compiler_params=_cparams(("parallel",)),
    )(xk, xv, proj3, proj3, pos8, w1.astype(BF16), w2.astype(BF16), g)


def _nsa_kernel(q_ref, gate_ref, kc_ref, vc_ref, ks_ref, vs_ref, kw_ref, vw_ref, o_ref, sel_ref, *, seq):
    i = pl.program_id(1)
    t0 = i * Q_BLOCK
    q_t = q_ref[0].T
    sub = _iota((HEAD_DIM, Q_BLOCK), 0)
    q = jnp.concatenate(
        [jnp.concatenate([q_t[h * HEAD_DIM:(h + 1) * HEAD_DIM],
                          jnp.where(sub == 0, NSA_SLOPES[h] * 256.0, jnp.where(sub == 1, NSA_SLOPES[h], 0.0))],
                         axis=0) for h in range(N_HEADS)], axis=1).astype(BF16)
    tq = t0 + _iota((1, Q_BLOCK), 1)

    def masked(s, ok):
        return jnp.concatenate([jnp.where(ok, s[:, h * Q_BLOCK:(h + 1) * Q_BLOCK], NEG_INF)
                                for h in range(N_HEADS)], axis=1)

    def heads(x):
        return jnp.concatenate([x] * N_HEADS, axis=1)

    ncmp = kc_ref.shape[1]
    cend = _iota((ncmp, 1), 0) * NSA_CMP_STRIDE + (NSA_CMP_LEN - 1)
    s = masked(_dot(kc_ref[0], q), cend <= tq)
    e = jnp.exp(s - jnp.max(s, axis=0, keepdims=True))
    col_ok = heads(jnp.where(tq >= NSA_CMP_LEN - 1, 1.0, 0.0))
    p = e * (col_ok / jnp.sum(e, axis=0, keepdims=True))
    o_cmp = _dot(vc_ref[0], p.astype(BF16))[:HEAD_DIM]

    nsel = seq // NSA_SEL_LEN
    psum = p[:, :Q_BLOCK] + p[:, Q_BLOCK:2 * Q_BLOCK] + p[:, 2 * Q_BLOCK:3 * Q_BLOCK] + p[:, 3 * Q_BLOCK:]
    jj = _iota((nsel, ncmp), 0)
    cc = _iota((nsel, ncmp), 1)
    ratio = NSA_SEL_LEN // NSA_CMP_STRIDE
    cover_t = jnp.where((cc < ratio * jj + ratio) & (cc * NSA_CMP_STRIDE + NSA_CMP_LEN > jj * NSA_SEL_LEN),
                        1.0, 0.0).astype(BF16)
    imp = _dot_rsplit(cover_t, psum)
    jrow = _iota((nsel, Q_BLOCK), 0)
    cur = (t0 + _iota((nsel, Q_BLOCK), 1)) // NSA_SEL_LEN
    forced = (jrow == 0) | (jrow == cur) | (jrow == cur - 1)
    impv = jnp.where(jrow <= cur, imp + jnp.where(forced, FORCE_BONUS, 0.0), NEG_INF)
    rank = jnp.zeros((nsel, Q_BLOCK), F32)
    for ii in range(nsel):
        row = impv[ii:ii + 1, :]
        tie = jnp.where(jrow > ii, 1.0, 0.0)
        rank = rank + jnp.where(row > impv, 1.0, jnp.where(row == impv, tie, 0.0))
    sel_ref[...] = jnp.where((rank < min(NSA_TOPN, nsel)) & (jrow <= cur), 1.0, 0.0)

    kpos_col = _iota((NSA_KCHUNK, 1), 0)
    per_chunk = NSA_KCHUNK // NSA_SEL_LEN

    def chunk_scores(c):
        k0 = pl.multiple_of(c * NSA_KCHUNK, NSA_KCHUNK)
        picked = jnp.concatenate(
            [jnp.broadcast_to(sel_ref[pl.ds(c * per_chunk + b, 1), :], (NSA_SEL_LEN, Q_BLOCK))
             for b in range(per_chunk)], axis=0)
        ok = (picked > 0.5) & (k0 + kpos_col <= tq)
        return masked(_dot(ks_ref[0, pl.ds(k0, NSA_KCHUNK), :], q), ok)

    def sel_body(it, carry):
        m_i, acc = carry
        sa = chunk_scores(2 * it)
        sb = chunk_scores(2 * it + 1)
        m_new = jnp.maximum(m_i, jnp.maximum(jnp.max(sa, axis=0, keepdims=True), jnp.max(sb, axis=0, keepdims=True)))
        pa = jnp.exp(sa - m_new).astype(BF16)
        pb = jnp.exp(sb - m_new).astype(BF16)
        return m_new, jnp.exp(m_i - m_new) * acc + _dot(vs_ref[0, 2 * it], pa) + _dot(vs_ref[0, 2 * it + 1], pb)

    cols = N_HEADS * Q_BLOCK
    init = (jnp.full((1, cols), NEG_INF, F32), jnp.zeros((LANE, cols), F32))
    _, acc_s = lax.fori_loop(0, (t0 + Q_BLOCK - 1) // (2 * NSA_KCHUNK) + 1, sel_body, init)
    o_sel = acc_s[:HEAD_DIM] / acc_s[HEAD_DIM:HEAD_DIM + 1]

    nwb = NSA_WIN // Q_BLOCK + 1
    span = nwb * Q_BLOCK
    b0 = jnp.maximum(i - NSA_WIN // Q_BLOCK, 0)
    start = pl.multiple_of(b0 * Q_BLOCK, Q_BLOCK)
    dw = tq - (start + _iota((span, 1), 0))
    sw = masked(_dot(kw_ref[0, pl.ds(start, span), :], q), (dw >= 0) & (dw < NSA_WIN))
    pw = jnp.exp(sw - jnp.max(sw, axis=0, keepdims=True)).astype(BF16)
    rw = _dot(vw_ref[0, b0], pw[:Q_BLOCK])
    for j in range(1, nwb):
        rw = rw + _dot(vw_ref[0, b0 + j], pw[j * Q_BLOCK:(j + 1) * Q_BLOCK])
    o_win = rw[:HEAD_DIM] / rw[HEAD_DIM:HEAD_DIM + 1]

    gate = _sigmoid(gate_ref[0].T)
    outs = []
    for h in range(N_HEADS):
        c = slice(h * Q_BLOCK, (h + 1) * Q_BLOCK)
        outs.append(gate[3 * h:3 * h + 1] * o_cmp[:, c] + gate[3 * h + 1:3 * h + 2] * o_sel[:, c]
                    + gate[3 * h + 2:3 * h + 3] * o_win[:, c])
    o_ref[0] = jnp.concatenate(outs, axis=0).T


def _nsa_attention(proj3, kc, vc, ks, vs, kw, vw):
    B, S, _ = proj3.shape
    nb = S // Q_BLOCK
    ng = kc.shape[1]
    assert S >= NSA_WIN + Q_BLOCK and S % (2 * NSA_KCHUNK) == 0
    small_spec = pl.BlockSpec((1, ng, LANE), lambda b, i: (b, 0, 0))
    small_t_spec = pl.BlockSpec((1, LANE, ng), lambda b, i: (b, 0, 0))
    big_spec = pl.BlockSpec((1, S, LANE), lambda b, i: (b, 0, 0))
    vs_spec = pl.BlockSpec((1, S // NSA_KCHUNK, LANE, NSA_KCHUNK), lambda b, i: (b, 0, 0, 0))
    vw_spec = pl.BlockSpec((1, S // Q_BLOCK, LANE, Q_BLOCK), lambda b, i: (b, 0, 0, 0))
    return pl.pallas_call(
        functools.partial(_nsa_kernel, seq=S),
        grid=(B, nb),
        in_specs=[
            pl.BlockSpec((1, Q_BLOCK, GROUP_WIDTH), lambda b, i: (b, i, COL_NQ // GROUP_WIDTH)),
            pl.BlockSpec((1, Q_BLOCK, LANE), lambda b, i: (b, i, COL_NG // LANE)),
            small_spec, small_t_spec, big_spec, vs_spec, big_spec, vw_spec,
        ],
        out_specs=pl.BlockSpec((1, Q_BLOCK, GROUP_WIDTH), lambda b, i: (b, i, 0)),
        out_shape=jax.ShapeDtypeStruct((B, S, GROUP_WIDTH), F32),
        scratch_shapes=[pltpu.VMEM((S // NSA_SEL_LEN, Q_BLOCK), F32)],
        compiler_params=_cparams(("parallel", "parallel")),
    )(proj3, proj3, kc, vc, ks, vs, kw, vw)


GLA_BLOCK = 128
GLA_W = N_HEADS * GLA_DK


def _gla_kernel(q_ref, k_ref, v_ref, a_ref, r_ref, wa_ref, ba_ref, g_ref, o_ref, st_ref):
    nsub = GLA_BLOCK // GLA_CHUNK

    @pl.when(pl.program_id(1) == 0)
    def _():
        st_ref[...] = jnp.zeros_like(st_ref)

    q = q_ref[0] * (GLA_DK ** -0.5)
    k = k_ref[0]
    v = v_ref[0]
    z = _dot3(a_ref[0], wa_ref[...]) + ba_ref[...]
    log_a = (jnp.minimum(z, 0.0) - jnp.log(1.0 + jnp.exp(-jnp.abs(z)))) * (1.0 / GLA_TAU)
    ri = _iota((GLA_BLOCK, GLA_BLOCK), 0)
    ci = _iota((GLA_BLOCK, GLA_BLOCK), 1)
    same = (ri // GLA_CHUNK) == (ci // GLA_CHUNK)
    b_loc = _dot_rsplit(jnp.where(same & (ci <= ri), 1.0, 0.0).astype(BF16), log_a)
    b_tot = _dot_rsplit(jnp.where(same, 1.0, 0.0).astype(BF16), log_a)
    qd = q * jnp.exp(b_loc)
    kd = k * jnp.exp(b_tot - b_loc)

    ind_kv = jnp.where(_iota((GLA_W, GROUP_WIDTH), 0) // GLA_DK == _iota((GLA_W, GROUP_WIDTH), 1) // GLA_DV,
                       1.0, 0.0).astype(BF16)
    rsub = _iota((GLA_BLOCK, 1), 0) % GLA_CHUNK
    o = jnp.zeros((GLA_BLOCK, GROUP_WIDTH), F32)
    for s in range(GLA_CHUNK):
        if s == 0:
            ks_, bs_, vs_ = k, b_loc, v
        else:
            ks_, bs_, vs_ = pltpu.roll(k, s, 0), pltpu.roll(b_loc, s, 0), pltpu.roll(v, s, 0)
        dec = jnp.where(rsub >= s, jnp.exp(jnp.minimum(b_loc - bs_, 0.0)), 0.0)
        w = _dot((q * ks_ * dec).astype(BF16), ind_kv)
        o = o + w * vs_

    v_t = v.T.astype(BF16)
    head_ok = _iota((GROUP_WIDTH, GLA_W), 0) // GLA_DV == _iota((GROUP_WIDTH, GLA_W), 1) // GLA_DK
    rblk = _iota((GLA_BLOCK, 1), 0) // GLA_CHUNK
    st = st_ref[...]
    for c in range(nsub):
        in_c = rblk == c
        o = o + _dot_nt(jnp.where(in_c, qd, 0.0).astype(BF16), st.astype(BF16))
        decay = jnp.exp(b_tot[c * GLA_CHUNK:c * GLA_CHUNK + 1, :])
        upd = _dot(v_t, jnp.where(in_c, kd, 0.0).astype(BF16))
        st = st * decay + jnp.where(head_ok, upd, 0.0)
    st_ref[...] = st

    ind_vv = jnp.where(_iota((GROUP_WIDTH, GROUP_WIDTH), 0) // GLA_DV == _iota((GROUP_WIDTH, GROUP_WIDTH), 1) // GLA_DV,
                       1.0, 0.0).astype(BF16)
    ms = _dot_lsplit(o * o, ind_vv) * (1.0 / GLA_DV)
    r = r_ref[0]
    o_ref[0] = o * lax.rsqrt(ms + EPS) * g_ref[...] * (r * _sigmoid(r))


def _gla(proj3, wa2, ba, norm_g):
    B, S, _ = proj3.shape
    nb = S // GLA_BLOCK
    wa_pad = jnp.zeros((LANE, GLA_W), F32).at[:GLA_RANK].set(wa2)
    g_t = jnp.tile(norm_g.reshape(1, GLA_DV), (1, N_HEADS))

    def col(c, w):
        return pl.BlockSpec((1, GLA_BLOCK, w), lambda b, j: (b, j, c // w))

    return pl.pallas_call(
        _gla_kernel,
        grid=(B, nb),
        in_specs=[col(COL_GQ, LANE), col(COL_GK, LANE), col(COL_GV, GROUP_WIDTH), col(COL_GA, LANE),
                  col(COL_GR, GROUP_WIDTH),
                  pl.BlockSpec((LANE, GLA_W), lambda b, j: (0, 0)),
                  pl.BlockSpec((1, GLA_W), lambda b, j: (0, 0)),
                  pl.BlockSpec((1, GROUP_WIDTH), lambda b, j: (0, 0))],
        out_specs=pl.BlockSpec((1, GLA_BLOCK, GROUP_WIDTH), lambda b, j: (b, j, 0)),
        out_shape=jax.ShapeDtypeStruct((B, S, GROUP_WIDTH), F32),
        scratch_shapes=[pltpu.VMEM((GROUP_WIDTH, GLA_W), F32)],
        compiler_params=_cparams(("parallel", "arbitrary")),
    )(proj3, proj3, proj3, proj3, proj3, wa_pad, ba.reshape(1, GLA_W), g_t)


S5_CW = S5_CHUNK * S5_GROUP
S5_SW = 2 * S5_STATE


def _s5_matrices(a_re, a_im, b_re, b_im, c_re, c_im, log_dt):
    hp = lax.Precision.HIGHEST
    L = S5_CHUNK
    dt = jnp.exp(log_dt)[:, None]
    lam_re, lam_im = dt * a_re, dt * a_im
    tau = jnp.arange(L + 1, dtype=F32)[:, None, None]
    mag = jnp.exp(tau * lam_re)
    p_re, p_im = mag * jnp.cos(tau * lam_im), mag * jnp.sin(tau * lam_im)
    den = a_re * a_re + a_im * a_im
    f_re = ((p_re[1] - 1.0) * a_re + p_im[1] * a_im) / den
    f_im = (p_im[1] * a_re - (p_re[1] - 1.0) * a_im) / den
    bb_re = f_re[..., None] * b_re - f_im[..., None] * b_im
    bb_im = f_re[..., None] * b_im + f_im[..., None] * b_re
    pb_re = p_re[..., None] * bb_re - p_im[..., None] * bb_im
    pb_im = p_re[..., None] * bb_im + p_im[..., None] * bb_re
    kern = (jnp.einsum('gon,tgnc->tgoc', c_re, pb_re, precision=hp)
            - jnp.einsum('gon,tgnc->tgoc', c_im, pb_im, precision=hp))
    eye = jnp.eye(S5_GROUPS, dtype=F32)
    s_idx = jnp.arange(L)[:, None]
    t_idx = jnp.arange(L)[None, :]
    lag = jnp.clip(t_idx - s_idx, 0, L)
    toep = jnp.where((t_idx >= s_idx)[..., None, None, None], kern[lag], 0.0)
    toep = toep.transpose(0, 2, 4, 1, 3)
    toep = toep[:, :, :, :, None, :] * eye[None, :, None, None, :, None]
    toep = toep.reshape(S5_ROW, S5_ROW)
    rev = (L - 1 - jnp.arange(L))
    b_in = jnp.stack([pb_re[rev], pb_im[rev]], axis=0).transpose(1, 2, 4, 0, 3)
    b_in = b_in[:, :, :, :, None, :] * eye[None, :, None, None, :, None]
    b_in = b_in.reshape(S5_ROW, S5_XW)
    pt_re, pt_im = p_re[1:], p_im[1:]
    co_re = c_re[None] * pt_re[:, :, None, :] - c_im[None] * pt_im[:, :, None, :]
    co_im = -c_re[None] * pt_im[:, :, None, :] - c_im[None] * pt_re[:, :, None, :]
    c_out = jnp.stack([co_re, co_im], axis=0).transpose(0, 2, 4, 1, 3)
    c_out = c_out[:, :, :, :, None, :] * eye[None, :, None, None, :, None]
    c_out = c_out.reshape(S5_XW, S5_ROW)
    a_chunk = jnp.stack([p_re[L].reshape(1, -1), p_im[L].reshape(1, -1)], axis=0)
    return toep.astype(BF16), b_in.astype(BF16), c_out.astype(BF16), a_chunk


S5_ROW = S5_CHUNK * GROUP_WIDTH
S5_XW = 2 * S5_GROUPS * S5_STATE
S5_TN = 512


def _rowmm_kernel(a_ref, w_ref, o_ref):
    o_ref[...] = _dot(a_ref[...], w_ref[...])


def _rowmm_add_kernel(a_ref, w_ref, y_ref, o_ref):
    o_ref[...] = y_ref[...] + _dot(a_ref[...], w_ref[...])


def _rowmm(a, w, add=None):
    R, K = a.shape
    N = w.shape[1]
    in_specs = [_resident((R, K), lambda j: (0, 0)), pl.BlockSpec((K, S5_TN), lambda j: (0, j))]
    args = [a, w]
    if add is not None:
        in_specs.append(pl.BlockSpec((R, S5_TN), lambda j: (0, j)))
        args.append(add)
    return pl.pallas_call(
        _rowmm_kernel if add is None else _rowmm_add_kernel,
        grid=(N // S5_TN,),
        in_specs=in_specs,
        out_specs=pl.BlockSpec((R, S5_TN), lambda j: (0, j)),
        out_shape=jax.ShapeDtypeStruct((R, N), F32),
        compiler_params=_cparams(("parallel",)),
    )(*args)


def _s5_scan_kernel(v_ref, a_ref, x_ref, st_ref):
    half = a_ref.shape[2]
    nb = v_ref.shape[1]
    a_r = jnp.broadcast_to(a_ref[0], (nb, half))
    a_i = jnp.broadcast_to(a_ref[1], (nb, half))

    @pl.when(pl.program_id(0) == 0)
    def _():
        st_ref[...] = jnp.zeros_like(st_ref)

    def body(m, carry):
        x_r, x_i = carry
        x_ref[m, :, :half] = x_r
        x_ref[m, :, half:] = x_i
        v = v_ref[m]
        return a_r * x_r - a_i * x_i + v[:, :half], a_r * x_i + a_i * x_r + v[:, half:]

    x_r, x_i = lax.fori_loop(0, v_ref.shape[0], body, (st_ref[:, :half], st_ref[:, half:]))
    st_ref[:, :half] = x_r
    st_ref[:, half:] = x_i


def _s5_linear(u3, a_re, a_im, b_re, b_im, c_re, c_im, log_dt):
    B, S, _ = u3.shape
    M = S // S5_CHUNK
    toep, b_in, c_out, a_chunk = _s5_matrices(a_re, a_im, b_re, b_im, c_re, c_im, log_dt)
    rows = u3.reshape(B * M, S5_ROW).astype(BF16)
    y_in = _rowmm(rows, toep)
    v = _rowmm(rows, b_in)
    v_t = v.reshape(B, M, S5_XW).transpose(1, 0, 2)
    mc = 32
    x_t = pl.pallas_call(
        _s5_scan_kernel,
        grid=(M // mc,),
        in_specs=[pl.BlockSpec((mc, B, S5_XW), lambda j: (j, 0, 0)),
                  pl.BlockSpec((2, 1, S5_XW // 2), lambda j: (0, 0, 0))],
        out_specs=pl.BlockSpec((mc, B, S5_XW), lambda j: (j, 0, 0)),
        out_shape=jax.ShapeDtypeStruct((M, B, S5_XW), F32),
        scratch_shapes=[pltpu.VMEM((B, S5_XW), F32)],
        compiler_params=_cparams(("arbitrary",)),
    )(v_t, a_chunk)
    x = x_t.transpose(1, 0, 2).reshape(B * M, S5_XW).astype(BF16)
    y = _rowmm(x, c_out, add=y_in)
    return y.reshape(B, S, GROUP_WIDTH)


def _mixout_kernel(dil_ref, nsa_ref, gla_ref, s5y_ref, u_ref, x_ref,
                   s5d_ref, gw_ref, gb_ref, ng_ref, wo_ref, o_ref):
    u = u_ref[...]
    hg = _gelu(s5y_ref[...] + s5d_ref[...] * u)
    y_s5 = hg * _sigmoid(_dot(hg.astype(BF16), gw_ref[...]) + gb_ref[...])
    parts = [dil_ref[...], nsa_ref[...], gla_ref[...], y_s5]
    mix = jnp.concatenate([(_rms(parts[n]) * ng_ref[n:n + 1, :]).astype(BF16) for n in range(4)], axis=1)
    o_ref[...] = x_ref[...] + _dot(mix, wo_ref[...])


def _mixout(y_dil, y_nsa, y_gla, y_s5, proj, x2, s5_d, glu_w, glu_b, out_g, w_out):
    T = x2.shape[0]
    tm = 512

    def row(w, c=0):
        return pl.BlockSpec((tm, w), lambda i: (i, c))

    def const(shape):
        return pl.BlockSpec(shape, lambda i: (0,) * len(shape))

    return pl.pallas_call(
        _mixout_kernel,
        grid=(T // tm,),
        in_specs=[row(GROUP_WIDTH), row(GROUP_WIDTH), row(GROUP_WIDTH),
                  row(GROUP_WIDTH), row(GROUP_WIDTH, COL_SU // GROUP_WIDTH), row(D_MODEL),
                  const((1, GROUP_WIDTH)), const((GROUP_WIDTH, GROUP_WIDTH)), const((1, GROUP_WIDTH)),
                  const((4, GROUP_WIDTH)), const((D_MODEL, D_MODEL))],
        out_specs=row(D_MODEL),
        out_shape=jax.ShapeDtypeStruct((T, D_MODEL), F32),
        compiler_params=_cparams(("parallel",)),
    )(y_dil, y_nsa, y_gla, y_s5, proj, x2, s5_d.reshape(1, -1), glu_w.astype(BF16),
      glu_b.reshape(1, -1), out_g.reshape(4, GROUP_WIDTH), w_out.astype(BF16))


def _swiglu_chunks(h, wg_ref, wu_ref, wd_ref, acc, width, chunk, lead=()):
    for c in range(width // chunk):
        cs = slice(c * chunk, (c + 1) * chunk)
        a = _dot(h, wg_ref[lead + (slice(None), cs)])
        u = _dot(h, wu_ref[lead + (slice(None), cs)])
        acc = acc + _dot((a * _sigmoid(a) * u).astype(BF16), wd_ref[lead + (cs, slice(None))])
    return acc


def _ffn_kernel(x_ref, g_ref, wg_ref, wu_ref, wd_ref, o_ref):
    x = x_ref[...]
    h = (_rms(x) * g_ref[...]).astype(BF16)
    o_ref[...] = _swiglu_chunks(h, wg_ref, wu_ref, wd_ref, x, D_FF_PAD, D_FF_PAD // 2)


def _resident(shape, index_map):
    return pl.BlockSpec(shape, index_map, pipeline_mode=pl.Buffered(1))


def _ffn(x2, g, wg, wu, wd):
    T = x2.shape[0]
    tm = 512
    pad = D_FF_PAD - D_FF
    wg_p = jnp.pad(wg, ((0, 0), (0, pad))).astype(BF16)
    wu_p = jnp.pad(wu, ((0, 0), (0, pad))).astype(BF16)
    wd_p = jnp.pad(wd, ((0, pad), (0, 0))).astype(BF16)
    return pl.pallas_call(
        _ffn_kernel,
        grid=(T // tm,),
        in_specs=[pl.BlockSpec((tm, D_MODEL), lambda i: (i, 0)),
                  pl.BlockSpec((1, D_MODEL), lambda i: (0, 0)),
                  _resident((D_MODEL, D_FF_PAD), lambda i: (0, 0)),
                  _resident((D_MODEL, D_FF_PAD), lambda i: (0, 0)),
                  _resident((D_FF_PAD, D_MODEL), lambda i: (0, 0))],
        out_specs=pl.BlockSpec((tm, D_MODEL), lambda i: (i, 0)),
        out_shape=jax.ShapeDtypeStruct((T, D_MODEL), F32),
        compiler_params=_cparams(("parallel",)),
    )(x2, g.reshape(1, D_MODEL), wg_p, wu_p, wd_p)


MOE_TILE = 512
DSP_TILE = 128
CMB_TILE = 128


def _router_kernel(x_ref, g_ref, w_ref, b_ref, r_ref):
    h = _rms(x_ref[...]) * g_ref[...]
    lane = _iota((h.shape[0], LANE), 1)
    logits = jnp.where(lane < N_EXPERTS, _dot3(h, w_ref[...]) + b_ref[...], -jnp.inf)
    m1 = jnp.max(logits, axis=-1, keepdims=True)
    i1 = jnp.min(jnp.where(logits == m1, lane, LANE), axis=-1, keepdims=True)
    rest = jnp.where(lane == i1, -jnp.inf, logits)
    m2 = jnp.max(rest, axis=-1, keepdims=True)
    i2 = jnp.min(jnp.where(rest == m2, lane, LANE), axis=-1, keepdims=True)
    e2 = jnp.exp(m2 - m1)
    w1 = 1.0 / (1.0 + e2)
    w2 = e2 / (1.0 + e2)
    r_ref[...] = jnp.where(lane == 0, i1.astype(F32),
                           jnp.where(lane == 1, i2.astype(F32),
                                     jnp.where(lane == 2, w1, jnp.where(lane == 3, w2, 0.0))))


def _router(x2, g, rw, rb):
    T = x2.shape[0]
    tm = 512
    rw_p = jnp.pad(rw, ((0, 0), (0, LANE - N_EXPERTS)))
    rb_p = jnp.pad(rb, (0, LANE - N_EXPERTS)).reshape(1, LANE)
    return pl.pallas_call(
        _router_kernel,
        grid=(T // tm,),
        in_specs=[pl.BlockSpec((tm, D_MODEL), lambda i: (i, 0)),
                  pl.BlockSpec((1, D_MODEL), lambda i: (0, 0)),
                  pl.BlockSpec((D_MODEL, LANE), lambda i: (0, 0)),
                  pl.BlockSpec((1, LANE), lambda i: (0, 0))],
        out_specs=pl.BlockSpec((tm, LANE), lambda i: (i, 0)),
        out_shape=jax.ShapeDtypeStruct((T, LANE), F32),
        compiler_params=_cparams(("parallel",)),
    )(x2, g.reshape(1, D_MODEL), rw_p, rb_p)


def _smem_rows(width, nt, shift):
    return pl.BlockSpec((1, 1, width), lambda i, *_: (jnp.clip(i + shift, 0, nt - 1), 0, 0),
                        memory_space=pltpu.SMEM)


def _scatter_copy(buf, slot, r, dst_hbm, row, sem, k):
    return pltpu.make_async_copy(buf.at[slot, pl.ds(r, 1), :], dst_hbm.at[pl.ds(row, 1), :], sem.at[slot, k, r])


def _dispatch_kernel(p1_ref, p2_ref, q1_ref, q2_ref, x_ref, g_ref, xs_in, xs_hbm, hbuf, sem):
    del xs_in
    i = pl.program_id(0)
    nt = pl.num_programs(0)
    slot = i % 2
    hbuf[slot] = _rms(x_ref[...]) * g_ref[...]
    for r in range(DSP_TILE):
        _scatter_copy(hbuf, slot, r, xs_hbm, p1_ref[0, 0, r], sem, 0).start()
        _scatter_copy(hbuf, slot, r, xs_hbm, p2_ref[0, 0, r], sem, 1).start()

    def wait_all(pa, pb, s):
        for r in range(DSP_TILE):
            _scatter_copy(hbuf, s, r, xs_hbm, pa[0, 0, r], sem, 0).wait()
            _scatter_copy(hbuf, s, r, xs_hbm, pb[0, 0, r], sem, 1).wait()

    @pl.when(i > 0)
    def _():
        wait_all(q1_ref, q2_ref, 1 - slot)

    @pl.when(i == nt - 1)
    def _():
        wait_all(p1_ref, p2_ref, slot)


def _moe_dispatch(x2, g, pos, n_rows):
    T = x2.shape[0]
    nt = T // DSP_TILE
    p1 = pos[:, 0].reshape(nt, 1, DSP_TILE)
    p2 = pos[:, 1].reshape(nt, 1, DSP_TILE)
    return pl.pallas_call(
        _dispatch_kernel,
        grid=(nt,),
        in_specs=[_smem_rows(DSP_TILE, nt, 0), _smem_rows(DSP_TILE, nt, 0),
                  _smem_rows(DSP_TILE, nt, -1), _smem_rows(DSP_TILE, nt, -1),
                  pl.BlockSpec((DSP_TILE, D_MODEL), lambda i: (i, 0)),
                  pl.BlockSpec((1, D_MODEL), lambda i: (0, 0)),
                  pl.BlockSpec(memory_space=pl.ANY)],
        out_specs=pl.BlockSpec(memory_space=pl.ANY),
        out_shape=jax.ShapeDtypeStruct((n_rows, D_MODEL), F32),
        scratch_shapes=[pltpu.VMEM((2, DSP_TILE, D_MODEL), F32), pltpu.SemaphoreType.DMA((2, 2, DSP_TILE))],
        input_output_aliases={6: 0},
        compiler_params=_cparams(("arbitrary",)),
    )(p1, p2, p1, p2, x2, g.reshape(1, D_MODEL), jnp.zeros((n_rows, D_MODEL), F32))


def _moe_kernel(texp_ref, x_ref, wg_ref, wu_ref, wd_ref, y_ref):
    y_ref[...] = _swiglu_chunks(x_ref[...].astype(BF16), wg_ref, wu_ref, wd_ref,
                                jnp.zeros((MOE_TILE, D_MODEL), F32), D_FF_EXPERT, D_FF_EXPERT // 4, lead=(0,))


def _moe_experts(xs, tile_expert, wg, wu, wd):
    nt = tile_expert.shape[0]
    grid_spec = pltpu.PrefetchScalarGridSpec(
        num_scalar_prefetch=1,
        grid=(nt,),
        in_specs=[
            pl.BlockSpec((MOE_TILE, D_MODEL), lambda i, te: (i, 0)),
            _resident((1, D_MODEL, D_FF_EXPERT), lambda i, te: (te[i], 0, 0)),
            _resident((1, D_MODEL, D_FF_EXPERT), lambda i, te: (te[i], 0, 0)),
            _resident((1, D_FF_EXPERT, D_MODEL), lambda i, te: (te[i], 0, 0)),
        ],
        out_specs=pl.BlockSpec((MOE_TILE, D_MODEL), lambda i, te: (i, 0)),
    )
    return pl.pallas_call(
        _moe_kernel,
        grid_spec=grid_spec,
        out_shape=jax.ShapeDtypeStruct((nt * MOE_TILE, D_MODEL), F32),
        compiler_params=_cparams(("arbitrary",)),
    )(tile_expert, xs, wg.astype(BF16), wu.astype(BF16), wd.astype(BF16))


def _gather_copy(src_hbm, row, buf, slot, k, r, sem):
    return pltpu.make_async_copy(src_hbm.at[pl.ds(row, 1), :], buf.at[slot, k, pl.ds(r, 1), :], sem.at[slot, k, r])


def _combine_kernel(p1_ref, p2_ref, n1_ref, n2_ref, y_hbm, x_ref, r_ref, o_ref, ybuf, sem):
    i = pl.program_id(0)
    nt = pl.num_programs(0)
    slot = i % 2

    def issue(pa, pb, s):
        for r in range(CMB_TILE):
            _gather_copy(y_hbm, pa[0, 0, r], ybuf, s, 0, r, sem).start()
            _gather_copy(y_hbm, pb[0, 0, r], ybuf, s, 1, r, sem).start()

    @pl.when(i == 0)
    def _():
        issue(p1_ref, p2_ref, 0)

    @pl.when(i + 1 < nt)
    def _():
        issue(n1_ref, n2_ref, 1 - slot)

    for r in range(CMB_TILE):
        _gather_copy(y_hbm, p1_ref[0, 0, r], ybuf, slot, 0, r, sem).wait()
        _gather_copy(y_hbm, p2_ref[0, 0, r], ybuf, slot, 1, r, sem).wait()
    r = r_ref[...]
    o_ref[...] = x_ref[...] + r[:, 2:3] * ybuf[slot, 0] + r[:, 3:4] * ybuf[slot, 1]


def _moe_combine(y_sorted, pos, x2, route):
    T = x2.shape[0]
    nt = T // CMB_TILE
    p1 = pos[:, 0].reshape(nt, 1, CMB_TILE)
    p2 = pos[:, 1].reshape(nt, 1, CMB_TILE)
    return pl.pallas_call(
        _combine_kernel,
        grid=(nt,),
        in_specs=[_smem_rows(CMB_TILE, nt, 0), _smem_rows(CMB_TILE, nt, 0),
                  _smem_rows(CMB_TILE, nt, 1), _smem_rows(CMB_TILE, nt, 1),
                  pl.BlockSpec(memory_space=pl.ANY),
                  pl.BlockSpec((CMB_TILE, D_MODEL), lambda i: (i, 0)),
                  pl.BlockSpec((CMB_TILE, LANE), lambda i: (i, 0))],
        out_specs=pl.BlockSpec((CMB_TILE, D_MODEL), lambda i: (i, 0)),
        out_shape=jax.ShapeDtypeStruct((T, D_MODEL), F32),
        scratch_shapes=[pltpu.VMEM((2, 2, CMB_TILE, D_MODEL), F32),
                        pltpu.SemaphoreType.DMA((2, 2, CMB_TILE))],
        compiler_params=_cparams(("arbitrary",)),
    )(p1, p2, p1, p2, y_sorted, x2, route)


def _moe(x2, g, rw, rb, wg, wu, wd):
    T = x2.shape[0]
    route = _router(x2, g, rw, rb)
    experts = route[:, :2].astype(jnp.int32)
    onehot = (experts.reshape(-1)[:, None] == jnp.arange(N_EXPERTS)[None, :]).astype(jnp.int32)
    csum = jnp.cumsum(onehot, axis=0)
    counts = csum[-1]
    rank = jnp.sum(csum * onehot, axis=1) - 1
    padded = ((counts + MOE_TILE - 1) // MOE_TILE) * MOE_TILE
    ends = jnp.cumsum(padded)
    starts = ends - padded
    pos = (jnp.sum(starts[None, :] * onehot, axis=1) + rank).reshape(T, 2)
    nt = (2 * T) // MOE_TILE + N_EXPERTS
    tile_start = jnp.arange(nt, dtype=jnp.int32) * MOE_TILE
    tile_expert = jnp.minimum(jnp.sum(tile_start[:, None] >= ends[None, :], axis=1), N_EXPERTS - 1)
    xs = _moe_dispatch(x2, g, pos, nt * MOE_TILE)
    y_sorted = _moe_experts(xs, tile_expert.astype(jnp.int32), wg, wu, wd)
    return _moe_combine(y_sorted, pos, x2, route)


def _layer_mixers(x2, B, S, l, p):
    proj = _inproj(x2, p["norm1_g"][l], _pad_w_in(p["w_in"][l]), _head_gains(p["dil_qk_g"][l], p["nsa_qk_g"][l]))
    proj3 = proj.reshape(B, S, PROJ_PAD)
    y_dil = _dil_attention(proj3).reshape(B * S, GROUP_WIDTH)
    kc, vc, ks, vs, kw, vw = _nsa_prep(proj3, p["nsa_cmp_pos"][l], p["nsa_cmp_w1"][l], p["nsa_cmp_w2"][l],
                                       p["nsa_qk_g"][l])
    y_nsa = _nsa_attention(proj3, kc, vc, ks, vs, kw, vw).reshape(B * S, GROUP_WIDTH)
    y_gla = _gla(proj3, p["gla_wa2"][l], p["gla_ba"][l], p["gla_norm_g"][l]).reshape(B * S, GROUP_WIDTH)
    y_s5 = _s5_linear(proj3[:, :, COL_SU:COL_SU + GROUP_WIDTH], p["s5_a_re"][l], p["s5_a_im"][l],
                      p["s5_b_re"][l], p["s5_b_im"][l], p["s5_c_re"][l], p["s5_c_im"][l],
                      p["s5_log_dt"][l]).reshape(B * S, GROUP_WIDTH)
    return _mixout(y_dil, y_nsa, y_gla, y_s5, proj, x2, p["s5_d"][l], p["s5_glu_w"][l],
                   p["s5_glu_b"][l], p["out_norm_g"][l], p["w_out"][l])


def kernel(x, norm1_g, w_in, dil_qk_g, nsa_qk_g, nsa_cmp_pos, nsa_cmp_w1, nsa_cmp_w2, gla_wa2, gla_ba,
           gla_norm_g, s5_a_re, s5_a_im, s5_b_re, s5_b_im, s5_c_re, s5_c_im, s5_d, s5_log_dt, s5_glu_w,
           s5_glu_b, out_norm_g, w_out, norm2_g, ffn_w_gate, ffn_w_up, ffn_w_down, moe_router_w,
           moe_router_b, moe_w_gate, moe_w_up, moe_w_down):
    B, S, D = x.shape
    p = dict(norm1_g=norm1_g, w_in=w_in, dil_qk_g=dil_qk_g, nsa_qk_g=nsa_qk_g, nsa_cmp_pos=nsa_cmp_pos,
             nsa_cmp_w1=nsa_cmp_w1, nsa_cmp_w2=nsa_cmp_w2, gla_wa2=gla_wa2, gla_ba=gla_ba,
             gla_norm_g=gla_norm_g, s5_a_re=s5_a_re, s5_a_im=s5_a_im, s5_b_re=s5_b_re, s5_b_im=s5_b_im,
             s5_c_re=s5_c_re, s5_c_im=s5_c_im, s5_d=s5_d, s5_log_dt=s5_log_dt, s5_glu_w=s5_glu_w,
             s5_glu_b=s5_glu_b, out_norm_g=out_norm_g, w_out=w_out)
    x2 = x.reshape(B * S, D)
    depth = norm1_g.shape[0]
    for l in range(depth):
        x2 = _layer_mixers(x2, B, S, l, p)
        i = l // 2
        if l % 2 == 0:
            x2 = _ffn(x2, norm2_g[l], ffn_w_gate[i], ffn_w_up[i], ffn_w_down[i])
        else:
            x2 = _moe(x2, norm2_g[l], moe_router_w[i], moe_router_b[i], moe_w_gate[i], moe_w_up[i],
                      moe_w_down[i])
    return x2.reshape(B, S, D)
```

```python
import functools
import math

import jax
import jax.numpy as jnp
from jax import lax
from jax.experimental import pallas as pl
from jax.experimental.pallas import tpu as pltpu

F32 = jnp.float32
BF16 = jnp.bfloat16

D_MODEL = 1024
HEAD_DIM = 64
GROUP_WIDTH = 256
N_HEADS = 4
DIL_PATTERNS = ((128, 1), (512, 4), (2048, 16))
Q_BLOCK = 128
NSA_CMP_LEN = 32
NSA_CMP_STRIDE = 16
NSA_CMP_HIDDEN = 256
NSA_SEL_LEN = 64
NSA_TOPN = 16
NSA_WIN = 512
FORCE_BONUS = 1.0e4
GLA_DK = 32
GLA_DV = 64
GLA_RANK = 16
GLA_TAU = 16.0
GLA_CHUNK = 16
S5_GROUP = 16
S5_GROUPS = 16
S5_STATE = 64
S5_CHUNK = 16
D_FF = 2752
N_EXPERTS = 8
D_FF_EXPERT = 3584
EPS = 1e-6
NEG_INF = -1e30

DIL_SLOPES = tuple(2.0 ** (-float(i)) for i in (2, 4, 6, 8))
NSA_SLOPES = tuple(2.0 ** (-float(i)) for i in (1, 3, 5, 7))

COL_DQ, COL_DK, COL_DV, COL_NQ = 0, 256, 512, 768
COL_NCMP, COL_NK, COL_NV, COL_NG = 1024, 1152, 1280, 1408
COL_GQ, COL_GK, COL_GV, COL_GA, COL_GR, COL_SU = 1536, 1664, 1792, 2048, 2304, 2560
PROJ_PAD = 2816
LANE = 128
D_FF_PAD = 2816

VMEM_LIMIT = 56 * 1024 * 1024


def _cparams(sem):
    return pltpu.CompilerParams(dimension_semantics=sem, vmem_limit_bytes=VMEM_LIMIT)


def _rms(x):
    return x * lax.rsqrt(jnp.mean(x * x, axis=-1, keepdims=True) + EPS)


def _dot(a, b):
    return jnp.dot(a, b, preferred_element_type=F32)


def _dot_nt(a, b):
    return lax.dot_general(a, b, (((1,), (1,)), ((), ())), preferred_element_type=F32)


def _split(x):
    hi = x.astype(BF16)
    lo = (x - hi.astype(F32)).astype(BF16)
    return hi, lo


def _dot_lsplit(x, m):
    hi, lo = _split(x)
    return _dot(hi, m) + _dot(lo, m)


def _dot_rsplit(m, x):
    hi, lo = _split(x)
    return _dot(m, hi) + _dot(m, lo)


def _dot3(a, b):
    ah, al = _split(a)
    bh, bl = _split(b)
    return _dot(ah, bh) + _dot(ah, bl) + _dot(al, bh)


def _gelu(x):
    return 0.5 * x * (1.0 + jnp.tanh(math.sqrt(2.0 / math.pi) * (x + 0.044715 * (x * x * x))))


def _sigmoid(x):
    return 1.0 / (1.0 + jnp.exp(-x))


def _iota(shape, dim):
    return lax.broadcasted_iota(jnp.int32, shape, dim)


def _head_indicator(rows, cols, rdiv, cdiv):
    return jnp.where(_iota((rows, cols), 0) // rdiv == _iota((rows, cols), 1) // cdiv, 1.0, 0.0).astype(BF16)


def _inproj_kernel(x_ref, g_ref, w_ref, hg_ref, o_ref):
    h = _rms(x_ref[...]) * g_ref[...]
    o_ref[...] = _dot(h.astype(BF16), w_ref[...])
    ind = _head_indicator(GROUP_WIDTH, GROUP_WIDTH, HEAD_DIM, HEAD_DIM)
    for lo, width in NORM_SEGMENTS:
        v = o_ref[:, lo:lo + width]
        ms = _dot_lsplit(v * v, ind[:width, :width]) * (1.0 / HEAD_DIM)
        o_ref[:, lo:lo + width] = v * lax.rsqrt(ms + EPS) * hg_ref[:, lo:lo + width]


NORM_COLS = COL_NV
NORM_SEGMENTS = ((COL_DQ, GROUP_WIDTH), (COL_DK, GROUP_WIDTH), (COL_NQ, GROUP_WIDTH), (COL_NK, LANE))


def _head_gains(dil_g, nsa_g):
    scale = HEAD_DIM ** -0.5
    z = jnp.zeros((HEAD_DIM,), F32)
    row = jnp.concatenate([jnp.tile(dil_g[0] * scale, N_HEADS), jnp.tile(dil_g[1], N_HEADS),
                           jnp.zeros((GROUP_WIDTH,), F32), jnp.tile(nsa_g[0] * scale, N_HEADS),
                           z, z, nsa_g[2], nsa_g[3]])
    return row.reshape(1, NORM_COLS)


def _inproj(x2, g, w_pad, head_gains):
    T = x2.shape[0]
    tm = 512
    return pl.pallas_call(
        _inproj_kernel,
        grid=(T // tm,),
        in_specs=[
            pl.BlockSpec((tm, D_MODEL), lambda i: (i, 0)),
            pl.BlockSpec((1, D_MODEL), lambda i: (0, 0)),
            pl.BlockSpec((D_MODEL, PROJ_PAD), lambda i: (0, 0)),
            pl.BlockSpec((1, NORM_COLS), lambda i: (0, 0)),
        ],
        out_specs=pl.BlockSpec((tm, PROJ_PAD), lambda i: (i, 0)),
        out_shape=jax.ShapeDtypeStruct((T, PROJ_PAD), F32),
        compiler_params=_cparams(("parallel",)),
    )(x2, g.reshape(1, D_MODEL), w_pad, head_gains)


def _pad_w_in(w):
    def seg(lo, width, pad_to):
        s = w[:, lo:lo + width]
        if pad_to > width:
            s = jnp.pad(s, ((0, 0), (0, pad_to - width)))
        return s
    k_cmp, v_cmp = seg(1024, 64, 64), seg(1088, 64, 64)
    k_slc, v_slc = seg(1152, 64, 64), seg(1216, 64, 64)
    k_win, v_win = seg(1280, 64, 64), seg(1344, 64, 64)
    pieces = [
        seg(0, 256, 256), seg(256, 256, 256), seg(512, 256, 256), seg(768, 256, 256),
        k_cmp, v_cmp, k_slc, k_win, v_slc, v_win,
        seg(1408, 12, 128),
        seg(1420, 128, 128), seg(1548, 128, 128), seg(1676, 256, 256),
        seg(1932, 16, 256),
        seg(1948, 256, 256), seg(2204, 256, 256),
    ]
    out = jnp.concatenate(pieces, axis=1)
    assert out.shape[1] == PROJ_PAD
    return out.astype(BF16)


DIL_SB = Q_BLOCK * max(d for _, d in DIL_PATTERNS)
DIL_SUBSTEPS = DIL_SB // Q_BLOCK


def _dil_kernel(q0_ref, q1_ref, k0_ref, k1_ref, v0_ref, v1_ref, o_ref,
                ks_ref, vs_ref, bias_ref, m_ref, l_ref, acc_ref):
    i = pl.program_id(1)
    rows4 = N_HEADS * Q_BLOCK

    def ld2(ref, rows):
        return jnp.concatenate([ref[0, rows, :], ref[1, rows, :]], axis=1)

    def st2(ref, rows, val):
        ref[0, rows, :] = val[:, :LANE]
        ref[1, rows, :] = val[:, LANE:]

    @pl.when(i == 0)
    def _():
        ks_ref[:, 0:DIL_SB, :] = jnp.zeros((2, DIL_SB, LANE), F32)
        vs_ref[:, 0:DIL_SB, :] = jnp.zeros((2, DIL_SB, LANE), F32)
        ri = _iota((rows4, 2 * Q_BLOCK), 0)
        ci = _iota((rows4, 2 * Q_BLOCK), 1)
        delta = Q_BLOCK + ri % Q_BLOCK - ci
        head = ri // Q_BLOCK
        slope = jnp.where(head == 0, DIL_SLOPES[0],
                          jnp.where(head == 1, DIL_SLOPES[1],
                                    jnp.where(head == 2, DIL_SLOPES[2], DIL_SLOPES[3]))).astype(F32)
        band = (delta >= 0) & (delta <= Q_BLOCK)
        for p, (_, dil) in enumerate(DIL_PATTERNS):
            b = jnp.where(band, -(slope * dil) * delta.astype(F32), NEG_INF)
            bias_ref[p, 0] = b
            bias_ref[p, 1] = jnp.where(ci >= Q_BLOCK, b, NEG_INF)

    @pl.when(i > 0)
    def _():
        ks_ref[:, 0:DIL_SB, :] = ks_ref[:, DIL_SB:, :]
        vs_ref[:, 0:DIL_SB, :] = vs_ref[:, DIL_SB:, :]

    ks_ref[0, DIL_SB:, :] = k0_ref[0]
    ks_ref[1, DIL_SB:, :] = k1_ref[0]
    vs_ref[0, DIL_SB:, :] = v0_ref[0]
    vs_ref[1, DIL_SB:, :] = v1_ref[0]
    lane_head = _iota((Q_BLOCK, GROUP_WIDTH), 1) // HEAD_DIM

    def spread(x):
        out = jnp.zeros((Q_BLOCK, GROUP_WIDTH), F32)
        for h in range(N_HEADS):
            out = jnp.where(lane_head == h, x[h * Q_BLOCK:(h + 1) * Q_BLOCK], out)
        return out

    order = sorted(range(len(DIL_PATTERNS)), key=lambda n: -DIL_PATTERNS[n][1])
    assert DIL_PATTERNS[order[-1]][1] == 1
    for p in order:
        dil = DIL_PATTERNS[p][1]

        def sub(j, carry, p=p, dil=dil):
            start = (j // dil) * (Q_BLOCK * dil) + j % dil
            rows = pl.ds(start, Q_BLOCK, stride=dil)
            cur = pl.ds(DIL_SB + start, Q_BLOCK, stride=dil)
            prev = pl.ds(DIL_SB + start - Q_BLOCK * dil, Q_BLOCK, stride=dil)
            q = jnp.concatenate([q0_ref[0, rows, :], q1_ref[0, rows, :]], axis=1)
            q4 = jnp.concatenate([jnp.where(lane_head == h, q, 0.0) for h in range(N_HEADS)], axis=0)
            kk = jnp.concatenate([ld2(ks_ref, prev), ld2(ks_ref, cur)], axis=0).astype(BF16)
            vv = jnp.concatenate([ld2(vs_ref, prev), ld2(vs_ref, cur)], axis=0).astype(BF16)
            first = jnp.where((i == 0) & (j < dil), 1, 0)
            s = _dot_nt(q4.astype(BF16), kk) + bias_ref[p, first]
            m = jnp.max(s, axis=-1, keepdims=True)
            e = jnp.exp(s - m)
            l = jnp.sum(e, axis=-1, keepdims=True)
            o_t = spread(_dot(e.astype(BF16), vv))
            m_t = spread(m)
            l_t = spread(l)
            if p == order[0]:
                st2(m_ref, rows, m_t)
                st2(l_ref, rows, l_t)
                st2(acc_ref, rows, o_t)
                return carry
            m_old = ld2(m_ref, rows)
            m_new = jnp.maximum(m_old, m_t)
            a = jnp.exp(m_old - m_new)
            b = jnp.exp(m_t - m_new)
            l_new = a * ld2(l_ref, rows) + b * l_t
            acc_new = a * ld2(acc_ref, rows) + b * o_t
            if p == order[-1]:
                o_ref[0, pl.ds(pl.multiple_of(start, Q_BLOCK), Q_BLOCK), :] = acc_new / l_new
            else:
                st2(m_ref, rows, m_new)
                st2(l_ref, rows, l_new)
                st2(acc_ref, rows, acc_new)
            return carry
        lax.fori_loop(0, DIL_SUBSTEPS, sub, 0)


def _dil_attention(proj3):
    B, S, _ = proj3.shape
    assert all(win // d == Q_BLOCK for win, d in DIL_PATTERNS) and S % DIL_SB == 0

    def halves(c):
        return [pl.BlockSpec((1, DIL_SB, LANE), lambda b, i, c=c, h=h: (b, i, c // LANE + h)) for h in range(2)]

    sb = pltpu.VMEM((2, DIL_SB, LANE), F32)
    sb2 = pltpu.VMEM((2, 2 * DIL_SB, LANE), F32)
    return pl.pallas_call(
        _dil_kernel,
        grid=(B, S // DIL_SB),
        in_specs=halves(COL_DQ) + halves(COL_DK) + halves(COL_DV),
        out_specs=pl.BlockSpec((1, DIL_SB, GROUP_WIDTH), lambda b, i: (b, i, 0)),
        out_shape=jax.ShapeDtypeStruct((B, S, GROUP_WIDTH), F32),
        scratch_shapes=[sb2, sb2, pltpu.VMEM((len(DIL_PATTERNS), 2, N_HEADS * Q_BLOCK, 2 * Q_BLOCK), F32),
                        sb, sb, sb],
        compiler_params=_cparams(("parallel", "arbitrary")),
    )(*([proj3] * 6))


def _nsa_prep_kernel(xk_ref, xv_ref, kraw_ref, vraw_ref, pos_ref, w1_ref, w2_ref, g_ref,
                     kc_ref, vc_ref, ks_ref, vs_ref, kw_ref, vw_ref):
    half = NSA_CMP_STRIDE * HEAD_DIM

    def compress(x, idx):
        w1 = w1_ref[idx]
        xb = x.astype(BF16)
        a = _dot(xb, w1[:half, :])
        b = _dot(xb, w1[half:, :])
        bias = _dot(pos_ref[idx].astype(BF16), w1)[0:1, :]
        pre = a + pltpu.roll(b, b.shape[0] - 1, 0) + bias
        return _dot(_gelu(pre).astype(BF16), w2_ref[idx])

    def with_pos(k, pos):
        lane = _iota((k.shape[0], HEAD_DIM), 1)
        extra = jnp.where(lane == 0, (pos // 256).astype(F32), jnp.where(lane == 1, (pos % 256).astype(F32), 0.0))
        return jnp.concatenate([k, extra], axis=1).astype(BF16)

    def with_one_t(v):
        lane = _iota((v.shape[0], HEAD_DIM), 1)
        return jnp.concatenate([v, jnp.where(lane == 0, 1.0, 0.0)], axis=1).T.astype(BF16)

    ng = xk_ref.shape[1]
    cend = _iota((ng, 1), 0) * NSA_CMP_STRIDE + (NSA_CMP_LEN - 1)
    kc_ref[0] = with_pos(_rms(compress(xk_ref[0], 0)) * g_ref[1:2, :], cend)
    vc_ref[0] = with_one_t(compress(xv_ref[0], 1))
    kraw = kraw_ref[0]
    tpos = _iota((kraw.shape[0], 1), 0)
    ks_ref[0] = with_pos(kraw[:, :HEAD_DIM], tpos)
    kw_ref[0] = with_pos(kraw[:, HEAD_DIM:], tpos)
    for c in range(vs_ref.shape[1]):
        vs_ref[0, c] = with_one_t(vraw_ref[0, c * NSA_KCHUNK:(c + 1) * NSA_KCHUNK, :HEAD_DIM])
    for c in range(vw_ref.shape[1]):
        vw_ref[0, c] = with_one_t(vraw_ref[0, c * Q_BLOCK:(c + 1) * Q_BLOCK, HEAD_DIM:])


NSA_KCHUNK = 2 * Q_BLOCK


def _nsa_prep(proj3, pos, w1, w2, g):
    B, S, _ = proj3.shape
    ng = S // NSA_CMP_STRIDE
    gw = NSA_CMP_STRIDE * HEAD_DIM
    xk = proj3[:, :, COL_NCMP:COL_NCMP + HEAD_DIM].reshape(B, ng, gw)
    xv = proj3[:, :, COL_NCMP + HEAD_DIM:COL_NCMP + 2 * HEAD_DIM].reshape(B, ng, gw)
    pos8 = jnp.broadcast_to(pos.reshape(2, 1, NSA_CMP_LEN * HEAD_DIM), (2, 8, NSA_CMP_LEN * HEAD_DIM))
    small = jax.ShapeDtypeStruct((B, ng, LANE), BF16)
    small_t = jax.ShapeDtypeStruct((B, LANE, ng), BF16)
    big = jax.ShapeDtypeStruct((B, S, LANE), BF16)
    vs_t = jax.ShapeDtypeStruct((B, S // NSA_KCHUNK, LANE, NSA_KCHUNK), BF16)
    vw_t = jax.ShapeDtypeStruct((B, S // Q_BLOCK, LANE, Q_BLOCK), BF16)
    small_spec = pl.BlockSpec((1, ng, LANE), lambda b: (b, 0, 0))
    small_t_spec = pl.BlockSpec((1, LANE, ng), lambda b: (b, 0, 0))
    big_spec = pl.BlockSpec((1, S, LANE), lambda b: (b, 0, 0))
    vs_spec = pl.BlockSpec((1, S // NSA_KCHUNK, LANE, NSA_KCHUNK), lambda b: (b, 0, 0, 0))
    vw_spec = pl.BlockSpec((1, S // Q_BLOCK, LANE, Q_BLOCK), lambda b: (b, 0, 0, 0))
    return pl.pallas_call(
        _nsa_prep_kernel,
        grid=(B,),
        in_specs=[
            pl.BlockSpec((1, ng, gw), lambda b: (b, 0, 0)),
            pl.BlockSpec((1, ng, gw), lambda b: (b, 0, 0)),
            pl.BlockSpec((1, S, LANE), lambda b: (b, 0, COL_NK // LANE)),
            pl.BlockSpec((1, S, LANE), lambda b: (b, 0, COL_NV // LANE)),
            pl.BlockSpec((2, 8, NSA_CMP_LEN * HEAD_DIM), lambda b: (0, 0, 0)),
            pl.BlockSpec((2, NSA_CMP_LEN * HEAD_DIM, NSA_CMP_HIDDEN), lambda b: (0, 0, 0)),
            pl.BlockSpec((2, NSA_CMP_HIDDEN, HEAD_DIM), lambda b: (0, 0, 0)),
            pl.BlockSpec((4, HEAD_DIM), lambda b: (0, 0)),
        ],
        out_specs=[small_spec, small_t_spec, big_spec, vs_spec, big_spec, vw_spec],
        out_shape=[small, small_t, big, vs_t, big, vw_t],
        compiler_params=_cparams(("parallel",)),
    )(xk, xv, proj3, proj3, pos8, w1.astype(BF16), w2.astype(BF16), g)


def _nsa_kernel(q_ref, gate_ref, kc_ref, vc_ref, ks_ref, vs_ref, kw_ref, vw_ref, o_ref, sel_ref, *, seq):
    i = pl.program_id(1)
    t0 = i * Q_BLOCK
    q_t = q_ref[0].T
    sub = _iota((HEAD_DIM, Q_BLOCK), 0)
    q = jnp.concatenate(
        [jnp.concatenate([q_t[h * HEAD_DIM:(h + 1) * HEAD_DIM],
                          jnp.where(sub == 0, NSA_SLOPES[h] * 256.0, jnp.where(sub == 1, NSA_SLOPES[h], 0.0))],
                         axis=0) for h in range(N_HEADS)], axis=1).astype(BF16)
    tq = t0 + _iota((1, Q_BLOCK), 1)

    def masked(s, ok):
        return jnp.concatenate([jnp.where(ok, s[:, h * Q_BLOCK:(h + 1) * Q_BLOCK], NEG_INF)
                                for h in range(N_HEADS)], axis=1)

    def heads(x):
        return jnp.concatenate([x] * N_HEADS, axis=1)

    ncmp = kc_ref.shape[1]
    cend = _iota((ncmp, 1), 0) * NSA_CMP_STRIDE + (NSA_CMP_LEN - 1)
    s = masked(_dot(kc_ref[0], q), cend <= tq)
    e = jnp.exp(s - jnp.max(s, axis=0, keepdims=True))
    col_ok = heads(jnp.where(tq >= NSA_CMP_LEN - 1, 1.0, 0.0))
    p = e * (col_ok / jnp.sum(e, axis=0, keepdims=True))
    o_cmp = _dot(vc_ref[0], p.astype(BF16))[:HEAD_DIM]

    nsel = seq // NSA_SEL_LEN
    psum = p[:, :Q_BLOCK] + p[:, Q_BLOCK:2 * Q_BLOCK] + p[:, 2 * Q_BLOCK:3 * Q_BLOCK] + p[:, 3 * Q_BLOCK:]
    jj = _iota((nsel, ncmp), 0)
    cc = _iota((nsel, ncmp), 1)
    ratio = NSA_SEL_LEN // NSA_CMP_STRIDE
    cover_t = jnp.where((cc < ratio * jj + ratio) & (cc * NSA_CMP_STRIDE + NSA_CMP_LEN > jj * NSA_SEL_LEN),
                        1.0, 0.0).astype(BF16)
    imp = _dot_rsplit(cover_t, psum)
    jrow = _iota((nsel, Q_BLOCK), 0)
    cur = (t0 + _iota((nsel, Q_BLOCK), 1)) // NSA_SEL_LEN
    forced = (jrow == 0) | (jrow == cur) | (jrow == cur - 1)
    impv = jnp.where(jrow <= cur, imp + jnp.where(forced, FORCE_BONUS, 0.0), NEG_INF)
    rank = jnp.zeros((nsel, Q_BLOCK), F32)
    for ii in range(nsel):
        row = impv[ii:ii + 1, :]
        tie = jnp.where(jrow > ii, 1.0, 0.0)
        rank = rank + jnp.where(row > impv, 1.0, jnp.where(row == impv, tie, 0.0))
    sel_ref[...] = jnp.where((rank < min(NSA_TOPN, nsel)) & (jrow <= cur), 1.0, 0.0)

    kpos_col = _iota((NSA_KCHUNK, 1), 0)
    per_chunk = NSA_KCHUNK // NSA_SEL_LEN

    def raw_scores(c):
        return _dot(ks_ref[0, pl.ds(pl.multiple_of(c * NSA_KCHUNK, NSA_KCHUNK), NSA_KCHUNK), :], q)

    def chunk_mask(c):
        picked = jnp.concatenate(
            [jnp.broadcast_to(sel_ref[pl.ds(c * per_chunk + b, 1), :], (NSA_SEL_LEN, Q_BLOCK))
             for b in range(per_chunk)], axis=0)
        return (picked > 0.5) & (c * NSA_KCHUNK + kpos_col <= tq)

    n_steps = (t0 + Q_BLOCK - 1) // (2 * NSA_KCHUNK) + 1

    def sel_body(it, carry):
        m_i, acc = carry
        sa = masked(raw_scores(2 * it), chunk_mask(2 * it))
        sb = masked(raw_scores(2 * it + 1), chunk_mask(2 * it + 1))
        m_new = jnp.maximum(m_i, jnp.maximum(jnp.max(sa, axis=0, keepdims=True), jnp.max(sb, axis=0, keepdims=True)))
        pa = jnp.exp(sa - m_new).astype(BF16)
        pb = jnp.exp(sb - m_new).astype(BF16)
        return m_new, jnp.exp(m_i - m_new) * acc + _dot(vs_ref[0, 2 * it], pa) + _dot(vs_ref[0, 2 * it + 1], pb)

    cols = N_HEADS * Q_BLOCK
    init = (jnp.full((1, cols), NEG_INF, F32), jnp.zeros((LANE, cols), F32))
    _, acc_s = lax.fori_loop(0, n_steps, sel_body, init)
    o_sel = acc_s[:HEAD_DIM] / acc_s[HEAD_DIM:HEAD_DIM + 1]

    nwb = NSA_WIN // Q_BLOCK + 1
    span = nwb * Q_BLOCK
    b0 = jnp.maximum(i - NSA_WIN // Q_BLOCK, 0)
    start = pl.multiple_of(b0 * Q_BLOCK, Q_BLOCK)
    dw = tq - (start + _iota((span, 1), 0))
    sw = masked(_dot(kw_ref[0, pl.ds(start, span), :], q), (dw >= 0) & (dw < NSA_WIN))
    pw = jnp.exp(sw - jnp.max(sw, axis=0, keepdims=True)).astype(BF16)
    rw = _dot(vw_ref[0, b0], pw[:Q_BLOCK])
    for j in range(1, nwb):
        rw = rw + _dot(vw_ref[0, b0 + j], pw[j * Q_BLOCK:(j + 1) * Q_BLOCK])
    o_win = rw[:HEAD_DIM] / rw[HEAD_DIM:HEAD_DIM + 1]

    gate = _sigmoid(gate_ref[0].T)
    outs = []
    for h in range(N_HEADS):
        c = slice(h * Q_BLOCK, (h + 1) * Q_BLOCK)
        outs.append(gate[3 * h:3 * h + 1] * o_cmp[:, c] + gate[3 * h + 1:3 * h + 2] * o_sel[:, c]
                    + gate[3 * h + 2:3 * h + 3] * o_win[:, c])
    o_ref[0] = jnp.concatenate(outs, axis=0).T


def _nsa_attention(proj3, kc, vc, ks, vs, kw, vw):
    B, S, _ = proj3.shape
    nb = S // Q_BLOCK
    ng = kc.shape[1]
    assert S >= NSA_WIN + Q_BLOCK and S % (2 * NSA_KCHUNK) == 0
    small_spec = pl.BlockSpec((1, ng, LANE), lambda b, i: (b, 0, 0))
    small_t_spec = pl.BlockSpec((1, LANE, ng), lambda b, i: (b, 0, 0))
    big_spec = pl.BlockSpec((1, S, LANE), lambda b, i: (b, 0, 0))
    vs_spec = pl.BlockSpec((1, S // NSA_KCHUNK, LANE, NSA_KCHUNK), lambda b, i: (b, 0, 0, 0))
    vw_spec = pl.BlockSpec((1, S // Q_BLOCK, LANE, Q_BLOCK), lambda b, i: (b, 0, 0, 0))
    return pl.pallas_call(
        functools.partial(_nsa_kernel, seq=S),
        grid=(B, nb),
        in_specs=[
            pl.BlockSpec((1, Q_BLOCK, GROUP_WIDTH), lambda b, i: (b, i, COL_NQ // GROUP_WIDTH)),
            pl.BlockSpec((1, Q_BLOCK, LANE), lambda b, i: (b, i, COL_NG // LANE)),
            small_spec, small_t_spec, big_spec, vs_spec, big_spec, vw_spec,
        ],
        out_specs=pl.BlockSpec((1, Q_BLOCK, GROUP_WIDTH), lambda b, i: (b, i, 0)),
        out_shape=jax.ShapeDtypeStruct((B, S, GROUP_WIDTH), F32),
        scratch_shapes=[pltpu.VMEM((S // NSA_SEL_LEN, Q_BLOCK), F32)],
        compiler_params=_cparams(("parallel", "parallel")),
    )(proj3, proj3, kc, vc, ks, vs, kw, vw)


GLA_BLOCK = 128
GLA_W = N_HEADS * GLA_DK


def _gla_kernel(q_ref, k_ref, v_ref, a_ref, r_ref, wa_ref, ba_ref, g_ref, o_ref, st_ref):
    nsub = GLA_BLOCK // GLA_CHUNK

    @pl.when(pl.program_id(1) == 0)
    def _():
        st_ref[...] = jnp.zeros_like(st_ref)

    q = q_ref[0] * (GLA_DK ** -0.5)
    k = k_ref[0]
    v = v_ref[0]
    z = _dot3(a_ref[0], wa_ref[...]) + ba_ref[...]
    log_a = (jnp.minimum(z, 0.0) - jnp.log(1.0 + jnp.exp(-jnp.abs(z)))) * (1.0 / GLA_TAU)
    ri = _iota((GLA_BLOCK, GLA_BLOCK), 0)
    ci = _iota((GLA_BLOCK, GLA_BLOCK), 1)
    same = (ri // GLA_CHUNK) == (ci // GLA_CHUNK)
    b_loc = _dot_rsplit(jnp.where(same & (ci <= ri), 1.0, 0.0).astype(BF16), log_a)
    b_tot = _dot_rsplit(jnp.where(same, 1.0, 0.0).astype(BF16), log_a)
    qd = q * jnp.exp(b_loc)
    kd = k * jnp.exp(b_tot - b_loc)

    ind_kv = jnp.where(_iota((GLA_W, GROUP_WIDTH), 0) // GLA_DK == _iota((GLA_W, GROUP_WIDTH), 1) // GLA_DV,
                       1.0, 0.0).astype(BF16)
    rsub = _iota((GLA_BLOCK, 1), 0) % GLA_CHUNK
    o = jnp.zeros((GLA_BLOCK, GROUP_WIDTH), F32)
    for s in range(GLA_CHUNK):
        if s == 0:
            ks_, bs_, vs_ = k, b_loc, v
        else:
            ks_, bs_, vs_ = pltpu.roll(k, s, 0), pltpu.roll(b_loc, s, 0), pltpu.roll(v, s, 0)
        dec = jnp.where(rsub >= s, jnp.exp(jnp.minimum(b_loc - bs_, 0.0)), 0.0)
        w = _dot((q * ks_ * dec).astype(BF16), ind_kv)
        o = o + w * vs_

    v_t = v.T.astype(BF16)
    head_ok = _iota((GROUP_WIDTH, GLA_W), 0) // GLA_DV == _iota((GROUP_WIDTH, GLA_W), 1) // GLA_DK
    rblk = _iota((GLA_BLOCK, 1), 0) // GLA_CHUNK
    st = st_ref[...]
    for c in range(nsub):
        in_c = rblk == c
        o = o + _dot_nt(jnp.where(in_c, qd, 0.0).astype(BF16), st.astype(BF16))
        decay = jnp.exp(b_tot[c * GLA_CHUNK:c * GLA_CHUNK + 1, :])
        upd = _dot(v_t, jnp.where(in_c, kd, 0.0).astype(BF16))
        st = st * decay + jnp.where(head_ok, upd, 0.0)
    st_ref[...] = st

    ind_vv = jnp.where(_iota((GROUP_WIDTH, GROUP_WIDTH), 0) // GLA_DV == _iota((GROUP_WIDTH, GROUP_WIDTH), 1) // GLA_DV,
                       1.0, 0.0).astype(BF16)
    ms = _dot_lsplit(o * o, ind_vv) * (1.0 / GLA_DV)
    r = r_ref[0]
    o_ref[0] = o * lax.rsqrt(ms + EPS) * g_ref[...] * (r * _sigmoid(r))


def _gla(proj3, wa2, ba, norm_g):
    B, S, _ = proj3.shape
    nb = S // GLA_BLOCK
    wa_pad = jnp.zeros((LANE, GLA_W), F32).at[:GLA_RANK].set(wa2)
    g_t = jnp.tile(norm_g.reshape(1, GLA_DV), (1, N_HEADS))

    def col(c, w):
        return pl.BlockSpec((1, GLA_BLOCK, w), lambda b, j: (b, j, c // w))

    return pl.pallas_call(
        _gla_kernel,
        grid=(B, nb),
        in_specs=[col(COL_GQ, LANE), col(COL_GK, LANE), col(COL_GV, GROUP_WIDTH), col(COL_GA, LANE),
                  col(COL_GR, GROUP_WIDTH),
                  pl.BlockSpec((LANE, GLA_W), lambda b, j: (0, 0)),
                  pl.BlockSpec((1, GLA_W), lambda b, j: (0, 0)),
                  pl.BlockSpec((1, GROUP_WIDTH), lambda b, j: (0, 0))],
        out_specs=pl.BlockSpec((1, GLA_BLOCK, GROUP_WIDTH), lambda b, j: (b, j, 0)),
        out_shape=jax.ShapeDtypeStruct((B, S, GROUP_WIDTH), F32),
        scratch_shapes=[pltpu.VMEM((GROUP_WIDTH, GLA_W), F32)],
        compiler_params=_cparams(("parallel", "arbitrary")),
    )(proj3, proj3, proj3, proj3, proj3, wa_pad, ba.reshape(1, GLA_W), g_t)


S5_CW = S5_CHUNK * S5_GROUP
S5_SW = 2 * S5_STATE


def _s5_matrices(a_re, a_im, b_re, b_im, c_re, c_im, log_dt):
    hp = lax.Precision.HIGHEST
    L = S5_CHUNK
    dt = jnp.exp(log_dt)[:, None]
    lam_re, lam_im = dt * a_re, dt * a_im
    tau = jnp.arange(L + 1, dtype=F32)[:, None, None]
    mag = jnp.exp(tau * lam_re)
    p_re, p_im = mag * jnp.cos(tau * lam_im), mag * jnp.sin(tau * lam_im)
    den = a_re * a_re + a_im * a_im
    f_re = ((p_re[1] - 1.0) * a_re + p_im[1] * a_im) / den
    f_im = (p_im[1] * a_re - (p_re[1] - 1.0) * a_im) / den
    bb_re = f_re[..., None] * b_re - f_im[..., None] * b_im
    bb_im = f_re[..., None] * b_im + f_im[..., None] * b_re
    pb_re = p_re[..., None] * bb_re - p_im[..., None] * bb_im
    pb_im = p_re[..., None] * bb_im + p_im[..., None] * bb_re
    kern = (jnp.einsum('gon,tgnc->tgoc', c_re, pb_re, precision=hp)
            - jnp.einsum('gon,tgnc->tgoc', c_im, pb_im, precision=hp))
    eye = jnp.eye(S5_GROUPS, dtype=F32)
    s_idx = jnp.arange(L)[:, None]
    t_idx = jnp.arange(L)[None, :]
    lag = jnp.clip(t_idx - s_idx, 0, L)
    toep = jnp.where((t_idx >= s_idx)[..., None, None, None], kern[lag], 0.0)
    toep = toep.transpose(0, 2, 4, 1, 3)
    toep = toep[:, :, :, :, None, :] * eye[None, :, None, None, :, None]
    toep = toep.reshape(S5_ROW, S5_ROW)
    rev = (L - 1 - jnp.arange(L))
    b_in = jnp.stack([pb_re[rev], pb_im[rev]], axis=0).transpose(1, 2, 4, 0, 3)
    b_in = b_in[:, :, :, :, None, :] * eye[None, :, None, None, :, None]
    b_in = b_in.reshape(S5_ROW, S5_XW)
    pt_re, pt_im = p_re[1:], p_im[1:]
    co_re = c_re[None] * pt_re[:, :, None, :] - c_im[None] * pt_im[:, :, None, :]
    co_im = -c_re[None] * pt_im[:, :, None, :] - c_im[None] * pt_re[:, :, None, :]
    c_out = jnp.stack([co_re, co_im], axis=0).transpose(0, 2, 4, 1, 3)
    c_out = c_out[:, :, :, :, None, :] * eye[None, :, None, None, :, None]
    c_out = c_out.reshape(S5_XW, S5_ROW)
    a_chunk = jnp.stack([p_re[L].reshape(1, -1), p_im[L].reshape(1, -1)], axis=0)
    return toep.astype(BF16), b_in.astype(BF16), c_out.astype(BF16), a_chunk


S5_ROW = S5_CHUNK * GROUP_WIDTH
S5_XW = 2 * S5_GROUPS * S5_STATE
S5_OUT_STEPS = 8


def _s5_in_kernel(u_ref, w_ref, v_ref):
    acc = _dot(u_ref[:, 0, :].astype(BF16), w_ref[0])
    for s in range(1, S5_CHUNK):
        acc = acc + _dot(u_ref[:, s, :].astype(BF16), w_ref[s])
    v_ref[...] = acc


def _s5_out_kernel(u_ref, x_ref, tz_ref, c_ref, o_ref):
    acc = _dot(x_ref[...].astype(BF16), c_ref[...])
    for s in range(S5_CHUNK):
        acc = acc + _dot(u_ref[:, s, :].astype(BF16), tz_ref[s])
    for t in range(S5_OUT_STEPS):
        o_ref[:, t, :] = acc[:, t * GROUP_WIDTH:(t + 1) * GROUP_WIDTH]


def _s5_scan_kernel(v_ref, a_ref, x_ref, st_ref):
    half = a_ref.shape[2]
    nb = v_ref.shape[1]
    a_r = jnp.broadcast_to(a_ref[0], (nb, half))
    a_i = jnp.broadcast_to(a_ref[1], (nb, half))

    @pl.when(pl.program_id(0) == 0)
    def _():
        st_ref[...] = jnp.zeros_like(st_ref)

    def body(m, carry):
        x_r, x_i = carry
        x_ref[m, :, :half] = x_r
        x_ref[m, :, half:] = x_i
        v = v_ref[m]
        return a_r * x_r - a_i * x_i + v[:, :half], a_r * x_i + a_i * x_r + v[:, half:]

    x_r, x_i = lax.fori_loop(0, v_ref.shape[0], body, (st_ref[:, :half], st_ref[:, half:]))
    st_ref[:, :half] = x_r
    st_ref[:, half:] = x_i


def _s5_linear(proj3, a_re, a_im, b_re, b_im, c_re, c_im, log_dt):
    B, S, _ = proj3.shape
    M = S // S5_CHUNK
    R = B * M
    rt = min(256, R)
    toep, b_in, c_out, a_chunk = _s5_matrices(a_re, a_im, b_re, b_im, c_re, c_im, log_dt)
    toep = toep.reshape(S5_CHUNK, GROUP_WIDTH, S5_ROW)
    b_in = b_in.reshape(S5_CHUNK, GROUP_WIDTH, S5_XW)
    chunks = proj3.reshape(R, S5_CHUNK, PROJ_PAD)
    ucol = COL_SU // GROUP_WIDTH
    v = pl.pallas_call(
        _s5_in_kernel,
        grid=(R // rt,),
        in_specs=[pl.BlockSpec((rt, S5_CHUNK, GROUP_WIDTH), lambda i: (i, 0, ucol)),
                  _resident((S5_CHUNK, GROUP_WIDTH, S5_XW), lambda i: (0, 0, 0))],
        out_specs=pl.BlockSpec((rt, S5_XW), lambda i: (i, 0)),
        out_shape=jax.ShapeDtypeStruct((R, S5_XW), F32),
        compiler_params=_cparams(("parallel",)),
    )(chunks, b_in)
    v_t = v.reshape(B, M, S5_XW).transpose(1, 0, 2)
    mc = 32
    x_t = pl.pallas_call(
        _s5_scan_kernel,
        grid=(M // mc,),
        in_specs=[pl.BlockSpec((mc, B, S5_XW), lambda j: (j, 0, 0)),
                  pl.BlockSpec((2, 1, S5_XW // 2), lambda j: (0, 0, 0))],
        out_specs=pl.BlockSpec((mc, B, S5_XW), lambda j: (j, 0, 0)),
        out_shape=jax.ShapeDtypeStruct((M, B, S5_XW), F32),
        scratch_shapes=[pltpu.VMEM((B, S5_XW), F32)],
        compiler_params=_cparams(("arbitrary",)),
    )(v_t, a_chunk)
    x = x_t.transpose(1, 0, 2).reshape(R, S5_XW)
    tn = S5_OUT_STEPS * GROUP_WIDTH
    y = pl.pallas_call(
        _s5_out_kernel,
        grid=(S5_ROW // tn, R // rt),
        in_specs=[pl.BlockSpec((rt, S5_CHUNK, GROUP_WIDTH), lambda j, i: (i, 0, ucol)),
                  pl.BlockSpec((rt, S5_XW), lambda j, i: (i, 0)),
                  _resident((S5_CHUNK, GROUP_WIDTH, tn), lambda j, i: (0, 0, j)),
                  _resident((S5_XW, tn), lambda j, i: (0, j))],
        out_specs=pl.BlockSpec((rt, S5_OUT_STEPS, GROUP_WIDTH), lambda j, i: (i, j, 0)),
        out_shape=jax.ShapeDtypeStruct((R, S5_CHUNK, GROUP_WIDTH), F32),
        compiler_params=_cparams(("arbitrary", "arbitrary")),
    )(chunks, x, toep, c_out)
    return y.reshape(B, S, GROUP_WIDTH)


def _mixout_kernel(dil_ref, nsa_ref, gla_ref, s5y_ref, u_ref, x_ref,
                   s5d_ref, gw_ref, gb_ref, ng_ref, wo_ref, o_ref):
    u = u_ref[...]
    hg = _gelu(s5y_ref[...] + s5d_ref[...] * u)
    y_s5 = hg * _sigmoid(_dot(hg.astype(BF16), gw_ref[...]) + gb_ref[...])
    parts = [dil_ref[...], nsa_ref[...], gla_ref[...], y_s5]
    mix = jnp.concatenate([(_rms(parts[n]) * ng_ref[n:n + 1, :]).astype(BF16) for n in range(4)], axis=1)
    o_ref[...] = x_ref[...] + _dot(mix, wo_ref[...])


def _mixout(y_dil, y_nsa, y_gla, y_s5, proj, x2, s5_d, glu_w, glu_b, out_g, w_out):
    T = x2.shape[0]
    tm = 512

    def row(w, c=0):
        return pl.BlockSpec((tm, w), lambda i: (i, c))

    def const(shape):
        return pl.BlockSpec(shape, lambda i: (0,) * len(shape))

    return pl.pallas_call(
        _mixout_kernel,
        grid=(T // tm,),
        in_specs=[row(GROUP_WIDTH), row(GROUP_WIDTH), row(GROUP_WIDTH),
                  row(GROUP_WIDTH), row(GROUP_WIDTH, COL_SU // GROUP_WIDTH), row(D_MODEL),
                  const((1, GROUP_WIDTH)), const((GROUP_WIDTH, GROUP_WIDTH)), const((1, GROUP_WIDTH)),
                  const((4, GROUP_WIDTH)), const((D_MODEL, D_MODEL))],
        out_specs=row(D_MODEL),
        out_shape=jax.ShapeDtypeStruct((T, D_MODEL), F32),
        compiler_params=_cparams(("parallel",)),
    )(y_dil, y_nsa, y_gla, y_s5, proj, x2, s5_d.reshape(1, -1), glu_w.astype(BF16),
      glu_b.reshape(1, -1), out_g.reshape(4, GROUP_WIDTH), w_out.astype(BF16))


def _swiglu_chunks(h, wg_ref, wu_ref, wd_ref, acc, width, chunk, lead=()):
    for c in range(width // chunk):
        cs = slice(c * chunk, (c + 1) * chunk)
        a = _dot(h, wg_ref[lead + (slice(None), cs)])
        u = _dot(h, wu_ref[lead + (slice(None), cs)])
        acc = acc + _dot((a * _sigmoid(a) * u).astype(BF16), wd_ref[lead + (cs, slice(None))])
    return acc


def _ffn_kernel(x_ref, g_ref, wg_ref, wu_ref, wd_ref, o_ref):
    x = x_ref[...]
    h = (_rms(x) * g_ref[...]).astype(BF16)
    o_ref[...] = _swiglu_chunks(h, wg_ref, wu_ref, wd_ref, x, D_FF_PAD, D_FF_PAD // 2)


def _resident(shape, index_map):
    return pl.BlockSpec(shape, index_map, pipeline_mode=pl.Buffered(1))


def _ffn(x2, g, wg, wu, wd):
    T = x2.shape[0]
    tm = 512
    pad = D_FF_PAD - D_FF
    wg_p = jnp.pad(wg, ((0, 0), (0, pad))).astype(BF16)
    wu_p = jnp.pad(wu, ((0, 0), (0, pad))).astype(BF16)
    wd_p = jnp.pad(wd, ((0, pad), (0, 0))).astype(BF16)
    return pl.pallas_call(
        _ffn_kernel,
        grid=(T // tm,),
        in_specs=[pl.BlockSpec((tm, D_MODEL), lambda i: (i, 0)),
                  pl.BlockSpec((1, D_MODEL), lambda i: (0, 0)),
                  _resident((D_MODEL, D_FF_PAD), lambda i: (0, 0)),
                  _resident((D_MODEL, D_FF_PAD), lambda i: (0, 0)),
                  _resident((D_FF_PAD, D_MODEL), lambda i: (0, 0))],
        out_specs=pl.BlockSpec((tm, D_MODEL), lambda i: (i, 0)),
        out_shape=jax.ShapeDtypeStruct((T, D_MODEL), F32),
        compiler_params=_cparams(("parallel",)),
    )(x2, g.reshape(1, D_MODEL), wg_p, wu_p, wd_p)


MOE_TILE = 512
DSP_TILE = 128
CMB_TILE = 128


def _router_kernel(x_ref, g_ref, w_ref, b_ref, r_ref):
    h = _rms(x_ref[...]) * g_ref[...]
    lane = _iota((h.shape[0], LANE), 1)
    logits = jnp.where(lane < N_EXPERTS, _dot3(h, w_ref[...]) + b_ref[...], -jnp.inf)
    m1 = jnp.max(logits, axis=-1, keepdims=True)
    i1 = jnp.min(jnp.where(logits == m1, lane, LANE), axis=-1, keepdims=True)
    rest = jnp.where(lane == i1, -jnp.inf, logits)
    m2 = jnp.max(rest, axis=-1, keepdims=True)
    i2 = jnp.min(jnp.where(rest == m2, lane, LANE), axis=-1, keepdims=True)
    e2 = jnp.exp(m2 - m1)
    w1 = 1.0 / (1.0 + e2)
    w2 = e2 / (1.0 + e2)
    r_ref[...] = jnp.where(lane == 0, i1.astype(F32),
                           jnp.where(lane == 1, i2.astype(F32),
                                     jnp.where(lane == 2, w1, jnp.where(lane == 3, w2, 0.0))))


def _router(x2, g, rw, rb):
    T = x2.shape[0]
    tm = 512
    rw_p = jnp.pad(rw, ((0, 0), (0, LANE - N_EXPERTS)))
    rb_p = jnp.pad(rb, (0, LANE - N_EXPERTS)).reshape(1, LANE)
    return pl.pallas_call(
        _router_kernel,
        grid=(T // tm,),
        in_specs=[pl.BlockSpec((tm, D_MODEL), lambda i: (i, 0)),
                  pl.BlockSpec((1, D_MODEL), lambda i: (0, 0)),
                  pl.BlockSpec((D_MODEL, LANE), lambda i: (0, 0)),
                  pl.BlockSpec((1, LANE), lambda i: (0, 0))],
        out_specs=pl.BlockSpec((tm, LANE), lambda i: (i, 0)),
        out_shape=jax.ShapeDtypeStruct((T, LANE), F32),
        compiler_params=_cparams(("parallel",)),
    )(x2, g.reshape(1, D_MODEL), rw_p, rb_p)


def _smem_rows(width, nt, shift):
    return pl.BlockSpec((1, 1, width), lambda i, *_: (jnp.clip(i + shift, 0, nt - 1), 0, 0),
                        memory_space=pltpu.SMEM)


def _scatter_copy(buf, slot, r, dst_hbm, row, sem, k):
    return pltpu.make_async_copy(buf.at[slot, pl.ds(r, 1), :], dst_hbm.at[pl.ds(row, 1), :], sem.at[slot, k, r])


def _dispatch_kernel(p1_ref, p2_ref, q1_ref, q2_ref, x_ref, g_ref, xs_in, xs_hbm, hbuf, sem):
    del xs_in
    i = pl.program_id(0)
    nt = pl.num_programs(0)
    slot = i % 2
    hbuf[slot] = _rms(x_ref[...]) * g_ref[...]
    for r in range(DSP_TILE):
        _scatter_copy(hbuf, slot, r, xs_hbm, p1_ref[0, 0, r], sem, 0).start()
        _scatter_copy(hbuf, slot, r, xs_hbm, p2_ref[0, 0, r], sem, 1).start()

    def wait_all(pa, pb, s):
        for r in range(DSP_TILE):
            _scatter_copy(hbuf, s, r, xs_hbm, pa[0, 0, r], sem, 0).wait()
            _scatter_copy(hbuf, s, r, xs_hbm, pb[0, 0, r], sem, 1).wait()

    @pl.when(i > 0)
    def _():
        wait_all(q1_ref, q2_ref, 1 - slot)

    @pl.when(i == nt - 1)
    def _():
        wait_all(p1_ref, p2_ref, slot)


def _moe_dispatch(x2, g, pos, n_rows):
    T = x2.shape[0]
    nt = T // DSP_TILE
    p1 = pos[:, 0].reshape(nt, 1, DSP_TILE)
    p2 = pos[:, 1].reshape(nt, 1, DSP_TILE)
    return pl.pallas_call(
        _dispatch_kernel,
        grid=(nt,),
        in_specs=[_smem_rows(DSP_TILE, nt, 0), _smem_rows(DSP_TILE, nt, 0),
                  _smem_rows(DSP_TILE, nt, -1), _smem_rows(DSP_TILE, nt, -1),
                  pl.BlockSpec((DSP_TILE, D_MODEL), lambda i: (i, 0)),
                  pl.BlockSpec((1, D_MODEL), lambda i: (0, 0)),
                  pl.BlockSpec(memory_space=pl.ANY)],
        out_specs=pl.BlockSpec(memory_space=pl.ANY),
        out_shape=jax.ShapeDtypeStruct((n_rows, D_MODEL), F32),
        scratch_shapes=[pltpu.VMEM((2, DSP_TILE, D_MODEL), F32), pltpu.SemaphoreType.DMA((2, 2, DSP_TILE))],
        input_output_aliases={6: 0},
        compiler_params=_cparams(("arbitrary",)),
    )(p1, p2, p1, p2, x2, g.reshape(1, D_MODEL), jnp.zeros((n_rows, D_MODEL), F32))


def _moe_kernel(texp_ref, x_ref, wg_ref, wu_ref, wd_ref, y_ref):
    y_ref[...] = _swiglu_chunks(x_ref[...].astype(BF16), wg_ref, wu_ref, wd_ref,
                                jnp.zeros((MOE_TILE, D_MODEL), F32), D_FF_EXPERT, D_FF_EXPERT // 4, lead=(0,))


def _moe_experts(xs, tile_expert, wg, wu, wd):
    nt = tile_expert.shape[0]
    grid_spec = pltpu.PrefetchScalarGridSpec(
        num_scalar_prefetch=1,
        grid=(nt,),
        in_specs=[
            pl.BlockSpec((MOE_TILE, D_MODEL), lambda i, te: (i, 0)),
            _resident((1, D_MODEL, D_FF_EXPERT), lambda i, te: (te[i], 0, 0)),
            _resident((1, D_MODEL, D_FF_EXPERT), lambda i, te: (te[i], 0, 0)),
            _resident((1, D_FF_EXPERT, D_MODEL), lambda i, te: (te[i], 0, 0)),
        ],
        out_specs=pl.BlockSpec((MOE_TILE, D_MODEL), lambda i, te: (i, 0)),
    )
    return pl.pallas_call(
        _moe_kernel,
        grid_spec=grid_spec,
        out_shape=jax.ShapeDtypeStruct((nt * MOE_TILE, D_MODEL), F32),
        compiler_params=_cparams(("arbitrary",)),
    )(tile_expert, xs, wg.astype(BF16), wu.astype(BF16), wd.astype(BF16))


def _gather_copy(src_hbm, row, buf, slot, k, r, sem):
    return pltpu.make_async_copy(src_hbm.at[pl.ds(row, 1), :], buf.at[slot, k, pl.ds(r, 1), :], sem.at[slot, k, r])


def _combine_kernel(p1_ref, p2_ref, n1_ref, n2_ref, y_hbm, x_ref, r_ref, o_ref, ybuf, sem):
    i = pl.program_id(0)
    nt = pl.num_programs(0)
    slot = i % 2

    def issue(pa, pb, s):
        for r in range(CMB_TILE):
            _gather_copy(y_hbm, pa[0, 0, r], ybuf, s, 0, r, sem).start()
            _gather_copy(y_hbm, pb[0, 0, r], ybuf, s, 1, r, sem).start()

    @pl.when(i == 0)
    def _():
        issue(p1_ref, p2_ref, 0)

    @pl.when(i + 1 < nt)
    def _():
        issue(n1_ref, n2_ref, 1 - slot)

    for r in range(CMB_TILE):
        _gather_copy(y_hbm, p1_ref[0, 0, r], ybuf, slot, 0, r, sem).wait()
        _gather_copy(y_hbm, p2_ref[0, 0, r], ybuf, slot, 1, r, sem).wait()
    r = r_ref[...]
    o_ref[...] = x_ref[...] + r[:, 2:3] * ybuf[slot, 0] + r[:, 3:4] * ybuf[slot, 1]


def _moe_combine(y_sorted, pos, x2, route):
    T = x2.shape[0]
    nt = T // CMB_TILE
    p1 = pos[:, 0].reshape(nt, 1, CMB_TILE)
    p2 = pos[:, 1].reshape(nt, 1, CMB_TILE)
    return pl.pallas_call(
        _combine_kernel,
        grid=(nt,),
        in_specs=[_smem_rows(CMB_TILE, nt, 0), _smem_rows(CMB_TILE, nt, 0),
                  _smem_rows(CMB_TILE, nt, 1), _smem_rows(CMB_TILE, nt, 1),
                  pl.BlockSpec(memory_space=pl.ANY),
                  pl.BlockSpec((CMB_TILE, D_MODEL), lambda i: (i, 0)),
                  pl.BlockSpec((CMB_TILE, LANE), lambda i: (i, 0))],
        out_specs=pl.BlockSpec((CMB_TILE, D_MODEL), lambda i: (i, 0)),
        out_shape=jax.ShapeDtypeStruct((T, D_MODEL), F32),
        scratch_shapes=[pltpu.VMEM((2, 2, CMB_TILE, D_MODEL), F32),
                        pltpu.SemaphoreType.DMA((2, 2, CMB_TILE))],
        compiler_params=_cparams(("arbitrary",)),
    )(p1, p2, p1, p2, y_sorted, x2, route)


def _moe(x2, g, rw, rb, wg, wu, wd):
    T = x2.shape[0]
    route = _router(x2, g, rw, rb)
    experts = route[:, :2].astype(jnp.int32)
    onehot = (experts.reshape(-1)[:, None] == jnp.arange(N_EXPERTS)[None, :]).astype(jnp.int32)
    csum = jnp.cumsum(onehot, axis=0)
    counts = csum[-1]
    rank = jnp.sum(csum * onehot, axis=1) - 1
    padded = ((counts + MOE_TILE - 1) // MOE_TILE) * MOE_TILE
    ends = jnp.cumsum(padded)
    starts = ends - padded
    pos = (jnp.sum(starts[None, :] * onehot, axis=1) + rank).reshape(T, 2)
    nt = (2 * T) // MOE_TILE + N_EXPERTS
    tile_start = jnp.arange(nt, dtype=jnp.int32) * MOE_TILE
    tile_expert = jnp.minimum(jnp.sum(tile_start[:, None] >= ends[None, :], axis=1), N_EXPERTS - 1)
    xs = _moe_dispatch(x2, g, pos, nt * MOE_TILE)
    y_sorted = _moe_experts(xs, tile_expert.astype(jnp.int32), wg, wu, wd)
    return _moe_combine(y_sorted, pos, x2, route)


def _layer_mixers(x2, B, S, l, p):
    proj = _inproj(x2, p["norm1_g"][l], _pad_w_in(p["w_in"][l]), _head_gains(p["dil_qk_g"][l], p["nsa_qk_g"][l]))
    proj3 = proj.reshape(B, S, PROJ_PAD)
    y_dil = _dil_attention(proj3).reshape(B * S, GROUP_WIDTH)
    kc, vc, ks, vs, kw, vw = _nsa_prep(proj3, p["nsa_cmp_pos"][l], p["nsa_cmp_w1"][l], p["nsa_cmp_w2"][l],
                                       p["nsa_qk_g"][l])
    y_nsa = _nsa_attention(proj3, kc, vc, ks, vs, kw, vw).reshape(B * S, GROUP_WIDTH)
    y_gla = _gla(proj3, p["gla_wa2"][l], p["gla_ba"][l], p["gla_norm_g"][l]).reshape(B * S, GROUP_WIDTH)
    y_s5 = _s5_linear(proj3, p["s5_a_re"][l], p["s5_a_im"][l],
                      p["s5_b_re"][l], p["s5_b_im"][l], p["s5_c_re"][l], p["s5_c_im"][l],
                      p["s5_log_dt"][l]).reshape(B * S, GROUP_WIDTH)
    return _mixout(y_dil, y_nsa, y_gla, y_s5, proj, x2, p["s5_d"][l], p["s5_glu_w"][l],
                   p["s5_glu_b"][l], p["out_norm_g"][l], p["w_out"][l])


def kernel(x, norm1_g, w_in, dil_qk_g, nsa_qk_g, nsa_cmp_pos, nsa_cmp_w1, nsa_cmp_w2, gla_wa2, gla_ba,
           gla_norm_g, s5_a_re, s5_a_im, s5_b_re, s5_b_im, s5_c_re, s5_c_im, s5_d, s5_log_dt, s5_glu_w,
           s5_glu_b, out_norm_g, w_out, norm2_g, ffn_w_gate, ffn_w_up, ffn_w_down, moe_router_w,
           moe_router_b, moe_w_gate, moe_w_up, moe_w_down):
    B, S, D = x.shape
    p = dict(norm1_g=norm1_g, w_in=w_in, dil_qk_g=dil_qk_g, nsa_qk_g=nsa_qk_g, nsa_cmp_pos=nsa_cmp_pos,
             nsa_cmp_w1=nsa_cmp_w1, nsa_cmp_w2=nsa_cmp_w2, gla_wa2=gla_wa2, gla_ba=gla_ba,
             gla_norm_g=gla_norm_g, s5_a_re=s5_a_re, s5_a_im=s5_a_im, s5_b_re=s5_b_re, s5_b_im=s5_b_im,
             s5_c_re=s5_c_re, s5_c_im=s5_c_im, s5_d=s5_d, s5_log_dt=s5_log_dt, s5_glu_w=s5_glu_w,
             s5_glu_b=s5_glu_b, out_norm_g=out_norm_g, w_out=w_out)
    x2 = x.reshape(B * S, D)
    depth = norm1_g.shape[0]
    for l in range(depth):
        x2 = _layer_mixers(x2, B, S, l, p)
        i = l // 2
        if l % 2 == 0:
            x2 = _ffn(x2, norm2_g[l], ffn_w_gate[i], ffn_w_up[i], ffn_w_down[i])
        else:
            x2 = _moe(x2, norm2_g[l], moe_router_w[i], moe_router_b[i], moe_w_gate[i], moe_w_up[i],
                      moe_w_down[i])
    return x2.reshape(B, S, D)
```

```python
import functools
import math

import jax
import jax.numpy as jnp
from jax import lax
from jax.experimental import pallas as pl
from jax.experimental.pallas import tpu as pltpu

F32 = jnp.float32
BF16 = jnp.bfloat16

D_MODEL = 1024
HEAD_DIM = 64
GROUP_WIDTH = 256
N_HEADS = 4
DIL_PATTERNS = ((128, 1), (512, 4), (2048, 16))
Q_BLOCK = 128
NSA_CMP_LEN = 32
NSA_CMP_STRIDE = 16
NSA_CMP_HIDDEN = 256
NSA_SEL_LEN = 64
NSA_TOPN = 16
NSA_WIN = 512
FORCE_BONUS = 1.0e4
GLA_DK = 32
GLA_DV = 64
GLA_RANK = 16
GLA_TAU = 16.0
GLA_CHUNK = 16
S5_GROUP = 16
S5_GROUPS = 16
S5_STATE = 64
S5_CHUNK = 16
D_FF = 2752
N_EXPERTS = 8
D_FF_EXPERT = 3584
EPS = 1e-6
NEG_INF = -1e30

DIL_SLOPES = tuple(2.0 ** (-float(i)) for i in (2, 4, 6, 8))
NSA_SLOPES = tuple(2.0 ** (-float(i)) for i in (1, 3, 5, 7))

COL_DQ, COL_DK, COL_DV, COL_NQ = 0, 256, 512, 768
COL_NCMP, COL_NK, COL_NV, COL_NG = 1024, 1152, 1280, 1408
COL_GQ, COL_GK, COL_GV, COL_GA, COL_GR, COL_SU = 1536, 1664, 1792, 2048, 2304, 2560
PROJ_PAD = 2816
LANE = 128
D_FF_PAD = 2816

VMEM_LIMIT = 56 * 1024 * 1024


def _cparams(sem):
    return pltpu.CompilerParams(dimension_semantics=sem, vmem_limit_bytes=VMEM_LIMIT)


def _rms(x):
    return x * lax.rsqrt(jnp.mean(x * x, axis=-1, keepdims=True) + EPS)


def _dot(a, b):
    return jnp.dot(a, b, preferred_element_type=F32)


def _dot_nt(a, b):
    return lax.dot_general(a, b, (((1,), (1,)), ((), ())), preferred_element_type=F32)


def _split(x):
    hi = x.astype(BF16)
    lo = (x - hi.astype(F32)).astype(BF16)
    return hi, lo


def _dot_lsplit(x, m):
    hi, lo = _split(x)
    return _dot(hi, m) + _dot(lo, m)


def _dot_rsplit(m, x):
    hi, lo = _split(x)
    return _dot(m, hi) + _dot(m, lo)


def _dot3(a, b):
    ah, al = _split(a)
    bh, bl = _split(b)
    return _dot(ah, bh) + _dot(ah, bl) + _dot(al, bh)


def _gelu(x):
    return 0.5 * x * (1.0 + jnp.tanh(math.sqrt(2.0 / math.pi) * (x + 0.044715 * (x * x * x))))


def _sigmoid(x):
    return 1.0 / (1.0 + jnp.exp(-x))


def _iota(shape, dim):
    return lax.broadcasted_iota(jnp.int32, shape, dim)


def _head_indicator(rows, cols, rdiv, cdiv):
    return jnp.where(_iota((rows, cols), 0) // rdiv == _iota((rows, cols), 1) // cdiv, 1.0, 0.0).astype(BF16)


def _inproj_kernel(x_ref, g_ref, w_ref, hg_ref, o_ref):
    h = _rms(x_ref[...]) * g_ref[...]
    o_ref[...] = _dot(h.astype(BF16), w_ref[...])
    ind = _head_indicator(GROUP_WIDTH, GROUP_WIDTH, HEAD_DIM, HEAD_DIM)
    for lo, width in NORM_SEGMENTS:
        v = o_ref[:, lo:lo + width]
        ms = _dot_lsplit(v * v, ind[:width, :width]) * (1.0 / HEAD_DIM)
        o_ref[:, lo:lo + width] = v * lax.rsqrt(ms + EPS) * hg_ref[:, lo:lo + width]


NORM_COLS = COL_NV
NORM_SEGMENTS = ((COL_DQ, GROUP_WIDTH), (COL_DK, GROUP_WIDTH), (COL_NQ, GROUP_WIDTH), (COL_NK, LANE))


def _head_gains(dil_g, nsa_g):
    scale = HEAD_DIM ** -0.5
    z = jnp.zeros((HEAD_DIM,), F32)
    row = jnp.concatenate([jnp.tile(dil_g[0] * scale, N_HEADS), jnp.tile(dil_g[1], N_HEADS),
                           jnp.zeros((GROUP_WIDTH,), F32), jnp.tile(nsa_g[0] * scale, N_HEADS),
                           z, z, nsa_g[2], nsa_g[3]])
    return row.reshape(1, NORM_COLS)


def _inproj(x2, g, w_pad, head_gains):
    T = x2.shape[0]
    tm = 512
    return pl.pallas_call(
        _inproj_kernel,
        grid=(T // tm,),
        in_specs=[
            pl.BlockSpec((tm, D_MODEL), lambda i: (i, 0)),
            pl.BlockSpec((1, D_MODEL), lambda i: (0, 0)),
            pl.BlockSpec((D_MODEL, PROJ_PAD), lambda i: (0, 0)),
            pl.BlockSpec((1, NORM_COLS), lambda i: (0, 0)),
        ],
        out_specs=pl.BlockSpec((tm, PROJ_PAD), lambda i: (i, 0)),
        out_shape=jax.ShapeDtypeStruct((T, PROJ_PAD), F32),
        compiler_params=_cparams(("parallel",)),
    )(x2, g.reshape(1, D_MODEL), w_pad, head_gains)


def _pad_w_in(w):
    def seg(lo, width, pad_to):
        s = w[:, lo:lo + width]
        if pad_to > width:
            s = jnp.pad(s, ((0, 0), (0, pad_to - width)))
        return s
    k_cmp, v_cmp = seg(1024, 64, 64), seg(1088, 64, 64)
    k_slc, v_slc = seg(1152, 64, 64), seg(1216, 64, 64)
    k_win, v_win = seg(1280, 64, 64), seg(1344, 64, 64)
    pieces = [
        seg(0, 256, 256), seg(256, 256, 256), seg(512, 256, 256), seg(768, 256, 256),
        k_cmp, v_cmp, k_slc, k_win, v_slc, v_win,
        seg(1408, 12, 128),
        seg(1420, 128, 128), seg(1548, 128, 128), seg(1676, 256, 256),
        seg(1932, 16, 256),
        seg(1948, 256, 256), seg(2204, 256, 256),
    ]
    out = jnp.concatenate(pieces, axis=1)
    assert out.shape[1] == PROJ_PAD
    return out.astype(BF16)


DIL_SB = Q_BLOCK * max(d for _, d in DIL_PATTERNS)
DIL_SUBSTEPS = DIL_SB // Q_BLOCK


def _dil_kernel(q0_ref, q1_ref, k0_ref, k1_ref, v0_ref, v1_ref, o_ref,
                ks_ref, vs_ref, bias_ref, m_ref, l_ref, acc_ref):
    i = pl.program_id(1)
    rows4 = N_HEADS * Q_BLOCK

    def ld2(ref, rows):
        return jnp.concatenate([ref[0, rows, :], ref[1, rows, :]], axis=1)

    def st2(ref, rows, val):
        ref[0, rows, :] = val[:, :LANE]
        ref[1, rows, :] = val[:, LANE:]

    @pl.when(i == 0)
    def _():
        ks_ref[:, 0:DIL_SB, :] = jnp.zeros((2, DIL_SB, LANE), F32)
        vs_ref[:, 0:DIL_SB, :] = jnp.zeros((2, DIL_SB, LANE), F32)
        ri = _iota((rows4, 2 * Q_BLOCK), 0)
        ci = _iota((rows4, 2 * Q_BLOCK), 1)
        delta = Q_BLOCK + ri % Q_BLOCK - ci
        head = ri // Q_BLOCK
        slope = jnp.where(head == 0, DIL_SLOPES[0],
                          jnp.where(head == 1, DIL_SLOPES[1],
                                    jnp.where(head == 2, DIL_SLOPES[2], DIL_SLOPES[3]))).astype(F32)
        band = (delta >= 0) & (delta <= Q_BLOCK)
        for p, (_, dil) in enumerate(DIL_PATTERNS):
            b = jnp.where(band, -(slope * dil) * delta.astype(F32), NEG_INF)
            bias_ref[p, 0] = b
            bias_ref[p, 1] = jnp.where(ci >= Q_BLOCK, b, NEG_INF)

    @pl.when(i > 0)
    def _():
        ks_ref[:, 0:DIL_SB, :] = ks_ref[:, DIL_SB:, :]
        vs_ref[:, 0:DIL_SB, :] = vs_ref[:, DIL_SB:, :]

    ks_ref[0, DIL_SB:, :] = k0_ref[0]
    ks_ref[1, DIL_SB:, :] = k1_ref[0]
    vs_ref[0, DIL_SB:, :] = v0_ref[0]
    vs_ref[1, DIL_SB:, :] = v1_ref[0]
    lane_head = _iota((Q_BLOCK, GROUP_WIDTH), 1) // HEAD_DIM

    def spread(x):
        out = jnp.zeros((Q_BLOCK, GROUP_WIDTH), F32)
        for h in range(N_HEADS):
            out = jnp.where(lane_head == h, x[h * Q_BLOCK:(h + 1) * Q_BLOCK], out)
        return out

    order = sorted(range(len(DIL_PATTERNS)), key=lambda n: -DIL_PATTERNS[n][1])
    assert DIL_PATTERNS[order[-1]][1] == 1
    for p in order:
        dil = DIL_PATTERNS[p][1]

        def sub(j, carry, p=p, dil=dil):
            start = (j // dil) * (Q_BLOCK * dil) + j % dil
            rows = pl.ds(start, Q_BLOCK, stride=dil)
            cur = pl.ds(DIL_SB + start, Q_BLOCK, stride=dil)
            prev = pl.ds(DIL_SB + start - Q_BLOCK * dil, Q_BLOCK, stride=dil)
            q = jnp.concatenate([q0_ref[0, rows, :], q1_ref[0, rows, :]], axis=1)
            q4 = jnp.concatenate([jnp.where(lane_head == h, q, 0.0) for h in range(N_HEADS)], axis=0)
            kk = jnp.concatenate([ld2(ks_ref, prev), ld2(ks_ref, cur)], axis=0).astype(BF16)
            vv = jnp.concatenate([ld2(vs_ref, prev), ld2(vs_ref, cur)], axis=0).astype(BF16)
            first = jnp.where((i == 0) & (j < dil), 1, 0)
            s = _dot_nt(q4.astype(BF16), kk) + bias_ref[p, first]
            m = jnp.max(s, axis=-1, keepdims=True)
            e = jnp.exp(s - m)
            l = jnp.sum(e, axis=-1, keepdims=True)
            o_t = spread(_dot(e.astype(BF16), vv))
            m_t = spread(m)
            l_t = spread(l)
            if p == order[0]:
                st2(m_ref, rows, m_t)
                st2(l_ref, rows, l_t)
                st2(acc_ref, rows, o_t)
                return carry
            m_old = ld2(m_ref, rows)
            m_new = jnp.maximum(m_old, m_t)
            a = jnp.exp(m_old - m_new)
            b = jnp.exp(m_t - m_new)
            l_new = a * ld2(l_ref, rows) + b * l_t
            acc_new = a * ld2(acc_ref, rows) + b * o_t
            if p == order[-1]:
                o_ref[0, pl.ds(pl.multiple_of(start, Q_BLOCK), Q_BLOCK), :] = acc_new / l_new
            else:
                st2(m_ref, rows, m_new)
                st2(l_ref, rows, l_new)
                st2(acc_ref, rows, acc_new)
            return carry
        lax.fori_loop(0, DIL_SUBSTEPS, sub, 0)


def _dil_attention(proj3):
    B, S, _ = proj3.shape
    assert all(win // d == Q_BLOCK for win, d in DIL_PATTERNS) and S % DIL_SB == 0

    def halves(c):
        return [pl.BlockSpec((1, DIL_SB, LANE), lambda b, i, c=c, h=h: (b, i, c // LANE + h)) for h in range(2)]

    sb = pltpu.VMEM((2, DIL_SB, LANE), F32)
    sb2 = pltpu.VMEM((2, 2 * DIL_SB, LANE), F32)
    return pl.pallas_call(
        _dil_kernel,
        grid=(B, S // DIL_SB),
        in_specs=halves(COL_DQ) + halves(COL_DK) + halves(COL_DV),
        out_specs=pl.BlockSpec((1, DIL_SB, GROUP_WIDTH), lambda b, i: (b, i, 0)),
        out_shape=jax.ShapeDtypeStruct((B, S, GROUP_WIDTH), F32),
        scratch_shapes=[sb2, sb2, pltpu.VMEM((len(DIL_PATTERNS), 2, N_HEADS * Q_BLOCK, 2 * Q_BLOCK), F32),
                        sb, sb, sb],
        compiler_params=_cparams(("parallel", "arbitrary")),
    )(*([proj3] * 6))


def _nsa_prep_kernel(xk_ref, xv_ref, kraw_ref, vraw_ref, pos_ref, w1_ref, w2_ref, g_ref,
                     kc_ref, vc_ref, ks_ref, vs_ref, kw_ref, vw_ref):
    half = NSA_CMP_STRIDE * HEAD_DIM

    def compress(x, idx):
        w1 = w1_ref[idx]
        xb = x.astype(BF16)
        a = _dot(xb, w1[:half, :])
        b = _dot(xb, w1[half:, :])
        bias = _dot(pos_ref[idx].astype(BF16), w1)[0:1, :]
        pre = a + pltpu.roll(b, b.shape[0] - 1, 0) + bias
        return _dot(_gelu(pre).astype(BF16), w2_ref[idx])

    def with_pos(k, pos):
        lane = _iota((k.shape[0], HEAD_DIM), 1)
        extra = jnp.where(lane == 0, (pos // 256).astype(F32), jnp.where(lane == 1, (pos % 256).astype(F32), 0.0))
        return jnp.concatenate([k, extra], axis=1).astype(BF16)

    def with_one_t(v):
        lane = _iota((v.shape[0], HEAD_DIM), 1)
        return jnp.concatenate([v, jnp.where(lane == 0, 1.0, 0.0)], axis=1).T.astype(BF16)

    ng = xk_ref.shape[1]
    cend = _iota((ng, 1), 0) * NSA_CMP_STRIDE + (NSA_CMP_LEN - 1)
    kc_ref[0] = with_pos(_rms(compress(xk_ref[0], 0)) * g_ref[1:2, :], cend)
    vc_ref[0] = with_one_t(compress(xv_ref[0], 1))
    kraw = kraw_ref[0]
    tpos = _iota((kraw.shape[0], 1), 0)
    ks_ref[0] = with_pos(kraw[:, :HEAD_DIM], tpos)
    kw_ref[0] = with_pos(kraw[:, HEAD_DIM:], tpos)
    for c in range(vs_ref.shape[1]):
        vs_ref[0, c] = with_one_t(vraw_ref[0, c * NSA_KCHUNK:(c + 1) * NSA_KCHUNK, :HEAD_DIM])
    for c in range(vw_ref.shape[1]):
        vw_ref[0, c] = with_one_t(vraw_ref[0, c * Q_BLOCK:(c + 1) * Q_BLOCK, HEAD_DIM:])


NSA_KCHUNK = 2 * Q_BLOCK


def _nsa_prep(proj3, pos, w1, w2, g):
    B, S, _ = proj3.shape
    ng = S // NSA_CMP_STRIDE
    gw = NSA_CMP_STRIDE * HEAD_DIM
    xk = proj3[:, :, COL_NCMP:COL_NCMP + HEAD_DIM].reshape(B, ng, gw)
    xv = proj3[:, :, COL_NCMP + HEAD_DIM:COL_NCMP + 2 * HEAD_DIM].reshape(B, ng, gw)
    pos8 = jnp.broadcast_to(pos.reshape(2, 1, NSA_CMP_LEN * HEAD_DIM), (2, 8, NSA_CMP_LEN * HEAD_DIM))
    small = jax.ShapeDtypeStruct((B, ng, LANE), BF16)
    small_t = jax.ShapeDtypeStruct((B, LANE, ng), BF16)
    big = jax.ShapeDtypeStruct((B, S, LANE), BF16)
    vs_t = jax.ShapeDtypeStruct((B, S // NSA_KCHUNK, LANE, NSA_KCHUNK), BF16)
    vw_t = jax.ShapeDtypeStruct((B, S // Q_BLOCK, LANE, Q_BLOCK), BF16)
    small_spec = pl.BlockSpec((1, ng, LANE), lambda b: (b, 0, 0))
    small_t_spec = pl.BlockSpec((1, LANE, ng), lambda b: (b, 0, 0))
    big_spec = pl.BlockSpec((1, S, LANE), lambda b: (b, 0, 0))
    vs_spec = pl.BlockSpec((1, S // NSA_KCHUNK, LANE, NSA_KCHUNK), lambda b: (b, 0, 0, 0))
    vw_spec = pl.BlockSpec((1, S // Q_BLOCK, LANE, Q_BLOCK), lambda b: (b, 0, 0, 0))
    return pl.pallas_call(
        _nsa_prep_kernel,
        grid=(B,),
        in_specs=[
            pl.BlockSpec((1, ng, gw), lambda b: (b, 0, 0)),
            pl.BlockSpec((1, ng, gw), lambda b: (b, 0, 0)),
            pl.BlockSpec((1, S, LANE), lambda b: (b, 0, COL_NK // LANE)),
            pl.BlockSpec((1, S, LANE), lambda b: (b, 0, COL_NV // LANE)),
            pl.BlockSpec((2, 8, NSA_CMP_LEN * HEAD_DIM), lambda b: (0, 0, 0)),
            pl.BlockSpec((2, NSA_CMP_LEN * HEAD_DIM, NSA_CMP_HIDDEN), lambda b: (0, 0, 0)),
            pl.BlockSpec((2, NSA_CMP_HIDDEN, HEAD_DIM), lambda b: (0, 0, 0)),
            pl.BlockSpec((4, HEAD_DIM), lambda b: (0, 0)),
        ],
        out_specs=[small_spec, small_t_spec, big_spec, vs_spec, big_spec, vw_spec],
        out_shape=[small, small_t, big, vs_t, big, vw_t],
        compiler_params=_cparams(("parallel",)),
    )(xk, xv, proj3, proj3, pos8, w1.astype(BF16), w2.astype(BF16), g)


def _nsa_kernel(q_ref, gate_ref, kc_ref, vc_ref, ks_ref, vs_ref, kw_ref, vw_ref, o_ref, sel_ref, *, seq):
    i = pl.program_id(1)
    t0 = i * Q_BLOCK
    q_t = q_ref[0].T
    sub = _iota((HEAD_DIM, Q_BLOCK), 0)
    q = jnp.concatenate(
        [jnp.concatenate([q_t[h * HEAD_DIM:(h + 1) * HEAD_DIM],
                          jnp.where(sub == 0, NSA_SLOPES[h] * 256.0, jnp.where(sub == 1, NSA_SLOPES[h], 0.0))],
                         axis=0) for h in range(N_HEADS)], axis=1).astype(BF16)
    tq = t0 + _iota((1, Q_BLOCK), 1)

    def masked(s, ok):
        return jnp.concatenate([jnp.where(ok, s[:, h * Q_BLOCK:(h + 1) * Q_BLOCK], NEG_INF)
                                for h in range(N_HEADS)], axis=1)

    def heads(x):
        return jnp.concatenate([x] * N_HEADS, axis=1)

    ncmp = kc_ref.shape[1]
    cend = _iota((ncmp, 1), 0) * NSA_CMP_STRIDE + (NSA_CMP_LEN - 1)
    s = masked(_dot(kc_ref[0], q), cend <= tq)
    e = jnp.exp(s - jnp.max(s, axis=0, keepdims=True))
    col_ok = heads(jnp.where(tq >= NSA_CMP_LEN - 1, 1.0, 0.0))
    p = e * (col_ok / jnp.sum(e, axis=0, keepdims=True))
    o_cmp = _dot(vc_ref[0], p.astype(BF16))[:HEAD_DIM]

    nsel = seq // NSA_SEL_LEN
    psum = p[:, :Q_BLOCK] + p[:, Q_BLOCK:2 * Q_BLOCK] + p[:, 2 * Q_BLOCK:3 * Q_BLOCK] + p[:, 3 * Q_BLOCK:]
    jj = _iota((nsel, ncmp), 0)
    cc = _iota((nsel, ncmp), 1)
    ratio = NSA_SEL_LEN // NSA_CMP_STRIDE
    cover_t = jnp.where((cc < ratio * jj + ratio) & (cc * NSA_CMP_STRIDE + NSA_CMP_LEN > jj * NSA_SEL_LEN),
                        1.0, 0.0).astype(BF16)
    imp = _dot_rsplit(cover_t, psum)
    jrow = _iota((nsel, Q_BLOCK), 0)
    cur = (t0 + _iota((nsel, Q_BLOCK), 1)) // NSA_SEL_LEN
    forced = (jrow == 0) | (jrow == cur) | (jrow == cur - 1)
    impv = jnp.where(jrow <= cur, imp + jnp.where(forced, FORCE_BONUS, 0.0), NEG_INF)
    rank = jnp.zeros((nsel, Q_BLOCK), F32)
    for ii in range(nsel):
        row = impv[ii:ii + 1, :]
        tie = jnp.where(jrow > ii, 1.0, 0.0)
        rank = rank + jnp.where(row > impv, 1.0, jnp.where(row == impv, tie, 0.0))
    sel_ref[...] = jnp.where((rank < min(NSA_TOPN, nsel)) & (jrow <= cur), 1.0, 0.0)

    kpos_col = _iota((NSA_KCHUNK, 1), 0)
    per_chunk = NSA_KCHUNK // NSA_SEL_LEN

    def raw_scores(c):
        return _dot(ks_ref[0, pl.ds(pl.multiple_of(c * NSA_KCHUNK, NSA_KCHUNK), NSA_KCHUNK), :], q)

    def chunk_mask(c):
        picked = jnp.concatenate(
            [jnp.broadcast_to(sel_ref[pl.ds(c * per_chunk + b, 1), :], (NSA_SEL_LEN, Q_BLOCK))
             for b in range(per_chunk)], axis=0)
        return (picked > 0.5) & (c * NSA_KCHUNK + kpos_col <= tq)

    n_steps = (t0 + Q_BLOCK - 1) // (2 * NSA_KCHUNK) + 1

    def sel_body(it, carry):
        m_i, acc = carry
        sa = masked(raw_scores(2 * it), chunk_mask(2 * it))
        sb = masked(raw_scores(2 * it + 1), chunk_mask(2 * it + 1))
        m_new = jnp.maximum(m_i, jnp.maximum(jnp.max(sa, axis=0, keepdims=True), jnp.max(sb, axis=0, keepdims=True)))
        pa = jnp.exp(sa - m_new).astype(BF16)
        pb = jnp.exp(sb - m_new).astype(BF16)
        return m_new, jnp.exp(m_i - m_new) * acc + _dot(vs_ref[0, 2 * it], pa) + _dot(vs_ref[0, 2 * it + 1], pb)

    cols = N_HEADS * Q_BLOCK
    init = (jnp.full((1, cols), NEG_INF, F32), jnp.zeros((LANE, cols), F32))
    _, acc_s = lax.fori_loop(0, n_steps, sel_body, init)
    o_sel = acc_s[:HEAD_DIM] / acc_s[HEAD_DIM:HEAD_DIM + 1]

    nwb = NSA_WIN // Q_BLOCK + 1
    span = nwb * Q_BLOCK
    b0 = jnp.maximum(i - NSA_WIN // Q_BLOCK, 0)
    start = pl.multiple_of(b0 * Q_BLOCK, Q_BLOCK)
    dw = tq - (start + _iota((span, 1), 0))
    sw = masked(_dot(kw_ref[0, pl.ds(start, span), :], q), (dw >= 0) & (dw < NSA_WIN))
    pw = jnp.exp(sw - jnp.max(sw, axis=0, keepdims=True)).astype(BF16)
    rw = _dot(vw_ref[0, b0], pw[:Q_BLOCK])
    for j in range(1, nwb):
        rw = rw + _dot(vw_ref[0, b0 + j], pw[j * Q_BLOCK:(j + 1) * Q_BLOCK])
    o_win = rw[:HEAD_DIM] / rw[HEAD_DIM:HEAD_DIM + 1]

    gate = _sigmoid(gate_ref[0].T)
    outs = []
    for h in range(N_HEADS):
        c = slice(h * Q_BLOCK, (h + 1) * Q_BLOCK)
        outs.append(gate[3 * h:3 * h + 1] * o_cmp[:, c] + gate[3 * h + 1:3 * h + 2] * o_sel[:, c]
                    + gate[3 * h + 2:3 * h + 3] * o_win[:, c])
    o_ref[0] = jnp.concatenate(outs, axis=0).T


def _nsa_attention(proj3, kc, vc, ks, vs, kw, vw):
    B, S, _ = proj3.shape
    nb = S // Q_BLOCK
    ng = kc.shape[1]
    assert S >= NSA_WIN + Q_BLOCK and S % (2 * NSA_KCHUNK) == 0
    small_spec = pl.BlockSpec((1, ng, LANE), lambda b, i: (b, 0, 0))
    small_t_spec = pl.BlockSpec((1, LANE, ng), lambda b, i: (b, 0, 0))
    big_spec = pl.BlockSpec((1, S, LANE), lambda b, i: (b, 0, 0))
    vs_spec = pl.BlockSpec((1, S // NSA_KCHUNK, LANE, NSA_KCHUNK), lambda b, i: (b, 0, 0, 0))
    vw_spec = pl.BlockSpec((1, S // Q_BLOCK, LANE, Q_BLOCK), lambda b, i: (b, 0, 0, 0))
    return pl.pallas_call(
        functools.partial(_nsa_kernel, seq=S),
        grid=(B, nb),
        in_specs=[
            pl.BlockSpec((1, Q_BLOCK, GROUP_WIDTH), lambda b, i: (b, i, COL_NQ // GROUP_WIDTH)),
            pl.BlockSpec((1, Q_BLOCK, LANE), lambda b, i: (b, i, COL_NG // LANE)),
            small_spec, small_t_spec, big_spec, vs_spec, big_spec, vw_spec,
        ],
        out_specs=pl.BlockSpec((1, Q_BLOCK, GROUP_WIDTH), lambda b, i: (b, i, 0)),
        out_shape=jax.ShapeDtypeStruct((B, S, GROUP_WIDTH), F32),
        scratch_shapes=[pltpu.VMEM((S // NSA_SEL_LEN, Q_BLOCK), F32)],
        compiler_params=_cparams(("parallel", "parallel")),
    )(proj3, proj3, kc, vc, ks, vs, kw, vw)


GLA_BLOCK = 128
GLA_W = N_HEADS * GLA_DK


def _gla_kernel(q_ref, k_ref, v_ref, a_ref, r_ref, wa_ref, ba_ref, g_ref, o_ref, st_ref):
    nsub = GLA_BLOCK // GLA_CHUNK

    @pl.when(pl.program_id(1) == 0)
    def _():
        st_ref[...] = jnp.zeros_like(st_ref)

    q = q_ref[0] * (GLA_DK ** -0.5)
    k = k_ref[0]
    v = v_ref[0]
    z = _dot3(a_ref[0], wa_ref[...]) + ba_ref[...]
    log_a = (jnp.minimum(z, 0.0) - jnp.log(1.0 + jnp.exp(-jnp.abs(z)))) * (1.0 / GLA_TAU)
    ri = _iota((GLA_BLOCK, GLA_BLOCK), 0)
    ci = _iota((GLA_BLOCK, GLA_BLOCK), 1)
    same = (ri // GLA_CHUNK) == (ci // GLA_CHUNK)
    b_loc = _dot_rsplit(jnp.where(same & (ci <= ri), 1.0, 0.0).astype(BF16), log_a)
    b_tot = _dot_rsplit(jnp.where(same, 1.0, 0.0).astype(BF16), log_a)
    qd = q * jnp.exp(b_loc)
    kd = k * jnp.exp(b_tot - b_loc)

    ind_kv = jnp.where(_iota((GLA_W, GROUP_WIDTH), 0) // GLA_DK == _iota((GLA_W, GROUP_WIDTH), 1) // GLA_DV,
                       1.0, 0.0).astype(BF16)
    rsub = _iota((GLA_BLOCK, 1), 0) % GLA_CHUNK
    o = jnp.zeros((GLA_BLOCK, GROUP_WIDTH), F32)
    for s in range(GLA_CHUNK):
        if s == 0:
            ks_, bs_, vs_ = k, b_loc, v
        else:
            ks_, bs_, vs_ = pltpu.roll(k, s, 0), pltpu.roll(b_loc, s, 0), pltpu.roll(v, s, 0)
        dec = jnp.where(rsub >= s, jnp.exp(jnp.minimum(b_loc - bs_, 0.0)), 0.0)
        w = _dot((q * ks_ * dec).astype(BF16), ind_kv)
        o = o + w * vs_

    v_t = v.T.astype(BF16)
    head_ok = _iota((GROUP_WIDTH, GLA_W), 0) // GLA_DV == _iota((GROUP_WIDTH, GLA_W), 1) // GLA_DK
    rblk = _iota((GLA_BLOCK, 1), 0) // GLA_CHUNK
    st = st_ref[...]
    for c in range(nsub):
        in_c = rblk == c
        o = o + _dot_nt(jnp.where(in_c, qd, 0.0).astype(BF16), st.astype(BF16))
        decay = jnp.exp(b_tot[c * GLA_CHUNK:c * GLA_CHUNK + 1, :])
        upd = _dot(v_t, jnp.where(in_c, kd, 0.0).astype(BF16))
        st = st * decay + jnp.where(head_ok, upd, 0.0)
    st_ref[...] = st

    ind_vv = jnp.where(_iota((GROUP_WIDTH, GROUP_WIDTH), 0) // GLA_DV == _iota((GROUP_WIDTH, GROUP_WIDTH), 1) // GLA_DV,
                       1.0, 0.0).astype(BF16)
    ms = _dot_lsplit(o * o, ind_vv) * (1.0 / GLA_DV)
    r = r_ref[0]
    o_ref[0] = o * lax.rsqrt(ms + EPS) * g_ref[...] * (r * _sigmoid(r))


def _gla(proj3, wa2, ba, norm_g):
    B, S, _ = proj3.shape
    nb = S // GLA_BLOCK
    wa_pad = jnp.zeros((LANE, GLA_W), F32).at[:GLA_RANK].set(wa2)
    g_t = jnp.tile(norm_g.reshape(1, GLA_DV), (1, N_HEADS))

    def col(c, w):
        return pl.BlockSpec((1, GLA_BLOCK, w), lambda b, j: (b, j, c // w))

    return pl.pallas_call(
        _gla_kernel,
        grid=(B, nb),
        in_specs=[col(COL_GQ, LANE), col(COL_GK, LANE), col(COL_GV, GROUP_WIDTH), col(COL_GA, LANE),
                  col(COL_GR, GROUP_WIDTH),
                  pl.BlockSpec((LANE, GLA_W), lambda b, j: (0, 0)),
                  pl.BlockSpec((1, GLA_W), lambda b, j: (0, 0)),
                  pl.BlockSpec((1, GROUP_WIDTH), lambda b, j: (0, 0))],
        out_specs=pl.BlockSpec((1, GLA_BLOCK, GROUP_WIDTH), lambda b, j: (b, j, 0)),
        out_shape=jax.ShapeDtypeStruct((B, S, GROUP_WIDTH), F32),
        scratch_shapes=[pltpu.VMEM((GROUP_WIDTH, GLA_W), F32)],
        compiler_params=_cparams(("parallel", "arbitrary")),
    )(proj3, proj3, proj3, proj3, proj3, wa_pad, ba.reshape(1, GLA_W), g_t)


S5_CW = S5_CHUNK * S5_GROUP
S5_SW = 2 * S5_STATE


def _s5_matrices(a_re, a_im, b_re, b_im, c_re, c_im, log_dt):
    hp = lax.Precision.HIGHEST
    L = S5_CHUNK
    dt = jnp.exp(log_dt)[:, None]
    lam_re, lam_im = dt * a_re, dt * a_im
    tau = jnp.arange(L + 1, dtype=F32)[:, None, None]
    mag = jnp.exp(tau * lam_re)
    p_re, p_im = mag * jnp.cos(tau * lam_im), mag * jnp.sin(tau * lam_im)
    den = a_re * a_re + a_im * a_im
    f_re = ((p_re[1] - 1.0) * a_re + p_im[1] * a_im) / den
    f_im = (p_im[1] * a_re - (p_re[1] - 1.0) * a_im) / den
    bb_re = f_re[..., None] * b_re - f_im[..., None] * b_im
    bb_im = f_re[..., None] * b_im + f_im[..., None] * b_re
    pb_re = p_re[..., None] * bb_re - p_im[..., None] * bb_im
    pb_im = p_re[..., None] * bb_im + p_im[..., None] * bb_re
    kern = (jnp.einsum('gon,tgnc->tgoc', c_re, pb_re, precision=hp)
            - jnp.einsum('gon,tgnc->tgoc', c_im, pb_im, precision=hp))
    s_idx = jnp.arange(L)[:, None]
    t_idx = jnp.arange(L)[None, :]
    lag = jnp.clip(t_idx - s_idx, 0, L)
    toep = jnp.where((t_idx >= s_idx)[..., None, None, None], kern[lag], 0.0)
    toep = toep.transpose(2, 0, 4, 1, 3).reshape(S5_GROUPS, S5_CW, S5_CW)
    rev = (L - 1 - jnp.arange(L))
    bin_re = pb_re[rev].transpose(1, 0, 3, 2).reshape(S5_GROUPS, S5_CW, S5_STATE)
    bin_im = pb_im[rev].transpose(1, 0, 3, 2).reshape(S5_GROUPS, S5_CW, S5_STATE)
    b_in = jnp.concatenate([bin_re, bin_im], axis=-1)
    pt_re, pt_im = p_re[1:], p_im[1:]
    co_re = c_re[None] * pt_re[:, :, None, :] - c_im[None] * pt_im[:, :, None, :]
    co_im = -c_re[None] * pt_im[:, :, None, :] - c_im[None] * pt_re[:, :, None, :]
    c_out = jnp.concatenate([co_re.transpose(1, 3, 0, 2), co_im.transpose(1, 3, 0, 2)], axis=1)
    c_out = c_out.reshape(S5_GROUPS, S5_SW, S5_CW)
    a_chunk = jnp.stack([p_re[L].reshape(1, -1), p_im[L].reshape(1, -1)], axis=0)
    return jnp.concatenate([toep, b_in], axis=-1).astype(BF16), c_out.astype(BF16), a_chunk


def _s5_in_kernel(u_ref, tb_ref, y_ref, v_ref):
    r = _dot(u_ref[0].astype(BF16), tb_ref[0])
    y_ref[0] = r[:, :S5_CW]
    v_ref[0] = r[:, S5_CW:]


def _s5_scan_kernel(v_ref, a_ref, x_ref):
    nb = v_ref.shape[2]
    a_r = jnp.broadcast_to(a_ref[0], (nb, a_ref.shape[2]))
    a_i = jnp.broadcast_to(a_ref[1], (nb, a_ref.shape[2]))

    def body(m, carry):
        x_r, x_i = carry
        x_ref[m, 0] = x_r
        x_ref[m, 1] = x_i
        return a_r * x_r - a_i * x_i + v_ref[m, 0], a_r * x_i + a_i * x_r + v_ref[m, 1]

    zero = jnp.zeros(a_r.shape, F32)
    lax.fori_loop(0, v_ref.shape[0], body, (zero, zero))


def _s5_out_kernel(y_ref, x_ref, c_ref, o_ref):
    o_ref[0] = y_ref[0] + _dot(x_ref[0].astype(BF16), c_ref[0])


def _s5_linear(u3, a_re, a_im, b_re, b_im, c_re, c_im, log_dt):
    B, S, _ = u3.shape
    M = S // S5_CHUNK
    G = S5_GROUPS
    tb, c_out, a_chunk = _s5_matrices(a_re, a_im, b_re, b_im, c_re, c_im, log_dt)
    ug = u3.reshape(B, M, S5_CHUNK, G, S5_GROUP).transpose(3, 0, 1, 2, 4).reshape(G, B * M, S5_CW)
    y_in, v = pl.pallas_call(
        _s5_in_kernel,
        grid=(G,),
        in_specs=[pl.BlockSpec((1, B * M, S5_CW), lambda g: (g, 0, 0)),
                  pl.BlockSpec((1, S5_CW, S5_CW + S5_SW), lambda g: (g, 0, 0))],
        out_specs=[pl.BlockSpec((1, B * M, S5_CW), lambda g: (g, 0, 0)),
                   pl.BlockSpec((1, B * M, S5_SW), lambda g: (g, 0, 0))],
        out_shape=[jax.ShapeDtypeStruct((G, B * M, S5_CW), F32),
                   jax.ShapeDtypeStruct((G, B * M, S5_SW), F32)],
        compiler_params=_cparams(("parallel",)),
    )(ug, tb)
    gn = G * S5_STATE
    v_t = v.reshape(G, B, M, 2, S5_STATE).transpose(2, 3, 1, 0, 4).reshape(M, 2, B, gn)
    lw = 256
    x_t = pl.pallas_call(
        _s5_scan_kernel,
        grid=(gn // lw,),
        in_specs=[pl.BlockSpec((M, 2, B, lw), lambda j: (0, 0, 0, j)),
                  pl.BlockSpec((2, 1, lw), lambda j: (0, 0, j))],
        out_specs=pl.BlockSpec((M, 2, B, lw), lambda j: (0, 0, 0, j)),
        out_shape=jax.ShapeDtypeStruct((M, 2, B, gn), F32),
        compiler_params=_cparams(("parallel",)),
    )(v_t, a_chunk)
    xg = x_t.reshape(M, 2, B, G, S5_STATE).transpose(3, 2, 0, 1, 4).reshape(G, B * M, S5_SW)
    y = pl.pallas_call(
        _s5_out_kernel,
        grid=(G,),
        in_specs=[pl.BlockSpec((1, B * M, S5_CW), lambda g: (g, 0, 0)),
                  pl.BlockSpec((1, B * M, S5_SW), lambda g: (g, 0, 0)),
                  pl.BlockSpec((1, S5_SW, S5_CW), lambda g: (g, 0, 0))],
        out_specs=pl.BlockSpec((1, B * M, S5_CW), lambda g: (g, 0, 0)),
        out_shape=jax.ShapeDtypeStruct((G, B * M, S5_CW), F32),
        compiler_params=_cparams(("parallel",)),
    )(y_in, xg, c_out)
    return y.reshape(G, B, M, S5_CHUNK, S5_GROUP).transpose(1, 2, 3, 0, 4).reshape(B, S, GROUP_WIDTH)


def _mixout_kernel(dil_ref, nsa_ref, gla_ref, s5y_ref, u_ref, x_ref,
                   s5d_ref, gw_ref, gb_ref, ng_ref, wo_ref, o_ref):
    u = u_ref[...]
    hg = _gelu(s5y_ref[...] + s5d_ref[...] * u)
    y_s5 = hg * _sigmoid(_dot(hg.astype(BF16), gw_ref[...]) + gb_ref[...])
    parts = [dil_ref[...], nsa_ref[...], gla_ref[...], y_s5]
    mix = jnp.concatenate([(_rms(parts[n]) * ng_ref[n:n + 1, :]).astype(BF16) for n in range(4)], axis=1)
    o_ref[...] = x_ref[...] + _dot(mix, wo_ref[...])


def _mixout(y_dil, y_nsa, y_gla, y_s5, proj, x2, s5_d, glu_w, glu_b, out_g, w_out):
    T = x2.shape[0]
    tm = 512

    def row(w, c=0):
        return pl.BlockSpec((tm, w), lambda i: (i, c))

    def const(shape):
        return pl.BlockSpec(shape, lambda i: (0,) * len(shape))

    return pl.pallas_call(
        _mixout_kernel,
        grid=(T // tm,),
        in_specs=[row(GROUP_WIDTH), row(GROUP_WIDTH), row(GROUP_WIDTH),
                  row(GROUP_WIDTH), row(GROUP_WIDTH, COL_SU // GROUP_WIDTH), row(D_MODEL),
                  const((1, GROUP_WIDTH)), const((GROUP_WIDTH, GROUP_WIDTH)), const((1, GROUP_WIDTH)),
                  const((4, GROUP_WIDTH)), const((D_MODEL, D_MODEL))],
        out_specs=row(D_MODEL),
        out_shape=jax.ShapeDtypeStruct((T, D_MODEL), F32),
        compiler_params=_cparams(("parallel",)),
    )(y_dil, y_nsa, y_gla, y_s5, proj, x2, s5_d.reshape(1, -1), glu_w.astype(BF16),
      glu_b.reshape(1, -1), out_g.reshape(4, GROUP_WIDTH), w_out.astype(BF16))


def _swiglu_chunks(h, wg_ref, wu_ref, wd_ref, acc, width, chunk, lead=()):
    for c in range(width // chunk):
        cs = slice(c * chunk, (c + 1) * chunk)
        a = _dot(h, wg_ref[lead + (slice(None), cs)])
        u = _dot(h, wu_ref[lead + (slice(None), cs)])
        acc = acc + _dot((a * _sigmoid(a) * u).astype(BF16), wd_ref[lead + (cs, slice(None))])
    return acc


def _ffn_kernel(x_ref, g_ref, wg_ref, wu_ref, wd_ref, o_ref):
    x = x_ref[...]
    h = (_rms(x) * g_ref[...]).astype(BF16)
    o_ref[...] = _swiglu_chunks(h, wg_ref, wu_ref, wd_ref, x, D_FF_PAD, D_FF_PAD // 2)


def _resident(shape, index_map):
    return pl.BlockSpec(shape, index_map, pipeline_mode=pl.Buffered(1))


def _ffn(x2, g, wg, wu, wd):
    T = x2.shape[0]
    tm = 512
    pad = D_FF_PAD - D_FF
    wg_p = jnp.pad(wg, ((0, 0), (0, pad))).astype(BF16)
    wu_p = jnp.pad(wu, ((0, 0), (0, pad))).astype(BF16)
    wd_p = jnp.pad(wd, ((0, pad), (0, 0))).astype(BF16)
    return pl.pallas_call(
        _ffn_kernel,
        grid=(T // tm,),
        in_specs=[pl.BlockSpec((tm, D_MODEL), lambda i: (i, 0)),
                  pl.BlockSpec((1, D_MODEL), lambda i: (0, 0)),
                  _resident((D_MODEL, D_FF_PAD), lambda i: (0, 0)),
                  _resident((D_MODEL, D_FF_PAD), lambda i: (0, 0)),
                  _resident((D_FF_PAD, D_MODEL), lambda i: (0, 0))],
        out_specs=pl.BlockSpec((tm, D_MODEL), lambda i: (i, 0)),
        out_shape=jax.ShapeDtypeStruct((T, D_MODEL), F32),
        compiler_params=_cparams(("parallel",)),
    )(x2, g.reshape(1, D_MODEL), wg_p, wu_p, wd_p)


MOE_TILE = 512
DSP_TILE = 256
CMB_TILE = 256


def _router_kernel(x_ref, g_ref, w_ref, b_ref, r_ref):
    h = _rms(x_ref[...]) * g_ref[...]
    lane = _iota((h.shape[0], LANE), 1)
    logits = jnp.where(lane < N_EXPERTS, _dot3(h, w_ref[...]) + b_ref[...], -jnp.inf)
    m1 = jnp.max(logits, axis=-1, keepdims=True)
    i1 = jnp.min(jnp.where(logits == m1, lane, LANE), axis=-1, keepdims=True)
    rest = jnp.where(lane == i1, -jnp.inf, logits)
    m2 = jnp.max(rest, axis=-1, keepdims=True)
    i2 = jnp.min(jnp.where(rest == m2, lane, LANE), axis=-1, keepdims=True)
    e2 = jnp.exp(m2 - m1)
    w1 = 1.0 / (1.0 + e2)
    w2 = e2 / (1.0 + e2)
    r_ref[...] = jnp.where(lane == 0, i1.astype(F32),
                           jnp.where(lane == 1, i2.astype(F32),
                                     jnp.where(lane == 2, w1, jnp.where(lane == 3, w2, 0.0))))


def _router(x2, g, rw, rb):
    T = x2.shape[0]
    tm = 512
    rw_p = jnp.pad(rw, ((0, 0), (0, LANE - N_EXPERTS)))
    rb_p = jnp.pad(rb, (0, LANE - N_EXPERTS)).reshape(1, LANE)
    return pl.pallas_call(
        _router_kernel,
        grid=(T // tm,),
        in_specs=[pl.BlockSpec((tm, D_MODEL), lambda i: (i, 0)),
                  pl.BlockSpec((1, D_MODEL), lambda i: (0, 0)),
                  pl.BlockSpec((D_MODEL, LANE), lambda i: (0, 0)),
                  pl.BlockSpec((1, LANE), lambda i: (0, 0))],
        out_specs=pl.BlockSpec((tm, LANE), lambda i: (i, 0)),
        out_shape=jax.ShapeDtypeStruct((T, LANE), F32),
        compiler_params=_cparams(("parallel",)),
    )(x2, g.reshape(1, D_MODEL), rw_p, rb_p)


def _smem_rows(width, nt, shift):
    return pl.BlockSpec((1, 1, width), lambda i, *_: (jnp.clip(i + shift, 0, nt - 1), 0, 0),
                        memory_space=pltpu.SMEM)


def _scatter_copy(buf, slot, r, dst_hbm, row, sem, k):
    return pltpu.make_async_copy(buf.at[slot, pl.ds(r, 1), :], dst_hbm.at[pl.ds(row, 1), :], sem.at[slot, k, r])


def _dispatch_kernel(p1_ref, p2_ref, q1_ref, q2_ref, x_ref, g_ref, xs_in, xs_hbm, hbuf, sem):
    del xs_in
    i = pl.program_id(0)
    nt = pl.num_programs(0)
    slot = i % 2
    hbuf[slot] = _rms(x_ref[...]) * g_ref[...]
    for r in range(DSP_TILE):
        _scatter_copy(hbuf, slot, r, xs_hbm, p1_ref[0, 0, r], sem, 0).start()
        _scatter_copy(hbuf, slot, r, xs_hbm, p2_ref[0, 0, r], sem, 1).start()

    def wait_all(pa, pb, s):
        for r in range(DSP_TILE):
            _scatter_copy(hbuf, s, r, xs_hbm, pa[0, 0, r], sem, 0).wait()
            _scatter_copy(hbuf, s, r, xs_hbm, pb[0, 0, r], sem, 1).wait()

    @pl.when(i > 0)
    def _():
        wait_all(q1_ref, q2_ref, 1 - slot)

    @pl.when(i == nt - 1)
    def _():
        wait_all(p1_ref, p2_ref, slot)


def _moe_dispatch(x2, g, pos, n_rows):
    T = x2.shape[0]
    nt = T // DSP_TILE
    p1 = pos[:, 0].reshape(nt, 1, DSP_TILE)
    p2 = pos[:, 1].reshape(nt, 1, DSP_TILE)
    return pl.pallas_call(
        _dispatch_kernel,
        grid=(nt,),
        in_specs=[_smem_rows(DSP_TILE, nt, 0), _smem_rows(DSP_TILE, nt, 0),
                  _smem_rows(DSP_TILE, nt, -1), _smem_rows(DSP_TILE, nt, -1),
                  pl.BlockSpec((DSP_TILE, D_MODEL), lambda i: (i, 0)),
                  pl.BlockSpec((1, D_MODEL), lambda i: (0, 0)),
                  pl.BlockSpec(memory_space=pl.ANY)],
        out_specs=pl.BlockSpec(memory_space=pl.ANY),
        out_shape=jax.ShapeDtypeStruct((n_rows, D_MODEL), F32),
        scratch_shapes=[pltpu.VMEM((2, DSP_TILE, D_MODEL), F32), pltpu.SemaphoreType.DMA((2, 2, DSP_TILE))],
        input_output_aliases={6: 0},
        compiler_params=_cparams(("arbitrary",)),
    )(p1, p2, p1, p2, x2, g.reshape(1, D_MODEL), jnp.zeros((n_rows, D_MODEL), F32))


def _moe_kernel(texp_ref, x_ref, wg_ref, wu_ref, wd_ref, y_ref):
    y_ref[...] = _swiglu_chunks(x_ref[...].astype(BF16), wg_ref, wu_ref, wd_ref,
                                jnp.zeros((MOE_TILE, D_MODEL), F32), D_FF_EXPERT, D_FF_EXPERT // 4, lead=(0,))


def _moe_experts(xs, tile_expert, wg, wu, wd):
    nt = tile_expert.shape[0]
    grid_spec = pltpu.PrefetchScalarGridSpec(
        num_scalar_prefetch=1,
        grid=(nt,),
        in_specs=[
            pl.BlockSpec((MOE_TILE, D_MODEL), lambda i, te: (i, 0)),
            _resident((1, D_MODEL, D_FF_EXPERT), lambda i, te: (te[i], 0, 0)),
            _resident((1, D_MODEL, D_FF_EXPERT), lambda i, te: (te[i], 0, 0)),
            _resident((1, D_FF_EXPERT, D_MODEL), lambda i, te: (te[i], 0, 0)),
        ],
        out_specs=pl.BlockSpec((MOE_TILE, D_MODEL), lambda i, te: (i, 0)),
    )
    return pl.pallas_call(
        _moe_kernel,
        grid_spec=grid_spec,
        out_shape=jax.ShapeDtypeStruct((nt * MOE_TILE, D_MODEL), F32),
        compiler_params=_cparams(("arbitrary",)),
    )(tile_expert, xs, wg.astype(BF16), wu.astype(BF16), wd.astype(BF16))


def _gather_copy(src_hbm, row, buf, slot, k, r, sem):
    return pltpu.make_async_copy(src_hbm.at[pl.ds(row, 1), :], buf.at[slot, k, pl.ds(r, 1), :], sem.at[slot, k, r])


def _combine_kernel(p1_ref, p2_ref, n1_ref, n2_ref, y_hbm, x_ref, r_ref, o_ref, ybuf, sem):
    i = pl.program_id(0)
    nt = pl.num_programs(0)
    slot = i % 2

    def issue(pa, pb, s):
        for r in range(CMB_TILE):
            _gather_copy(y_hbm, pa[0, 0, r], ybuf, s, 0, r, sem).start()
            _gather_copy(y_hbm, pb[0, 0, r], ybuf, s, 1, r, sem).start()

    @pl.when(i == 0)
    def _():
        issue(p1_ref, p2_ref, 0)

    @pl.when(i + 1 < nt)
    def _():
        issue(n1_ref, n2_ref, 1 - slot)

    for r in range(CMB_TILE):
        _gather_copy(y_hbm, p1_ref[0, 0, r], ybuf, slot, 0, r, sem).wait()
        _gather_copy(y_hbm, p2_ref[0, 0, r], ybuf, slot, 1, r, sem).wait()
    r = r_ref[...]
    o_ref[...] = x_ref[...] + r[:, 2:3] * ybuf[slot, 0] + r[:, 3:4] * ybuf[slot, 1]


def _moe_combine(y_sorted, pos, x2, route):
    T = x2.shape[0]
    nt = T // CMB_TILE
    p1 = pos[:, 0].reshape(nt, 1, CMB_TILE)
    p2 = pos[:, 1].reshape(nt, 1, CMB_TILE)
    return pl.pallas_call(
        _combine_kernel,
        grid=(nt,),
        in_specs=[_smem_rows(CMB_TILE, nt, 0), _smem_rows(CMB_TILE, nt, 0),
                  _smem_rows(CMB_TILE, nt, 1), _smem_rows(CMB_TILE, nt, 1),
                  pl.BlockSpec(memory_space=pl.ANY),
                  pl.BlockSpec((CMB_TILE, D_MODEL), lambda i: (i, 0)),
                  pl.BlockSpec((CMB_TILE, LANE), lambda i: (i, 0))],
        out_specs=pl.BlockSpec((CMB_TILE, D_MODEL), lambda i: (i, 0)),
        out_shape=jax.ShapeDtypeStruct((T, D_MODEL), F32),
        scratch_shapes=[pltpu.VMEM((2, 2, CMB_TILE, D_MODEL), F32),
                        pltpu.SemaphoreType.DMA((2, 2, CMB_TILE))],
        compiler_params=_cparams(("arbitrary",)),
    )(p1, p2, p1, p2, y_sorted, x2, route)


def _moe(x2, g, rw, rb, wg, wu, wd):
    T = x2.shape[0]
    route = _router(x2, g, rw, rb)
    experts = route[:, :2].astype(jnp.int32)
    onehot = (experts.reshape(-1)[:, None] == jnp.arange(N_EXPERTS)[None, :]).astype(jnp.int32)
    csum = jnp.cumsum(onehot, axis=0)
    counts = csum[-1]
    rank = jnp.sum(csum * onehot, axis=1) - 1
    padded = ((counts + MOE_TILE - 1) // MOE_TILE) * MOE_TILE
    ends = jnp.cumsum(padded)
    starts = ends - padded
    pos = (jnp.sum(starts[None, :] * onehot, axis=1) + rank).reshape(T, 2)
    nt = (2 * T) // MOE_TILE + N_EXPERTS
    tile_start = jnp.arange(nt, dtype=jnp.int32) * MOE_TILE
    tile_expert = jnp.minimum(jnp.sum(tile_start[:, None] >= ends[None, :], axis=1), N_EXPERTS - 1)
    xs = _moe_dispatch(x2, g, pos, nt * MOE_TILE)
    y_sorted = _moe_experts(xs, tile_expert.astype(jnp.int32), wg, wu, wd)
    return _moe_combine(y_sorted, pos, x2, route)


def _layer_mixers(x2, B, S, l, p):
    proj = _inproj(x2, p["norm1_g"][l], _pad_w_in(p["w_in"][l]), _head_gains(p["dil_qk_g"][l], p["nsa_qk_g"][l]))
    proj3 = proj.reshape(B, S, PROJ_PAD)
    y_dil = _dil_attention(proj3).reshape(B * S, GROUP_WIDTH)
    kc, vc, ks, vs, kw, vw = _nsa_prep(proj3, p["nsa_cmp_pos"][l], p["nsa_cmp_w1"][l], p["nsa_cmp_w2"][l],
                                       p["nsa_qk_g"][l])
    y_nsa = _nsa_attention(proj3, kc, vc, ks, vs, kw, vw).reshape(B * S, GROUP_WIDTH)
    y_gla = _gla(proj3, p["gla_wa2"][l], p["gla_ba"][l], p["gla_norm_g"][l]).reshape(B * S, GROUP_WIDTH)
    y_s5 = _s5_linear(proj3[:, :, COL_SU:COL_SU + GROUP_WIDTH], p["s5_a_re"][l], p["s5_a_im"][l],
                      p["s5_b_re"][l], p["s5_b_im"][l], p["s5_c_re"][l], p["s5_c_im"][l],
                      p["s5_log_dt"][l]).reshape(B * S, GROUP_WIDTH)
    return _mixout(y_dil, y_nsa, y_gla, y_s5, proj, x2, p["s5_d"][l], p["s5_glu_w"][l],
                   p["s5_glu_b"][l], p["out_norm_g"][l], p["w_out"][l])


def kernel(x, norm1_g, w_in, dil_qk_g, nsa_qk_g, nsa_cmp_pos, nsa_cmp_w1, nsa_cmp_w2, gla_wa2, gla_ba,
           gla_norm_g, s5_a_re, s5_a_im, s5_b_re, s5_b_im, s5_c_re, s5_c_im, s5_d, s5_log_dt, s5_glu_w,
           s5_glu_b, out_norm_g, w_out, norm2_g, ffn_w_gate, ffn_w_up, ffn_w_down, moe_router_w,
           moe_router_b, moe_w_gate, moe_w_up, moe_w_down):
    B, S, D = x.shape
    p = dict(norm1_g=norm1_g, w_in=w_in, dil_qk_g=dil_qk_g, nsa_qk_g=nsa_qk_g, nsa_cmp_pos=nsa_cmp_pos,
             nsa_cmp_w1=nsa_cmp_w1, nsa_cmp_w2=nsa_cmp_w2, gla_wa2=gla_wa2, gla_ba=gla_ba,
             gla_norm_g=gla_norm_g, s5_a_re=s5_a_re, s5_a_im=s5_a_im, s5_b_re=s5_b_re, s5_b_im=s5_b_im,
             s5_c_re=s5_c_re, s5_c_im=s5_c_im, s5_d=s5_d, s5_log_dt=s5_log_dt, s5_glu_w=s5_glu_w,
             s5_glu_b=s5_glu_b, out_norm_g=out_norm_g, w_out=w_out)
    x2 = x.reshape(B * S, D)
    depth = norm1_g.shape[0]
    for l in range(depth):
        x2 = _layer_mixers(x2, B, S, l, p)
        i = l // 2
        if l % 2 == 0:
            x2 = _ffn(x2, norm2_g[l], ffn_w_gate[i], ffn_w_up[i], ffn_w_down[i])
        else:
            x2 = _moe(x2, norm2_g[l], moe_router_w[i], moe_router_b[i], moe_w_gate[i], moe_w_up[i],
                      moe_w_down[i])
    return x2.reshape(B, S, D)
```

```python
import functools
import math

import jax
import jax.numpy as jnp
from jax import lax
from jax.experimental import pallas as pl
from jax.experimental.pallas import tpu as pltpu

F32 = jnp.float32
BF16 = jnp.bfloat16

D_MODEL = 1024
HEAD_DIM = 64
GROUP_WIDTH = 256
N_HEADS = 4
DIL_PATTERNS = ((128, 1), (512, 4), (2048, 16))
Q_BLOCK = 128
NSA_CMP_LEN = 32
NSA_CMP_STRIDE = 16
NSA_CMP_HIDDEN = 256
NSA_SEL_LEN = 64
NSA_TOPN = 16
NSA_WIN = 512
FORCE_BONUS = 1.0e4
GLA_DK = 32
GLA_DV = 64
GLA_RANK = 16
GLA_TAU = 16.0
GLA_CHUNK = 16
S5_GROUP = 16
S5_GROUPS = 16
S5_STATE = 64
S5_CHUNK = 16
D_FF = 2752
N_EXPERTS = 8
D_FF_EXPERT = 3584
EPS = 1e-6
NEG_INF = -1e30

DIL_SLOPES = tuple(2.0 ** (-float(i)) for i in (2, 4, 6, 8))
NSA_SLOPES = tuple(2.0 ** (-float(i)) for i in (1, 3, 5, 7))

COL_DQ, COL_DK, COL_DV, COL_NQ = 0, 256, 512, 768
COL_NCMP, COL_NK, COL_NV, COL_NG = 1024, 1152, 1280, 1408
COL_GQ, COL_GK, COL_GV, COL_GA, COL_GR, COL_SU = 1536, 1664, 1792, 2048, 2304, 2560
PROJ_PAD = 2816
LANE = 128
D_FF_PAD = 2816

VMEM_LIMIT = 56 * 1024 * 1024


def _cparams(sem):
    return pltpu.CompilerParams(dimension_semantics=sem, vmem_limit_bytes=VMEM_LIMIT)


def _rms(x):
    return x * lax.rsqrt(jnp.mean(x * x, axis=-1, keepdims=True) + EPS)


def _dot(a, b):
    return jnp.dot(a, b, preferred_element_type=F32)


def _dot_nt(a, b):
    return lax.dot_general(a, b, (((1,), (1,)), ((), ())), preferred_element_type=F32)


def _split(x):
    hi = x.astype(BF16)
    lo = (x - hi.astype(F32)).astype(BF16)
    return hi, lo


def _dot_lsplit(x, m):
    hi, lo = _split(x)
    return _dot(hi, m) + _dot(lo, m)


def _dot_rsplit(m, x):
    hi, lo = _split(x)
    return _dot(m, hi) + _dot(m, lo)


def _dot3(a, b):
    ah, al = _split(a)
    bh, bl = _split(b)
    return _dot(ah, bh) + _dot(ah, bl) + _dot(al, bh)


def _gelu(x):
    return 0.5 * x * (1.0 + jnp.tanh(math.sqrt(2.0 / math.pi) * (x + 0.044715 * (x * x * x))))


def _sigmoid(x):
    return 1.0 / (1.0 + jnp.exp(-x))


def _iota(shape, dim):
    return lax.broadcasted_iota(jnp.int32, shape, dim)


def _head_indicator(rows, cols, rdiv, cdiv):
    return jnp.where(_iota((rows, cols), 0) // rdiv == _iota((rows, cols), 1) // cdiv, 1.0, 0.0).astype(BF16)


def _inproj_kernel(x_ref, g_ref, w_ref, hg_ref, o_ref):
    h = _rms(x_ref[...]) * g_ref[...]
    o_ref[...] = _dot(h.astype(BF16), w_ref[...])
    ind = _head_indicator(GROUP_WIDTH, GROUP_WIDTH, HEAD_DIM, HEAD_DIM)
    for lo, width in NORM_SEGMENTS:
        v = o_ref[:, lo:lo + width]
        ms = _dot_lsplit(v * v, ind[:width, :width]) * (1.0 / HEAD_DIM)
        o_ref[:, lo:lo + width] = v * lax.rsqrt(ms + EPS) * hg_ref[:, lo:lo + width]


NORM_COLS = COL_NV
NORM_SEGMENTS = ((COL_DQ, GROUP_WIDTH), (COL_DK, GROUP_WIDTH), (COL_NQ, GROUP_WIDTH), (COL_NK, LANE))


def _head_gains(dil_g, nsa_g):
    scale = HEAD_DIM ** -0.5
    z = jnp.zeros((HEAD_DIM,), F32)
    row = jnp.concatenate([jnp.tile(dil_g[0] * scale, N_HEADS), jnp.tile(dil_g[1], N_HEADS),
                           jnp.zeros((GROUP_WIDTH,), F32), jnp.tile(nsa_g[0] * scale, N_HEADS),
                           z, z, nsa_g[2], nsa_g[3]])
    return row.reshape(1, NORM_COLS)


def _inproj(x2, g, w_pad, head_gains):
    T = x2.shape[0]
    tm = 512
    return pl.pallas_call(
        _inproj_kernel,
        grid=(T // tm,),
        in_specs=[
            pl.BlockSpec((tm, D_MODEL), lambda i: (i, 0)),
            pl.BlockSpec((1, D_MODEL), lambda i: (0, 0)),
            pl.BlockSpec((D_MODEL, PROJ_PAD), lambda i: (0, 0)),
            pl.BlockSpec((1, NORM_COLS), lambda i: (0, 0)),
        ],
        out_specs=pl.BlockSpec((tm, PROJ_PAD), lambda i: (i, 0)),
        out_shape=jax.ShapeDtypeStruct((T, PROJ_PAD), F32),
        compiler_params=_cparams(("parallel",)),
    )(x2, g.reshape(1, D_MODEL), w_pad, head_gains)


def _pad_w_in(w):
    def seg(lo, width, pad_to):
        s = w[:, lo:lo + width]
        if pad_to > width:
            s = jnp.pad(s, ((0, 0), (0, pad_to - width)))
        return s
    k_cmp, v_cmp = seg(1024, 64, 64), seg(1088, 64, 64)
    k_slc, v_slc = seg(1152, 64, 64), seg(1216, 64, 64)
    k_win, v_win = seg(1280, 64, 64), seg(1344, 64, 64)
    pieces = [
        seg(0, 256, 256), seg(256, 256, 256), seg(512, 256, 256), seg(768, 256, 256),
        k_cmp, v_cmp, k_slc, k_win, v_slc, v_win,
        seg(1408, 12, 128),
        seg(1420, 128, 128), seg(1548, 128, 128), seg(1676, 256, 256),
        seg(1932, 16, 256),
        seg(1948, 256, 256), seg(2204, 256, 256),
    ]
    out = jnp.concatenate(pieces, axis=1)
    assert out.shape[1] == PROJ_PAD
    return out.astype(BF16)


DIL_SB = Q_BLOCK * max(d for _, d in DIL_PATTERNS)
DIL_SUBSTEPS = DIL_SB // Q_BLOCK


def _dil_kernel(q0_ref, q1_ref, k0_ref, k1_ref, v0_ref, v1_ref, o_ref,
                ks_ref, vs_ref, bias_ref, m_ref, l_ref, acc_ref):
    i = pl.program_id(1)
    rows4 = N_HEADS * Q_BLOCK

    def ld2(ref, rows):
        return jnp.concatenate([ref[0, rows, :], ref[1, rows, :]], axis=1)

    def st2(ref, rows, val):
        ref[0, rows, :] = val[:, :LANE]
        ref[1, rows, :] = val[:, LANE:]

    @pl.when(i == 0)
    def _():
        ks_ref[:, 0:DIL_SB, :] = jnp.zeros((2, DIL_SB, LANE), F32)
        vs_ref[:, 0:DIL_SB, :] = jnp.zeros((2, DIL_SB, LANE), F32)
        ri = _iota((rows4, 2 * Q_BLOCK), 0)
        ci = _iota((rows4, 2 * Q_BLOCK), 1)
        delta = Q_BLOCK + ri % Q_BLOCK - ci
        head = ri // Q_BLOCK
        slope = jnp.where(head == 0, DIL_SLOPES[0],
                          jnp.where(head == 1, DIL_SLOPES[1],
                                    jnp.where(head == 2, DIL_SLOPES[2], DIL_SLOPES[3]))).astype(F32)
        band = (delta >= 0) & (delta <= Q_BLOCK)
        for p, (_, dil) in enumerate(DIL_PATTERNS):
            b = jnp.where(band, -(slope * dil) * delta.astype(F32), NEG_INF)
            bias_ref[p, 0] = b
            bias_ref[p, 1] = jnp.where(ci >= Q_BLOCK, b, NEG_INF)

    @pl.when(i > 0)
    def _():
        ks_ref[:, 0:DIL_SB, :] = ks_ref[:, DIL_SB:, :]
        vs_ref[:, 0:DIL_SB, :] = vs_ref[:, DIL_SB:, :]

    ks_ref[0, DIL_SB:, :] = k0_ref[0]
    ks_ref[1, DIL_SB:, :] = k1_ref[0]
    vs_ref[0, DIL_SB:, :] = v0_ref[0]
    vs_ref[1, DIL_SB:, :] = v1_ref[0]
    lane_head = _iota((Q_BLOCK, GROUP_WIDTH), 1) // HEAD_DIM

    def spread(x):
        out = jnp.zeros((Q_BLOCK, GROUP_WIDTH), F32)
        for h in range(N_HEADS):
            out = jnp.where(lane_head == h, x[h * Q_BLOCK:(h + 1) * Q_BLOCK], out)
        return out

    order = sorted(range(len(DIL_PATTERNS)), key=lambda n: -DIL_PATTERNS[n][1])
    assert DIL_PATTERNS[order[-1]][1] == 1
    for p in order:
        dil = DIL_PATTERNS[p][1]

        def sub(j, carry, p=p, dil=dil):
            start = (j // dil) * (Q_BLOCK * dil) + j % dil
            rows = pl.ds(start, Q_BLOCK, stride=dil)
            cur = pl.ds(DIL_SB + start, Q_BLOCK, stride=dil)
            prev = pl.ds(DIL_SB + start - Q_BLOCK * dil, Q_BLOCK, stride=dil)
            q = jnp.concatenate([q0_ref[0, rows, :], q1_ref[0, rows, :]], axis=1)
            q4 = jnp.concatenate([jnp.where(lane_head == h, q, 0.0) for h in range(N_HEADS)], axis=0)
            kk = jnp.concatenate([ld2(ks_ref, prev), ld2(ks_ref, cur)], axis=0).astype(BF16)
            vv = jnp.concatenate([ld2(vs_ref, prev), ld2(vs_ref, cur)], axis=0).astype(BF16)
            first = jnp.where((i == 0) & (j < dil), 1, 0)
            s = _dot_nt(q4.astype(BF16), kk) + bias_ref[p, first]
            m = jnp.max(s, axis=-1, keepdims=True)
            e = jnp.exp(s - m)
            l = jnp.sum(e, axis=-1, keepdims=True)
            o_t = spread(_dot(e.astype(BF16), vv))
            m_t = spread(m)
            l_t = spread(l)
            if p == order[0]:
                st2(m_ref, rows, m_t)
                st2(l_ref, rows, l_t)
                st2(acc_ref, rows, o_t)
                return carry
            m_old = ld2(m_ref, rows)
            m_new = jnp.maximum(m_old, m_t)
            a = jnp.exp(m_old - m_new)
            b = jnp.exp(m_t - m_new)
            l_new = a * ld2(l_ref, rows) + b * l_t
            acc_new = a * ld2(acc_ref, rows) + b * o_t
            if p == order[-1]:
                o_ref[0, pl.ds(pl.multiple_of(start, Q_BLOCK), Q_BLOCK), :] = acc_new / l_new
            else:
                st2(m_ref, rows, m_new)
                st2(l_ref, rows, l_new)
                st2(acc_ref, rows, acc_new)
            return carry
        lax.fori_loop(0, DIL_SUBSTEPS, sub, 0)


def _dil_attention(proj3):
    B, S, _ = proj3.shape
    assert all(win // d == Q_BLOCK for win, d in DIL_PATTERNS) and S % DIL_SB == 0

    def halves(c):
        return [pl.BlockSpec((1, DIL_SB, LANE), lambda b, i, c=c, h=h: (b, i, c // LANE + h)) for h in range(2)]

    sb = pltpu.VMEM((2, DIL_SB, LANE), F32)
    sb2 = pltpu.VMEM((2, 2 * DIL_SB, LANE), F32)
    return pl.pallas_call(
        _dil_kernel,
        grid=(B, S // DIL_SB),
        in_specs=halves(COL_DQ) + halves(COL_DK) + halves(COL_DV),
        out_specs=pl.BlockSpec((1, DIL_SB, GROUP_WIDTH), lambda b, i: (b, i, 0)),
        out_shape=jax.ShapeDtypeStruct((B, S, GROUP_WIDTH), F32),
        scratch_shapes=[sb2, sb2, pltpu.VMEM((len(DIL_PATTERNS), 2, N_HEADS * Q_BLOCK, 2 * Q_BLOCK), F32),
                        sb, sb, sb],
        compiler_params=_cparams(("parallel", "arbitrary")),
    )(*([proj3] * 6))


def _nsa_prep_kernel(xk_ref, xv_ref, kraw_ref, vraw_ref, pos_ref, w1_ref, w2_ref, g_ref,
                     kc_ref, vc_ref, ks_ref, vs_ref, kw_ref, vw_ref):
    half = NSA_CMP_STRIDE * HEAD_DIM

    def compress(x, idx):
        w1 = w1_ref[idx]
        xb = x.astype(BF16)
        a = _dot(xb, w1[:half, :])
        b = _dot(xb, w1[half:, :])
        bias = _dot(pos_ref[idx].astype(BF16), w1)[0:1, :]
        pre = a + pltpu.roll(b, b.shape[0] - 1, 0) + bias
        return _dot(_gelu(pre).astype(BF16), w2_ref[idx])

    def with_pos(k, pos):
        lane = _iota((k.shape[0], HEAD_DIM), 1)
        extra = jnp.where(lane == 0, (pos // 256).astype(F32), jnp.where(lane == 1, (pos % 256).astype(F32), 0.0))
        return jnp.concatenate([k, extra], axis=1).astype(BF16)

    def with_one_t(v):
        lane = _iota((v.shape[0], HEAD_DIM), 1)
        return jnp.concatenate([v, jnp.where(lane == 0, 1.0, 0.0)], axis=1).T.astype(BF16)

    ng = xk_ref.shape[1]
    cend = _iota((ng, 1), 0) * NSA_CMP_STRIDE + (NSA_CMP_LEN - 1)
    kc_ref[0] = with_pos(_rms(compress(xk_ref[0], 0)) * g_ref[1:2, :], cend)
    vc_ref[0] = with_one_t(compress(xv_ref[0], 1))
    kraw = kraw_ref[0]
    tpos = _iota((kraw.shape[0], 1), 0)
    ks_ref[0] = with_pos(kraw[:, :HEAD_DIM], tpos)
    kw_ref[0] = with_pos(kraw[:, HEAD_DIM:], tpos)
    for c in range(vs_ref.shape[1]):
        vs_ref[0, c] = with_one_t(vraw_ref[0, c * NSA_KCHUNK:(c + 1) * NSA_KCHUNK, :HEAD_DIM])
    for c in range(vw_ref.shape[1]):
        vw_ref[0, c] = with_one_t(vraw_ref[0, c * Q_BLOCK:(c + 1) * Q_BLOCK, HEAD_DIM:])


NSA_KCHUNK = 2 * Q_BLOCK


def _nsa_prep(proj3, pos, w1, w2, g):
    B, S, _ = proj3.shape
    ng = S // NSA_CMP_STRIDE
    gw = NSA_CMP_STRIDE * HEAD_DIM
    xk = proj3[:, :, COL_NCMP:COL_NCMP + HEAD_DIM].reshape(B, ng, gw)
    xv = proj3[:, :, COL_NCMP + HEAD_DIM:COL_NCMP + 2 * HEAD_DIM].reshape(B, ng, gw)
    pos8 = jnp.broadcast_to(pos.reshape(2, 1, NSA_CMP_LEN * HEAD_DIM), (2, 8, NSA_CMP_LEN * HEAD_DIM))
    small = jax.ShapeDtypeStruct((B, ng, LANE), BF16)
    small_t = jax.ShapeDtypeStruct((B, LANE, ng), BF16)
    big = jax.ShapeDtypeStruct((B, S, LANE), BF16)
    vs_t = jax.ShapeDtypeStruct((B, S // NSA_KCHUNK, LANE, NSA_KCHUNK), BF16)
    vw_t = jax.ShapeDtypeStruct((B, S // Q_BLOCK, LANE, Q_BLOCK), BF16)
    small_spec = pl.BlockSpec((1, ng, LANE), lambda b: (b, 0, 0))
    small_t_spec = pl.BlockSpec((1, LANE, ng), lambda b: (b, 0, 0))
    big_spec = pl.BlockSpec((1, S, LANE), lambda b: (b, 0, 0))
    vs_spec = pl.BlockSpec((1, S // NSA_KCHUNK, LANE, NSA_KCHUNK), lambda b: (b, 0, 0, 0))
    vw_spec = pl.BlockSpec((1, S // Q_BLOCK, LANE, Q_BLOCK), lambda b: (b, 0, 0, 0))
    return pl.pallas_call(
        _nsa_prep_kernel,
        grid=(B,),
        in_specs=[
            pl.BlockSpec((1, ng, gw), lambda b: (b, 0, 0)),
            pl.BlockSpec((1, ng, gw), lambda b: (b, 0, 0)),
            pl.BlockSpec((1, S, LANE), lambda b: (b, 0, COL_NK // LANE)),
            pl.BlockSpec((1, S, LANE), lambda b: (b, 0, COL_NV // LANE)),
            pl.BlockSpec((2, 8, NSA_CMP_LEN * HEAD_DIM), lambda b: (0, 0, 0)),
            pl.BlockSpec((2, NSA_CMP_LEN * HEAD_DIM, NSA_CMP_HIDDEN), lambda b: (0, 0, 0)),
            pl.BlockSpec((2, NSA_CMP_HIDDEN, HEAD_DIM), lambda b: (0, 0, 0)),
            pl.BlockSpec((4, HEAD_DIM), lambda b: (0, 0)),
        ],
        out_specs=[small_spec, small_t_spec, big_spec, vs_spec, big_spec, vw_spec],
        out_shape=[small, small_t, big, vs_t, big, vw_t],
        compiler_params=_cparams(("parallel",)),
    )(xk, xv, proj3, proj3, pos8, w1.astype(BF16), w2.astype(BF16), g)


def _nsa_kernel(q_ref, gate_ref, kc_ref, vc_ref, ks_ref, vs_ref, kw_ref, vw_ref, o_ref, sel_ref, *, seq):
    i = pl.program_id(1)
    t0 = i * Q_BLOCK
    q_t = q_ref[0].T
    sub = _iota((HEAD_DIM, Q_BLOCK), 0)
    q = jnp.concatenate(
        [jnp.concatenate([q_t[h * HEAD_DIM:(h + 1) * HEAD_DIM],
                          jnp.where(sub == 0, NSA_SLOPES[h] * 256.0, jnp.where(sub == 1, NSA_SLOPES[h], 0.0))],
                         axis=0) for h in range(N_HEADS)], axis=1).astype(BF16)
    tq = t0 + _iota((1, Q_BLOCK), 1)

    def masked(s, ok):
        return jnp.concatenate([jnp.where(ok, s[:, h * Q_BLOCK:(h + 1) * Q_BLOCK], NEG_INF)
                                for h in range(N_HEADS)], axis=1)

    def heads(x):
        return jnp.concatenate([x] * N_HEADS, axis=1)

    ncmp = kc_ref.shape[1]
    cend = _iota((ncmp, 1), 0) * NSA_CMP_STRIDE + (NSA_CMP_LEN - 1)
    s = masked(_dot(kc_ref[0], q), cend <= tq)
    e = jnp.exp(s - jnp.max(s, axis=0, keepdims=True))
    col_ok = heads(jnp.where(tq >= NSA_CMP_LEN - 1, 1.0, 0.0))
    p = e * (col_ok / jnp.sum(e, axis=0, keepdims=True))
    o_cmp = _dot(vc_ref[0], p.astype(BF16))[:HEAD_DIM]

    nsel = seq // NSA_SEL_LEN
    psum = p[:, :Q_BLOCK] + p[:, Q_BLOCK:2 * Q_BLOCK] + p[:, 2 * Q_BLOCK:3 * Q_BLOCK] + p[:, 3 * Q_BLOCK:]
    jj = _iota((nsel, ncmp), 0)
    cc = _iota((nsel, ncmp), 1)
    ratio = NSA_SEL_LEN // NSA_CMP_STRIDE
    cover_t = jnp.where((cc < ratio * jj + ratio) & (cc * NSA_CMP_STRIDE + NSA_CMP_LEN > jj * NSA_SEL_LEN),
                        1.0, 0.0).astype(BF16)
    imp = _dot_rsplit(cover_t, psum)
    jrow = _iota((nsel, Q_BLOCK), 0)
    cur = (t0 + _iota((nsel, Q_BLOCK), 1)) // NSA_SEL_LEN
    forced = (jrow == 0) | (jrow == cur) | (jrow == cur - 1)
    impv = jnp.where(jrow <= cur, imp + jnp.where(forced, FORCE_BONUS, 0.0), NEG_INF)
    rank = jnp.zeros((nsel, Q_BLOCK), F32)
    for ii in range(nsel):
        row = impv[ii:ii + 1, :]
        tie = jnp.where(jrow > ii, 1.0, 0.0)
        rank = rank + jnp.where(row > impv, 1.0, jnp.where(row == impv, tie, 0.0))
    sel_ref[...] = jnp.where((rank < min(NSA_TOPN, nsel)) & (jrow <= cur), 1.0, 0.0)

    kpos_col = _iota((NSA_KCHUNK, 1), 0)
    per_chunk = NSA_KCHUNK // NSA_SEL_LEN

    def raw_scores(c):
        return _dot(ks_ref[0, pl.ds(pl.multiple_of(c * NSA_KCHUNK, NSA_KCHUNK), NSA_KCHUNK), :], q)

    def chunk_mask(c):
        picked = jnp.concatenate(
            [jnp.broadcast_to(sel_ref[pl.ds(c * per_chunk + b, 1), :], (NSA_SEL_LEN, Q_BLOCK))
             for b in range(per_chunk)], axis=0)
        return (picked > 0.5) & (c * NSA_KCHUNK + kpos_col <= tq)

    n_steps = (t0 + Q_BLOCK - 1) // (2 * NSA_KCHUNK) + 1

    def sel_body(it, carry):
        m_i, acc = carry
        sa = masked(raw_scores(2 * it), chunk_mask(2 * it))
        sb = masked(raw_scores(2 * it + 1), chunk_mask(2 * it + 1))
        m_new = jnp.maximum(m_i, jnp.maximum(jnp.max(sa, axis=0, keepdims=True), jnp.max(sb, axis=0, keepdims=True)))
        pa = jnp.exp(sa - m_new).astype(BF16)
        pb = jnp.exp(sb - m_new).astype(BF16)
        return m_new, jnp.exp(m_i - m_new) * acc + _dot(vs_ref[0, 2 * it], pa) + _dot(vs_ref[0, 2 * it + 1], pb)

    cols = N_HEADS * Q_BLOCK
    init = (jnp.full((1, cols), NEG_INF, F32), jnp.zeros((LANE, cols), F32))
    _, acc_s = lax.fori_loop(0, n_steps, sel_body, init)
    o_sel = acc_s[:HEAD_DIM] / acc_s[HEAD_DIM:HEAD_DIM + 1]

    nwb = NSA_WIN // Q_BLOCK + 1
    span = nwb * Q_BLOCK
    b0 = jnp.maximum(i - NSA_WIN // Q_BLOCK, 0)
    start = pl.multiple_of(b0 * Q_BLOCK, Q_BLOCK)
    dw = tq - (start + _iota((span, 1), 0))
    sw = masked(_dot(kw_ref[0, pl.ds(start, span), :], q), (dw >= 0) & (dw < NSA_WIN))
    pw = jnp.exp(sw - jnp.max(sw, axis=0, keepdims=True)).astype(BF16)
    rw = _dot(vw_ref[0, b0], pw[:Q_BLOCK])
    for j in range(1, nwb):
        rw = rw + _dot(vw_ref[0, b0 + j], pw[j * Q_BLOCK:(j + 1) * Q_BLOCK])
    o_win = rw[:HEAD_DIM] / rw[HEAD_DIM:HEAD_DIM + 1]

    gate = _sigmoid(gate_ref[0].T)
    outs = []
    for h in range(N_HEADS):
        c = slice(h * Q_BLOCK, (h + 1) * Q_BLOCK)
        outs.append(gate[3 * h:3 * h + 1] * o_cmp[:, c] + gate[3 * h + 1:3 * h + 2] * o_sel[:, c]
                    + gate[3 * h + 2:3 * h + 3] * o_win[:, c])
    o_ref[0] = jnp.concatenate(outs, axis=0).T


def _nsa_attention(proj3, kc, vc, ks, vs, kw, vw):
    B, S, _ = proj3.shape
    nb = S // Q_BLOCK
    ng = kc.shape[1]
    assert S >= NSA_WIN + Q_BLOCK and S % (2 * NSA_KCHUNK) == 0
    small_spec = pl.BlockSpec((1, ng, LANE), lambda b, i: (b, 0, 0))
    small_t_spec = pl.BlockSpec((1, LANE, ng), lambda b, i: (b, 0, 0))
    big_spec = pl.BlockSpec((1, S, LANE), lambda b, i: (b, 0, 0))
    vs_spec = pl.BlockSpec((1, S // NSA_KCHUNK, LANE, NSA_KCHUNK), lambda b, i: (b, 0, 0, 0))
    vw_spec = pl.BlockSpec((1, S // Q_BLOCK, LANE, Q_BLOCK), lambda b, i: (b, 0, 0, 0))
    return pl.pallas_call(
        functools.partial(_nsa_kernel, seq=S),
        grid=(B, nb),
        in_specs=[
            pl.BlockSpec((1, Q_BLOCK, GROUP_WIDTH), lambda b, i: (b, i, COL_NQ // GROUP_WIDTH)),
            pl.BlockSpec((1, Q_BLOCK, LANE), lambda b, i: (b, i, COL_NG // LANE)),
            small_spec, small_t_spec, big_spec, vs_spec, big_spec, vw_spec,
        ],
        out_specs=pl.BlockSpec((1, Q_BLOCK, GROUP_WIDTH), lambda b, i: (b, i, 0)),
        out_shape=jax.ShapeDtypeStruct((B, S, GROUP_WIDTH), F32),
        scratch_shapes=[pltpu.VMEM((S // NSA_SEL_LEN, Q_BLOCK), F32)],
        compiler_params=_cparams(("parallel", "parallel")),
    )(proj3, proj3, kc, vc, ks, vs, kw, vw)


GLA_BLOCK = 128
GLA_W = N_HEADS * GLA_DK


def _gla_kernel(q_ref, k_ref, v_ref, a_ref, r_ref, wa_ref, ba_ref, g_ref, o_ref, st_ref):
    nsub = GLA_BLOCK // GLA_CHUNK

    @pl.when(pl.program_id(1) == 0)
    def _():
        st_ref[...] = jnp.zeros_like(st_ref)

    q = q_ref[0] * (GLA_DK ** -0.5)
    k = k_ref[0]
    v = v_ref[0]
    z = _dot3(a_ref[0], wa_ref[...]) + ba_ref[...]
    log_a = (jnp.minimum(z, 0.0) - jnp.log(1.0 + jnp.exp(-jnp.abs(z)))) * (1.0 / GLA_TAU)
    ri = _iota((GLA_BLOCK, GLA_BLOCK), 0)
    ci = _iota((GLA_BLOCK, GLA_BLOCK), 1)
    same = (ri // GLA_CHUNK) == (ci // GLA_CHUNK)
    b_loc = _dot_rsplit(jnp.where(same & (ci <= ri), 1.0, 0.0).astype(BF16), log_a)
    b_tot = _dot_rsplit(jnp.where(same, 1.0, 0.0).astype(BF16), log_a)
    qd = q * jnp.exp(b_loc)
    kd = k * jnp.exp(b_tot - b_loc)

    ind_kv = jnp.where(_iota((GLA_W, GROUP_WIDTH), 0) // GLA_DK == _iota((GLA_W, GROUP_WIDTH), 1) // GLA_DV,
                       1.0, 0.0).astype(BF16)
    rsub = _iota((GLA_BLOCK, 1), 0) % GLA_CHUNK
    o = jnp.zeros((GLA_BLOCK, GROUP_WIDTH), F32)
    for s in range(GLA_CHUNK):
        if s == 0:
            ks_, bs_, vs_ = k, b_loc, v
        else:
            ks_, bs_, vs_ = pltpu.roll(k, s, 0), pltpu.roll(b_loc, s, 0), pltpu.roll(v, s, 0)
        dec = jnp.where(rsub >= s, jnp.exp(jnp.minimum(b_loc - bs_, 0.0)), 0.0)
        w = _dot((q * ks_ * dec).astype(BF16), ind_kv)
        o = o + w * vs_

    v_t = v.T.astype(BF16)
    head_ok = _iota((GROUP_WIDTH, GLA_W), 0) // GLA_DV == _iota((GROUP_WIDTH, GLA_W), 1) // GLA_DK
    rblk = _iota((GLA_BLOCK, 1), 0) // GLA_CHUNK
    st = st_ref[...]
    for c in range(nsub):
        in_c = rblk == c
        o = o + _dot_nt(jnp.where(in_c, qd, 0.0).astype(BF16), st.astype(BF16))
        decay = jnp.exp(b_tot[c * GLA_CHUNK:c * GLA_CHUNK + 1, :])
        upd = _dot(v_t, jnp.where(in_c, kd, 0.0).astype(BF16))
        st = st * decay + jnp.where(head_ok, upd, 0.0)
    st_ref[...] = st

    ind_vv = jnp.where(_iota((GROUP_WIDTH, GROUP_WIDTH), 0) // GLA_DV == _iota((GROUP_WIDTH, GROUP_WIDTH), 1) // GLA_DV,
                       1.0, 0.0).astype(BF16)
    ms = _dot_lsplit(o * o, ind_vv) * (1.0 / GLA_DV)
    r = r_ref[0]
    o_ref[0] = o * lax.rsqrt(ms + EPS) * g_ref[...] * (r * _sigmoid(r))


def _gla(proj3, wa2, ba, norm_g):
    B, S, _ = proj3.shape
    nb = S // GLA_BLOCK
    wa_pad = jnp.zeros((LANE, GLA_W), F32).at[:GLA_RANK].set(wa2)
    g_t = jnp.tile(norm_g.reshape(1, GLA_DV), (1, N_HEADS))

    def col(c, w):
        return pl.BlockSpec((1, GLA_BLOCK, w), lambda b, j: (b, j, c // w))

    return pl.pallas_call(
        _gla_kernel,
        grid=(B, nb),
        in_specs=[col(COL_GQ, LANE), col(COL_GK, LANE), col(COL_GV, GROUP_WIDTH), col(COL_GA, LANE),
                  col(COL_GR, GROUP_WIDTH),
                  pl.BlockSpec((LANE, GLA_W), lambda b, j: (0, 0)),
                  pl.BlockSpec((1, GLA_W), lambda b, j: (0, 0)),
                  pl.BlockSpec((1, GROUP_WIDTH), lambda b, j: (0, 0))],
        out_specs=pl.BlockSpec((1, GLA_BLOCK, GROUP_WIDTH), lambda b, j: (b, j, 0)),
        out_shape=jax.ShapeDtypeStruct((B, S, GROUP_WIDTH), F32),
        scratch_shapes=[pltpu.VMEM((GROUP_WIDTH, GLA_W), F32)],
        compiler_params=_cparams(("parallel", "arbitrary")),
    )(proj3, proj3, proj3, proj3, proj3, wa_pad, ba.reshape(1, GLA_W), g_t)


S5_CW = S5_CHUNK * S5_GROUP
S5_SW = 2 * S5_STATE


def _s5_matrices(a_re, a_im, b_re, b_im, c_re, c_im, log_dt):
    hp = lax.Precision.HIGHEST
    L = S5_CHUNK
    dt = jnp.exp(log_dt)[:, None]
    lam_re, lam_im = dt * a_re, dt * a_im
    tau = jnp.arange(L + 1, dtype=F32)[:, None, None]
    mag = jnp.exp(tau * lam_re)
    p_re, p_im = mag * jnp.cos(tau * lam_im), mag * jnp.sin(tau * lam_im)
    den = a_re * a_re + a_im * a_im
    f_re = ((p_re[1] - 1.0) * a_re + p_im[1] * a_im) / den
    f_im = (p_im[1] * a_re - (p_re[1] - 1.0) * a_im) / den
    bb_re = f_re[..., None] * b_re - f_im[..., None] * b_im
    bb_im = f_re[..., None] * b_im + f_im[..., None] * b_re
    pb_re = p_re[..., None] * bb_re - p_im[..., None] * bb_im
    pb_im = p_re[..., None] * bb_im + p_im[..., None] * bb_re
    kern = (jnp.einsum('gon,tgnc->tgoc', c_re, pb_re, precision=hp)
            - jnp.einsum('gon,tgnc->tgoc', c_im, pb_im, precision=hp))
    s_idx = jnp.arange(L)[:, None]
    t_idx = jnp.arange(L)[None, :]
    lag = jnp.clip(t_idx - s_idx, 0, L)
    toep = jnp.where((t_idx >= s_idx)[..., None, None, None], kern[lag], 0.0)
    toep = toep.transpose(2, 0, 4, 1, 3).reshape(S5_GROUPS, S5_CW, S5_CW)
    rev = (L - 1 - jnp.arange(L))
    bin_re = pb_re[rev].transpose(1, 0, 3, 2).reshape(S5_GROUPS, S5_CW, S5_STATE)
    bin_im = pb_im[rev].transpose(1, 0, 3, 2).reshape(S5_GROUPS, S5_CW, S5_STATE)
    b_in = jnp.concatenate([bin_re, bin_im], axis=-1)
    pt_re, pt_im = p_re[1:], p_im[1:]
    co_re = c_re[None] * pt_re[:, :, None, :] - c_im[None] * pt_im[:, :, None, :]
    co_im = -c_re[None] * pt_im[:, :, None, :] - c_im[None] * pt_re[:, :, None, :]
    c_out = jnp.concatenate([co_re.transpose(1, 3, 0, 2), co_im.transpose(1, 3, 0, 2)], axis=1)
    c_out = c_out.reshape(S5_GROUPS, S5_SW, S5_CW)
    a_chunk = jnp.stack([p_re[L].reshape(1, -1), p_im[L].reshape(1, -1)], axis=0)
    return jnp.concatenate([toep, b_in], axis=-1).astype(BF16), c_out.astype(BF16), a_chunk


def _s5_in_kernel(u_ref, tb_ref, y_ref, v_ref):
    r = _dot(u_ref[0].astype(BF16), tb_ref[0])
    y_ref[0] = r[:, :S5_CW]
    v_ref[0] = r[:, S5_CW:]


def _s5_scan_kernel(v_ref, a_ref, x_ref):
    nb = v_ref.shape[2]
    a_r = jnp.broadcast_to(a_ref[0], (nb, a_ref.shape[2]))
    a_i = jnp.broadcast_to(a_ref[1], (nb, a_ref.shape[2]))

    def body(m, carry):
        x_r, x_i = carry
        x_ref[m, 0] = x_r
        x_ref[m, 1] = x_i
        return a_r * x_r - a_i * x_i + v_ref[m, 0], a_r * x_i + a_i * x_r + v_ref[m, 1]

    zero = jnp.zeros(a_r.shape, F32)
    lax.fori_loop(0, v_ref.shape[0], body, (zero, zero))


def _s5_out_kernel(y_ref, x_ref, c_ref, o_ref):
    o_ref[0] = y_ref[0] + _dot(x_ref[0].astype(BF16), c_ref[0])


def _s5_linear(u3, a_re, a_im, b_re, b_im, c_re, c_im, log_dt):
    B, S, _ = u3.shape
    M = S // S5_CHUNK
    G = S5_GROUPS
    tb, c_out, a_chunk = _s5_matrices(a_re, a_im, b_re, b_im, c_re, c_im, log_dt)
    ug = u3.astype(BF16).reshape(B, M, S5_CHUNK, G, S5_GROUP).transpose(3, 0, 1, 2, 4).reshape(G, B * M, S5_CW)
    y_in, v = pl.pallas_call(
        _s5_in_kernel,
        grid=(G,),
        in_specs=[pl.BlockSpec((1, B * M, S5_CW), lambda g: (g, 0, 0)),
                  pl.BlockSpec((1, S5_CW, S5_CW + S5_SW), lambda g: (g, 0, 0))],
        out_specs=[pl.BlockSpec((1, B * M, S5_CW), lambda g: (g, 0, 0)),
                   pl.BlockSpec((1, B * M, S5_SW), lambda g: (g, 0, 0))],
        out_shape=[jax.ShapeDtypeStruct((G, B * M, S5_CW), F32),
                   jax.ShapeDtypeStruct((G, B * M, S5_SW), F32)],
        compiler_params=_cparams(("parallel",)),
    )(ug, tb)
    gn = G * S5_STATE
    v_t = v.reshape(G, B, M, 2, S5_STATE).transpose(2, 3, 1, 0, 4).reshape(M, 2, B, gn)
    lw = 256
    x_t = pl.pallas_call(
        _s5_scan_kernel,
        grid=(gn // lw,),
        in_specs=[pl.BlockSpec((M, 2, B, lw), lambda j: (0, 0, 0, j)),
                  pl.BlockSpec((2, 1, lw), lambda j: (0, 0, j))],
        out_specs=pl.BlockSpec((M, 2, B, lw), lambda j: (0, 0, 0, j)),
        out_shape=jax.ShapeDtypeStruct((M, 2, B, gn), F32),
        compiler_params=_cparams(("parallel",)),
    )(v_t, a_chunk)
    xg = x_t.reshape(M, 2, B, G, S5_STATE).transpose(3, 2, 0, 1, 4).reshape(G, B * M, S5_SW)
    y = pl.pallas_call(
        _s5_out_kernel,
        grid=(G,),
        in_specs=[pl.BlockSpec((1, B * M, S5_CW), lambda g: (g, 0, 0)),
                  pl.BlockSpec((1, B * M, S5_SW), lambda g: (g, 0, 0)),
                  pl.BlockSpec((1, S5_SW, S5_CW), lambda g: (g, 0, 0))],
        out_specs=pl.BlockSpec((1, B * M, S5_CW), lambda g: (g, 0, 0)),
        out_shape=jax.ShapeDtypeStruct((G, B * M, S5_CW), F32),
        compiler_params=_cparams(("parallel",)),
    )(y_in, xg, c_out)
    return y.reshape(G, B, M, S5_CHUNK, S5_GROUP).transpose(1, 2, 3, 0, 4).reshape(B, S, GROUP_WIDTH)


def _mixout_kernel(dil_ref, nsa_ref, gla_ref, s5y_ref, u_ref, x_ref,
                   s5d_ref, gw_ref, gb_ref, ng_ref, wo_ref, o_ref):
    u = u_ref[...]
    hg = _gelu(s5y_ref[...] + s5d_ref[...] * u)
    y_s5 = hg * _sigmoid(_dot(hg.astype(BF16), gw_ref[...]) + gb_ref[...])
    parts = [dil_ref[...], nsa_ref[...], gla_ref[...], y_s5]
    mix = jnp.concatenate([(_rms(parts[n]) * ng_ref[n:n + 1, :]).astype(BF16) for n in range(4)], axis=1)
    o_ref[...] = x_ref[...] + _dot(mix, wo_ref[...])


def _mixout(y_dil, y_nsa, y_gla, y_s5, proj, x2, s5_d, glu_w, glu_b, out_g, w_out):
    T = x2.shape[0]
    tm = 512

    def row(w, c=0):
        return pl.BlockSpec((tm, w), lambda i: (i, c))

    def const(shape):
        return pl.BlockSpec(shape, lambda i: (0,) * len(shape))

    return pl.pallas_call(
        _mixout_kernel,
        grid=(T // tm,),
        in_specs=[row(GROUP_WIDTH), row(GROUP_WIDTH), row(GROUP_WIDTH),
                  row(GROUP_WIDTH), row(GROUP_WIDTH, COL_SU // GROUP_WIDTH), row(D_MODEL),
                  const((1, GROUP_WIDTH)), const((GROUP_WIDTH, GROUP_WIDTH)), const((1, GROUP_WIDTH)),
                  const((4, GROUP_WIDTH)), const((D_MODEL, D_MODEL))],
        out_specs=row(D_MODEL),
        out_shape=jax.ShapeDtypeStruct((T, D_MODEL), F32),
        compiler_params=_cparams(("parallel",)),
    )(y_dil, y_nsa, y_gla, y_s5, proj, x2, s5_d.reshape(1, -1), glu_w.astype(BF16),
      glu_b.reshape(1, -1), out_g.reshape(4, GROUP_WIDTH), w_out.astype(BF16))


def _swiglu_chunks(h, wg_ref, wu_ref, wd_ref, acc, width, chunk, lead=()):
    for c in range(width // chunk):
        cs = slice(c * chunk, (c + 1) * chunk)
        a = _dot(h, wg_ref[lead + (slice(None), cs)])
        u = _dot(h, wu_ref[lead + (slice(None), cs)])
        acc = acc + _dot((a * _sigmoid(a) * u).astype(BF16), wd_ref[lead + (cs, slice(None))])
    return acc


def _ffn_kernel(x_ref, g_ref, wg_ref, wu_ref, wd_ref, o_ref):
    x = x_ref[...]
    h = (_rms(x) * g_ref[...]).astype(BF16)
    o_ref[...] = _swiglu_chunks(h, wg_ref, wu_ref, wd_ref, x, D_FF_PAD, D_FF_PAD // 2)


def _resident(shape, index_map):
    return pl.BlockSpec(shape, index_map, pipeline_mode=pl.Buffered(1))


def _ffn(x2, g, wg, wu, wd):
    T = x2.shape[0]
    tm = 512
    pad = D_FF_PAD - D_FF
    wg_p = jnp.pad(wg, ((0, 0), (0, pad))).astype(BF16)
    wu_p = jnp.pad(wu, ((0, 0), (0, pad))).astype(BF16)
    wd_p = jnp.pad(wd, ((0, pad), (0, 0))).astype(BF16)
    return pl.pallas_call(
        _ffn_kernel,
        grid=(T // tm,),
        in_specs=[pl.BlockSpec((tm, D_MODEL), lambda i: (i, 0)),
                  pl.BlockSpec((1, D_MODEL), lambda i: (0, 0)),
                  _resident((D_MODEL, D_FF_PAD), lambda i: (0, 0)),
                  _resident((D_MODEL, D_FF_PAD), lambda i: (0, 0)),
                  _resident((D_FF_PAD, D_MODEL), lambda i: (0, 0))],
        out_specs=pl.BlockSpec((tm, D_MODEL), lambda i: (i, 0)),
        out_shape=jax.ShapeDtypeStruct((T, D_MODEL), F32),
        compiler_params=_cparams(("parallel",)),
    )(x2, g.reshape(1, D_MODEL), wg_p, wu_p, wd_p)


MOE_TILE = 512
DSP_TILE = 256
CMB_TILE = 256


def _router_kernel(x_ref, g_ref, w_ref, b_ref, r_ref):
    h = _rms(x_ref[...]) * g_ref[...]
    lane = _iota((h.shape[0], LANE), 1)
    logits = jnp.where(lane < N_EXPERTS, _dot3(h, w_ref[...]) + b_ref[...], -jnp.inf)
    m1 = jnp.max(logits, axis=-1, keepdims=True)
    i1 = jnp.min(jnp.where(logits == m1, lane, LANE), axis=-1, keepdims=True)
    rest = jnp.where(lane == i1, -jnp.inf, logits)
    m2 = jnp.max(rest, axis=-1, keepdims=True)
    i2 = jnp.min(jnp.where(rest == m2, lane, LANE), axis=-1, keepdims=True)
    e2 = jnp.exp(m2 - m1)
    w1 = 1.0 / (1.0 + e2)
    w2 = e2 / (1.0 + e2)
    r_ref[...] = jnp.where(lane == 0, i1.astype(F32),
                           jnp.where(lane == 1, i2.astype(F32),
                                     jnp.where(lane == 2, w1, jnp.where(lane == 3, w2, 0.0))))


def _router(x2, g, rw, rb):
    T = x2.shape[0]
    tm = 512
    rw_p = jnp.pad(rw, ((0, 0), (0, LANE - N_EXPERTS)))
    rb_p = jnp.pad(rb, (0, LANE - N_EXPERTS)).reshape(1, LANE)
    return pl.pallas_call(
        _router_kernel,
        grid=(T // tm,),
        in_specs=[pl.BlockSpec((tm, D_MODEL), lambda i: (i, 0)),
                  pl.BlockSpec((1, D_MODEL), lambda i: (0, 0)),
                  pl.BlockSpec((D_MODEL, LANE), lambda i: (0, 0)),
                  pl.BlockSpec((1, LANE), lambda i: (0, 0))],
        out_specs=pl.BlockSpec((tm, LANE), lambda i: (i, 0)),
        out_shape=jax.ShapeDtypeStruct((T, LANE), F32),
        compiler_params=_cparams(("parallel",)),
    )(x2, g.reshape(1, D_MODEL), rw_p, rb_p)


def _smem_rows(width, nt, shift):
    return pl.BlockSpec((1, 1, width), lambda i, *_: (jnp.clip(i + shift, 0, nt - 1), 0, 0),
                        memory_space=pltpu.SMEM)


def _scatter_copy(buf, slot, r, dst_hbm, row, sem, k):
    return pltpu.make_async_copy(buf.at[slot, pl.ds(r, 1), :], dst_hbm.at[pl.ds(row, 1), :], sem.at[slot, k, r])


def _dispatch_kernel(p1_ref, p2_ref, q1_ref, q2_ref, x_ref, g_ref, xs_in, xs_hbm, hbuf, sem):
    del xs_in
    i = pl.program_id(0)
    nt = pl.num_programs(0)
    slot = i % 2
    hbuf[slot] = _rms(x_ref[...]) * g_ref[...]
    for r in range(DSP_TILE):
        _scatter_copy(hbuf, slot, r, xs_hbm, p1_ref[0, 0, r], sem, 0).start()
        _scatter_copy(hbuf, slot, r, xs_hbm, p2_ref[0, 0, r], sem, 1).start()

    def wait_all(pa, pb, s):
        for r in range(DSP_TILE):
            _scatter_copy(hbuf, s, r, xs_hbm, pa[0, 0, r], sem, 0).wait()
            _scatter_copy(hbuf, s, r, xs_hbm, pb[0, 0, r], sem, 1).wait()

    @pl.when(i > 0)
    def _():
        wait_all(q1_ref, q2_ref, 1 - slot)

    @pl.when(i == nt - 1)
    def _():
        wait_all(p1_ref, p2_ref, slot)


def _moe_dispatch(x2, g, pos, n_rows):
    T = x2.shape[0]
    nt = T // DSP_TILE
    p1 = pos[:, 0].reshape(nt, 1, DSP_TILE)
    p2 = pos[:, 1].reshape(nt, 1, DSP_TILE)
    return pl.pallas_call(
        _dispatch_kernel,
        grid=(nt,),
        in_specs=[_smem_rows(DSP_TILE, nt, 0), _smem_rows(DSP_TILE, nt, 0),
                  _smem_rows(DSP_TILE, nt, -1), _smem_rows(DSP_TILE, nt, -1),
                  pl.BlockSpec((DSP_TILE, D_MODEL), lambda i: (i, 0)),
                  pl.BlockSpec((1, D_MODEL), lambda i: (0, 0)),
                  pl.BlockSpec(memory_space=pl.ANY)],
        out_specs=pl.BlockSpec(memory_space=pl.ANY),
        out_shape=jax.ShapeDtypeStruct((n_rows, D_MODEL), F32),
        scratch_shapes=[pltpu.VMEM((2, DSP_TILE, D_MODEL), F32), pltpu.SemaphoreType.DMA((2, 2, DSP_TILE))],
        input_output_aliases={6: 0},
        compiler_params=_cparams(("arbitrary",)),
    )(p1, p2, p1, p2, x2, g.reshape(1, D_MODEL), jnp.zeros((n_rows, D_MODEL), F32))


def _moe_kernel(texp_ref, x_ref, wg_ref, wu_ref, wd_ref, y_ref):
    y_ref[...] = _swiglu_chunks(x_ref[...].astype(BF16), wg_ref, wu_ref, wd_ref,
                                jnp.zeros((MOE_TILE, D_MODEL), F32), D_FF_EXPERT, D_FF_EXPERT // 4, lead=(0,))


def _moe_experts(xs, tile_expert, wg, wu, wd):
    nt = tile_expert.shape[0]
    grid_spec = pltpu.PrefetchScalarGridSpec(
        num_scalar_prefetch=1,
        grid=(nt,),
        in_specs=[
            pl.BlockSpec((MOE_TILE, D_MODEL), lambda i, te: (i, 0)),
            _resident((1, D_MODEL, D_FF_EXPERT), lambda i, te: (te[i], 0, 0)),
            _resident((1, D_MODEL, D_FF_EXPERT), lambda i, te: (te[i], 0, 0)),
            _resident((1, D_FF_EXPERT, D_MODEL), lambda i, te: (te[i], 0, 0)),
        ],
        out_specs=pl.BlockSpec((MOE_TILE, D_MODEL), lambda i, te: (i, 0)),
    )
    return pl.pallas_call(
        _moe_kernel,
        grid_spec=grid_spec,
        out_shape=jax.ShapeDtypeStruct((nt * MOE_TILE, D_MODEL), F32),
        compiler_params=_cparams(("arbitrary",)),
    )(tile_expert, xs, wg.astype(BF16), wu.astype(BF16), wd.astype(BF16))


def _gather_copy(src_hbm, row, buf, slot, k, r, sem):
    return pltpu.make_async_copy(src_hbm.at[pl.ds(row, 1), :], buf.at[slot, k, pl.ds(r, 1), :], sem.at[slot, k, r])


def _combine_kernel(p1_ref, p2_ref, n1_ref, n2_ref, y_hbm, x_ref, r_ref, o_ref, ybuf, sem):
    i = pl.program_id(0)
    nt = pl.num_programs(0)
    slot = i % 2

    def issue(pa, pb, s):
        for r in range(CMB_TILE):
            _gather_copy(y_hbm, pa[0, 0, r], ybuf, s, 0, r, sem).start()
            _gather_copy(y_hbm, pb[0, 0, r], ybuf, s, 1, r, sem).start()

    @pl.when(i == 0)
    def _():
        issue(p1_ref, p2_ref, 0)

    @pl.when(i + 1 < nt)
    def _():
        issue(n1_ref, n2_ref, 1 - slot)

    for r in range(CMB_TILE):
        _gather_copy(y_hbm, p1_ref[0, 0, r], ybuf, slot, 0, r, sem).wait()
        _gather_copy(y_hbm, p2_ref[0, 0, r], ybuf, slot, 1, r, sem).wait()
    r = r_ref[...]
    o_ref[...] = x_ref[...] + r[:, 2:3] * ybuf[slot, 0] + r[:, 3:4] * ybuf[slot, 1]


def _moe_combine(y_sorted, pos, x2, route):
    T = x2.shape[0]
    nt = T // CMB_TILE
    p1 = pos[:, 0].reshape(nt, 1, CMB_TILE)
    p2 = pos[:, 1].reshape(nt, 1, CMB_TILE)
    return pl.pallas_call(
        _combine_kernel,
        grid=(nt,),
        in_specs=[_smem_rows(CMB_TILE, nt, 0), _smem_rows(CMB_TILE, nt, 0),
                  _smem_rows(CMB_TILE, nt, 1), _smem_rows(CMB_TILE, nt, 1),
                  pl.BlockSpec(memory_space=pl.ANY),
                  pl.BlockSpec((CMB_TILE, D_MODEL), lambda i: (i, 0)),
                  pl.BlockSpec((CMB_TILE, LANE), lambda i: (i, 0))],
        out_specs=pl.BlockSpec((CMB_TILE, D_MODEL), lambda i: (i, 0)),
        out_shape=jax.ShapeDtypeStruct((T, D_MODEL), F32),
        scratch_shapes=[pltpu.VMEM((2, 2, CMB_TILE, D_MODEL), F32),
                        pltpu.SemaphoreType.DMA((2, 2, CMB_TILE))],
        compiler_params=_cparams(("arbitrary",)),
    )(p1, p2, p1, p2, y_sorted, x2, route)


def _moe(x2, g, rw, rb, wg, wu, wd):
    T = x2.shape[0]
    route = _router(x2, g, rw, rb)
    experts = route[:, :2].astype(jnp.int32)
    onehot = (experts.reshape(-1)[:, None] == jnp.arange(N_EXPERTS)[None, :]).astype(jnp.int32)
    csum = jnp.cumsum(onehot, axis=0)
    counts = csum[-1]
    rank = jnp.sum(csum * onehot, axis=1) - 1
    padded = ((counts + MOE_TILE - 1) // MOE_TILE) * MOE_TILE
    ends = jnp.cumsum(padded)
    starts = ends - padded
    pos = (jnp.sum(starts[None, :] * onehot, axis=1) + rank).reshape(T, 2)
    nt = (2 * T) // MOE_TILE + N_EXPERTS
    tile_start = jnp.arange(nt, dtype=jnp.int32) * MOE_TILE
    tile_expert = jnp.minimum(jnp.sum(tile_start[:, None] >= ends[None, :], axis=1), N_EXPERTS - 1)
    xs = _moe_dispatch(x2, g, pos, nt * MOE_TILE)
    y_sorted = _moe_experts(xs, tile_expert.astype(jnp.int32), wg, wu, wd)
    return _moe_combine(y_sorted, pos, x2, route)


def _layer_mixers(x2, B, S, l, p):
    proj = _inproj(x2, p["norm1_g"][l], _pad_w_in(p["w_in"][l]), _head_gains(p["dil_qk_g"][l], p["nsa_qk_g"][l]))
    proj3 = proj.reshape(B, S, PROJ_PAD)
    y_dil = _dil_attention(proj3).reshape(B * S, GROUP_WIDTH)
    kc, vc, ks, vs, kw, vw = _nsa_prep(proj3, p["nsa_cmp_pos"][l], p["nsa_cmp_w1"][l], p["nsa_cmp_w2"][l],
                                       p["nsa_qk_g"][l])
    y_nsa = _nsa_attention(proj3, kc, vc, ks, vs, kw, vw).reshape(B * S, GROUP_WIDTH)
    y_gla = _gla(proj3, p["gla_wa2"][l], p["gla_ba"][l], p["gla_norm_g"][l]).reshape(B * S, GROUP_WIDTH)
    y_s5 = _s5_linear(proj3[:, :, COL_SU:COL_SU + GROUP_WIDTH], p["s5_a_re"][l], p["s5_a_im"][l],
                      p["s5_b_re"][l], p["s5_b_im"][l], p["s5_c_re"][l], p["s5_c_im"][l],
                      p["s5_log_dt"][l]).reshape(B * S, GROUP_WIDTH)
    return _mixout(y_dil, y_nsa, y_gla, y_s5, proj, x2, p["s5_d"][l], p["s5_glu_w"][l],
                   p["s5_glu_b"][l], p["out_norm_g"][l], p["w_out"][l])


def kernel(x, norm1_g, w_in, dil_qk_g, nsa_qk_g, nsa_cmp_pos, nsa_cmp_w1, nsa_cmp_w2, gla_wa2, gla_ba,
           gla_norm_g, s5_a_re, s5_a_im, s5_b_re, s5_b_im, s5_c_re, s5_c_im, s5_d, s5_log_dt, s5_glu_w,
           s5_glu_b, out_norm_g, w_out, norm2_g, ffn_w_gate, ffn_w_up, ffn_w_down, moe_router_w,
           moe_router_b, moe_w_gate, moe_w_up, moe_w_down):
    B, S, D = x.shape
    p = dict(norm1_g=norm1_g, w_in=w_in, dil_qk_g=dil_qk_g, nsa_qk_g=nsa_qk_g, nsa_cmp_pos=nsa_cmp_pos,
             nsa_cmp_w1=nsa_cmp_w1, nsa_cmp_w2=nsa_cmp_w2, gla_wa2=gla_wa2, gla_ba=gla_ba,
             gla_norm_g=gla_norm_g, s5_a_re=s5_a_re, s5_a_im=s5_a_im, s5_b_re=s5_b_re, s5_b_im=s5_b_im,
             s5_c_re=s5_c_re, s5_c_im=s5_c_im, s5_d=s5_d, s5_log_dt=s5_log_dt, s5_glu_w=s5_glu_w,
             s5_glu_b=s5_glu_b, out_norm_g=out_norm_g, w_out=w_out)
    x2 = x.reshape(B * S, D)
    depth = norm1_g.shape[0]
    for l in range(depth):
        x2 = _layer_mixers(x2, B, S, l, p)
        i = l // 2
        if l % 2 == 0:
            x2 = _ffn(x2, norm2_g[l], ffn_w_gate[i], ffn_w_up[i], ffn_w_down[i])
        else:
            x2 = _moe(x2, norm2_g[l], moe_router_w[i], moe_router_b[i], moe_w_gate[i], moe_w_up[i],
                      moe_w_down[i])
    return x2.reshape(B, S, D)
```
